```python
import math
import jax, jax.numpy as jnp
from jax import lax
import numpy as np

D_MODEL = 2048
BATCH = 1
SEQ = 8192
DEPTH = 1

HEAD_DIM = 128
DIFF_HEADS = D_MODEL // 512
DIFF_V_DIM = 2 * HEAD_DIM
FOX_HEADS = D_MODEL // 256
Q_BLOCK = 128
ROPE_THETA = 10000.0
N_GROUPS = 4
EXPERTS_PER_GROUP = 8
N_EXPERTS = N_GROUPS * EXPERTS_PER_GROUP
TOP_K = 2
EXPERT_FF = D_MODEL // 4
MOE_BLOCK = 128
LN_EPS = 1e-5
DEEPNORM_ALPHA = (2 * DEPTH) ** 0.25
DEEPNORM_BETA = (8 * DEPTH) ** -0.25

DIFF_QK_COLS = DIFF_HEADS * 2 * HEAD_DIM
DIFF_V_COLS = DIFF_HEADS * DIFF_V_DIM
FOX_QK_COLS = FOX_HEADS * HEAD_DIM
FOX_V_COLS = FOX_HEADS * HEAD_DIM
GATE_COLS = 2 * D_MODEL
IN_SIZES = (DIFF_QK_COLS, DIFF_QK_COLS, DIFF_V_COLS, FOX_QK_COLS, FOX_QK_COLS, FOX_V_COLS, FOX_HEADS, GATE_COLS)
IN_COLS = sum(IN_SIZES)
IN_SPLITS = tuple(np.cumsum(IN_SIZES)[:-1].tolist())
VALUE_SEGMENTS = (2, 5)

kernel_name = "hybrid_diffattn_fox_hmoe_deepnorm"


def _layer_norm(x, g, b):
    xf = x.astype(jnp.float32)
    mu = jnp.mean(xf, -1, keepdims=True)
    var = jnp.mean(jnp.square(xf - mu), -1, keepdims=True)
    y = (xf - mu) * lax.rsqrt(var + LN_EPS) * g.astype(jnp.float32) + b.astype(jnp.float32)
    return y.astype(x.dtype)


def _rope(t, cos, sin):
    half = t.shape[-1] // 2
    tf = t.astype(jnp.float32)
    t1, t2 = tf[..., :half], tf[..., half:]
    return jnp.concatenate([t1 * cos - t2 * sin, t1 * sin + t2 * cos], -1).astype(t.dtype)


def _heads(t, n):
    b, s, _ = t.shape
    return t.reshape(b, s, n, -1).transpose(0, 2, 1, 3)


def _to_blocks(t):
    b, h, s, e = t.shape
    return jnp.moveaxis(t.reshape(b, h, s // Q_BLOCK, Q_BLOCK, e), 2, 0)


def _from_blocks(t):
    nb, b, h, q, e = t.shape
    return jnp.moveaxis(t, 0, 2).reshape(b, h, nb * q, e).transpose(0, 2, 1, 3)


def _causal_block_attention(dq1, dq2, dk1, dk2, dv, lam, fq, fk, fv, fcum):
    b, hf, seq = fcum.shape
    nb = seq // Q_BLOCK
    k_pos = jnp.arange(seq)
    scale = HEAD_DIM ** -0.5
    fcum_q = jnp.moveaxis(fcum.reshape(b, hf, nb, Q_BLOCK), 2, 0)

    def probs(q, k, causal, bias=None):
        s = jnp.einsum("bhqd,bhkd->bhqk", q, k, preferred_element_type=jnp.float32) * scale
        if bias is not None:
            s = s + bias
        return jax.nn.softmax(jnp.where(causal, s, -jnp.inf), axis=-1)

    def one_block(args):
        i, q1, q2, qf, cq = args
        q_pos = i * Q_BLOCK + jnp.arange(Q_BLOCK)
        causal = q_pos[:, None] >= k_pos[None, :]
        a_diff = probs(q1, dk1, causal) - lam * probs(q2, dk2, causal)
        o_diff = jnp.einsum("bhqk,bhke->bhqe", a_diff.astype(dv.dtype), dv)
        decay = cq[..., :, None] - fcum[..., None, :]
        a_fox = probs(qf, fk, causal, decay)
        o_fox = jnp.einsum("bhqk,bhke->bhqe", a_fox.astype(fv.dtype), fv)
        return o_diff, o_fox

    o_diff, o_fox = lax.map(one_block, (jnp.arange(nb), _to_blocks(dq1), _to_blocks(dq2), _to_blocks(fq), fcum_q))
    return _from_blocks(o_diff), _from_blocks(o_fox)


def _hybrid_mixer(h, w_in, b_forget, lam_q1, lam_k1, lam_q2, lam_k2, diff_norm_g,
                  w_proj_diff, w_proj_fox, w_out, lam_init):
    f32 = jnp.float32
    b, s, _ = h.shape
    dq, dk, dv, fq, fk, fv, f_logit, gate_logit = jnp.split(h @ w_in, IN_SPLITS, axis=-1)
    inv_freq = ROPE_THETA ** (-jnp.arange(0, HEAD_DIM, 2, dtype=f32) / HEAD_DIM)
    ang = jnp.arange(s, dtype=f32)[:, None] * inv_freq[None, :]
    cos, sin = jnp.cos(ang), jnp.sin(ang)
    dq = _rope(_heads(dq, 2 * DIFF_HEADS), cos, sin)
    dk = _rope(_heads(dk, 2 * DIFF_HEADS), cos, sin)
    dq1, dq2 = dq[:, 0::2], dq[:, 1::2]
    dk1, dk2 = dk[:, 0::2], dk[:, 1::2]
    dv = _heads(dv, DIFF_HEADS)
    lam = (jnp.exp(jnp.sum(lam_q1.astype(f32) * lam_k1.astype(f32)))
           - jnp.exp(jnp.sum(lam_q2.astype(f32) * lam_k2.astype(f32))) + lam_init)
    fq, fk, fv = _heads(fq, FOX_HEADS), _heads(fk, FOX_HEADS), _heads(fv, FOX_HEADS)
    log_f = jax.nn.log_sigmoid(f_logit.astype(f32) + b_forget.astype(f32))
    fcum = jnp.cumsum(log_f, axis=1).transpose(0, 2, 1)
    o_diff, o_fox = _causal_block_attention(dq1, dq2, dk1, dk2, dv, lam, fq, fk, fv, fcum)
    of = o_diff.astype(f32)
    of = of * lax.rsqrt(jnp.mean(jnp.square(of), -1, keepdims=True) + LN_EPS) * diff_norm_g.astype(f32) * (1.0 - lam_init)
    u_diff = of.astype(h.dtype).reshape(b, s, DIFF_V_COLS) @ w_proj_diff
    u_fox = o_fox.reshape(b, s, FOX_V_COLS) @ w_proj_fox
    g_diff, g_fox = jnp.split(jax.nn.sigmoid(gate_logit), 2, axis=-1)
    return (g_diff * u_diff + g_fox * u_fox) @ w_out


def _hierarchical_moe(h, w_rg, b_rg, w_re, b_re, w_gate, w_up, w_down):
    f32 = jnp.float32
    b, s, d = h.shape
    t = b * s
    xt = h.reshape(t, d)
    p_group = jax.nn.softmax((xt @ w_rg).astype(f32) + b_rg.astype(f32), axis=-1)
    top_gp, top_g = lax.top_k(p_group, 1)
    exp_logits = jnp.einsum("td,gde->tge", xt, w_re).astype(f32) + b_re.astype(f32)
    sel_logits = exp_logits[jnp.arange(t), top_g[:, 0]]
    top_el, top_e = lax.top_k(sel_logits, TOP_K)
    weights = top_gp * jax.nn.softmax(top_el, axis=-1)
    expert_ids = top_g * EXPERTS_PER_GROUP + top_e
    a = t * TOP_K
    flat_e = expert_ids.reshape(a)
    flat_tok = jnp.repeat(jnp.arange(t), TOP_K)
    order = jnp.argsort(flat_e)
    s_e, s_tok, s_w = flat_e[order], flat_tok[order], weights.reshape(a)[order]
    counts = jnp.bincount(flat_e, length=N_EXPERTS)
    group_start = jnp.cumsum(counts) - counts
    padded = (counts + MOE_BLOCK - 1) // MOE_BLOCK * MOE_BLOCK
    pad_end = jnp.cumsum(padded)
    pad_start = pad_end - padded
    dest = pad_start[s_e] + jnp.arange(a) - group_start[s_e]
    n_blocks = -(-a // MOE_BLOCK) + N_EXPERTS
    rows = n_blocks * MOE_BLOCK
    xs = jnp.zeros((rows, d), xt.dtype).at[dest].set(xt[s_tok])
    block_expert = jnp.minimum(jnp.searchsorted(pad_end, jnp.arange(n_blocks) * MOE_BLOCK, side="right"), N_EXPERTS - 1)

    def expert_block(args):
        xb, e = args
        return (jax.nn.silu(xb @ w_gate[e]) * (xb @ w_up[e])) @ w_down[e]

    ys = lax.map(expert_block, (xs.reshape(n_blocks, MOE_BLOCK, d), block_expert)).reshape(rows, d)
    contrib = ys[dest] * s_w[:, None].astype(ys.dtype)
    return jax.ops.segment_sum(contrib, s_tok, num_segments=t).reshape(b, s, d)


def setup_inputs(seed: int = 0) -> dict:
    key = jax.random.key(seed)
    ks = jax.random.split(key, 24)
    f32 = jnp.float32

    def nrm(k, shape, scale):
        return jax.random.normal(k, shape, f32) * scale

    col_scale = jnp.concatenate([jnp.full((n,), DEEPNORM_BETA if i in VALUE_SEGMENTS else 1.0, f32)
                                 for i, n in enumerate(IN_SIZES)])
    return {
        "x": nrm(ks[0], (BATCH, SEQ, D_MODEL), 1.0),
        "w_in": nrm(ks[1], (DEPTH, D_MODEL, IN_COLS), D_MODEL ** -0.5) * col_scale,
        "b_forget": jax.random.uniform(ks[2], (DEPTH, FOX_HEADS), f32, 1.0, 4.0),
        "lam_q1": nrm(ks[3], (DEPTH, HEAD_DIM), 0.1),
        "lam_k1": nrm(ks[4], (DEPTH, HEAD_DIM), 0.1),
        "lam_q2": nrm(ks[5], (DEPTH, HEAD_DIM), 0.1),
        "lam_k2": nrm(ks[6], (DEPTH, HEAD_DIM), 0.1),
        "diff_norm_g": 1.0 + nrm(ks[7], (DEPTH, DIFF_HEADS, DIFF_V_DIM), 0.02),
        "w_proj_diff": nrm(ks[8], (DEPTH, DIFF_V_COLS, D_MODEL), DIFF_V_COLS ** -0.5 * DEEPNORM_BETA),
        "w_proj_fox": nrm(ks[9], (DEPTH, FOX_V_COLS, D_MODEL), FOX_V_COLS ** -0.5 * DEEPNORM_BETA),
        "w_out": nrm(ks[10], (DEPTH, D_MODEL, D_MODEL), D_MODEL ** -0.5 * DEEPNORM_BETA),
        "ln1_g": 1.0 + nrm(ks[11], (DEPTH, D_MODEL), 0.02),
        "ln1_b": nrm(ks[12], (DEPTH, D_MODEL), 0.02),
        "w_router_group": nrm(ks[13], (DEPTH, D_MODEL, N_GROUPS), D_MODEL ** -0.5),
        "b_router_group": nrm(ks[14], (DEPTH, N_GROUPS), 0.01),
        "w_router_expert": nrm(ks[15], (DEPTH, N_GROUPS, D_MODEL, EXPERTS_PER_GROUP), D_MODEL ** -0.5),
        "b_router_expert": nrm(ks[16], (DEPTH, N_GROUPS, EXPERTS_PER_GROUP), 0.01),
        "w_gate": nrm(ks[17], (DEPTH, N_EXPERTS, D_MODEL, EXPERT_FF), D_MODEL ** -0.5),
        "w_up": nrm(ks[18], (DEPTH, N_EXPERTS, D_MODEL, EXPERT_FF), D_MODEL ** -0.5 * DEEPNORM_BETA),
        "w_down": nrm(ks[19], (DEPTH, N_EXPERTS, EXPERT_FF, D_MODEL), EXPERT_FF ** -0.5 * DEEPNORM_BETA),
        "ln2_g": 1.0 + nrm(ks[20], (DEPTH, D_MODEL), 0.02),
        "ln2_b": nrm(ks[21], (DEPTH, D_MODEL), 0.02),
    }


def reference(x, w_in, b_forget, lam_q1, lam_k1, lam_q2, lam_k2, diff_norm_g, w_proj_diff, w_proj_fox,
              w_out, ln1_g, ln1_b, w_router_group, b_router_group, w_router_expert, b_router_expert,
              w_gate, w_up, w_down, ln2_g, ln2_b):
    for layer in range(DEPTH):
        lam_init = 0.8 - 0.6 * math.exp(-0.3 * layer)
        mix = _hybrid_mixer(x, w_in[layer], b_forget[layer], lam_q1[layer], lam_k1[layer], lam_q2[layer],
                            lam_k2[layer], diff_norm_g[layer], w_proj_diff[layer], w_proj_fox[layer],
                            w_out[layer], lam_init)
        x = _layer_norm(DEEPNORM_ALPHA * x + mix, ln1_g[layer], ln1_b[layer])
        ffn = _hierarchical_moe(x, w_router_group[layer], b_router_group[layer], w_router_expert[layer],
                                b_router_expert[layer], w_gate[layer], w_up[layer], w_down[layer])
        x = _layer_norm(DEEPNORM_ALPHA * x + ffn, ln2_g[layer], ln2_b[layer])
    return x
```

```python
import functools
import math

import jax
import jax.numpy as jnp
from jax import lax
from jax.experimental import pallas as pl
from jax.experimental.pallas import tpu as pltpu

F32, BF16, I32 = jnp.float32, jnp.bfloat16, jnp.int32

HEAD_DIM = 128
DIFF_V_DIM = 2 * HEAD_DIM
ROPE_THETA = 10000.0
N_GROUPS = 4
EXPERTS_PER_GROUP = 8
N_EXPERTS = N_GROUPS * EXPERTS_PER_GROUP
TOP_K = 2
LN_EPS = 1e-5
LOG2E = 1.4426950408889634
NEG = -1e30
LANES = 128
VMEM_LIMIT = 56 * 1024 * 1024

MOE_BLOCK = 128
PROJ_TM, PROJ_TN = 1024, 1024
ATT_TQ, ATT_TK = 512, 512
FORGET_TS = 512
MERGE_TM = 512
OUTLN_TM = 256
RANK_TB = 512
ROW_TM = 256


def _params(sem, vmem=VMEM_LIMIT):
    return pltpu.CompilerParams(dimension_semantics=sem, vmem_limit_bytes=vmem)


def _proj_kernel(x_ref, w_ref, *rest, epilogue):
    acc = jnp.dot(x_ref[...], w_ref[...], preferred_element_type=F32)
    if epilogue == "rope":
        cos_ref, sin_ref, o_ref = rest
        cosf, sinf = cos_ref[0], sin_ref[0]
        for h in range(acc.shape[1] // HEAD_DIM):
            t = acc[:, h * HEAD_DIM:(h + 1) * HEAD_DIM]
            o_ref[:, h * HEAD_DIM:(h + 1) * HEAD_DIM] = (
                t * cosf + pltpu.roll(t, HEAD_DIM // 2, 1) * sinf).astype(o_ref.dtype)
    elif epilogue == "scale":
        s_ref, o_ref = rest
        o_ref[...] = (acc * s_ref[...]).astype(o_ref.dtype)
    else:
        (o_ref,) = rest
        o_ref[...] = (1.0 / (1.0 + jnp.exp(-acc))).astype(o_ref.dtype)


def _proj(x_bf, w_bf, epilogue, extra=(), extra_specs=()):
    m, k = x_bf.shape
    n = w_bf.shape[1]
    tm, tn = min(PROJ_TM, m), min(PROJ_TN, n)
    return pl.pallas_call(
        functools.partial(_proj_kernel, epilogue=epilogue),
        out_shape=jax.ShapeDtypeStruct((m, n), BF16),
        grid=(n // tn, m // tm),
        in_specs=[pl.BlockSpec((tm, k), lambda j, i: (i, 0)),
                  pl.BlockSpec((k, tn), lambda j, i: (0, j)), *extra_specs],
        out_specs=pl.BlockSpec((tm, tn), lambda j, i: (i, j)),
        compiler_params=_params(("parallel", "parallel")),
        name=f"in_proj_{epilogue}",
    )(x_bf, w_bf, *extra)


def _split3(v):
    h = v.astype(BF16)
    r = v - h.astype(F32)
    m = r.astype(BF16)
    return h, m, (r - m.astype(F32)).astype(BF16)


def _forget_kernel(x_ref, w_ref, b_ref, o_ref, carry_ref, *, n_heads):
    i = pl.program_id(0)

    @pl.when(i == 0)
    def _():
        carry_ref[...] = jnp.zeros_like(carry_ref)

    x = x_ref[...]
    xh = x.astype(BF16)
    xl = (x - xh.astype(F32)).astype(BF16)
    w = w_ref[...]
    a = jnp.dot(xh, w, preferred_element_type=F32)
    b = jnp.dot(xl, w[:, :LANES], preferred_element_type=F32)
    z = a[:, :LANES] + a[:, LANES:] + b + b_ref[...]
    logf = jnp.minimum(z, 0.0) - jnp.log1p(jnp.exp(-jnp.abs(z)))
    lt = logf.T[:n_heads]
    ts = lt.shape[1]
    tri = (lax.broadcasted_iota(I32, (ts, ts), 0) <= lax.broadcasted_iota(I32, (ts, ts), 1)).astype(BF16)
    c = carry_ref[:, 0:1]
    for piece in _split3(lt):
        c = c + jnp.dot(piece, tri, preferred_element_type=F32)
    o_ref[...] = c * LOG2E
    carry_ref[...] = jnp.broadcast_to(c[:, ts - 1:ts], carry_ref.shape)


def _forget_cumsum(x, w_f, b_f):
    s, d = x.shape
    nh = w_f.shape[1]
    wh = w_f.astype(BF16)
    wl = (w_f - wh.astype(F32)).astype(BF16)
    pad = lambda a: jnp.pad(a, ((0, 0), (0, LANES - nh)))
    w2 = jnp.concatenate([pad(wh), pad(wl)], axis=1)
    b2 = jnp.pad(b_f.astype(F32), (0, LANES - nh)).reshape(1, LANES)
    ts = min(FORGET_TS, s)
    return pl.pallas_call(
        functools.partial(_forget_kernel, n_heads=nh),
        out_shape=jax.ShapeDtypeStruct((nh, s), F32),
        grid=(s // ts,),
        in_specs=[pl.BlockSpec((ts, d), lambda i: (i, 0)),
                  pl.BlockSpec((d, 2 * LANES), lambda i: (0, 0)),
                  pl.BlockSpec((1, LANES), lambda i: (0, 0))],
        out_specs=pl.BlockSpec((nh, ts), lambda i: (0, i)),
        scratch_shapes=[pltpu.VMEM((nh, LANES), F32)],
        compiler_params=_params(("arbitrary",)),
        name="forget_cumsum",
    )(x, w2, b2)


def _softmax_step(q, k, v, carry, bias=None, mask=None):
    m, l, acc = carry
    s = lax.dot_general(q, k, (((1,), (1,)), ((), ())), preferred_element_type=F32)
    if bias is not None:
        s = s - bias
    if mask is not None:
        s = jnp.where(mask, s, NEG)
    m_new = jnp.maximum(m, jnp.max(s, axis=1, keepdims=True))
    a = jnp.exp2(m - m_new)
    p = jnp.exp2(s - m_new)
    l = a * l + jnp.sum(p, axis=1, keepdims=True)
    acc = a * acc + jnp.dot(p.astype(BF16), v, preferred_element_type=F32)
    return m_new, l, acc


def _init_carry(tq, dv):
    return (jnp.full((tq, 1), NEG, F32), jnp.zeros((tq, 1), F32), jnp.zeros((tq, dv), F32))


def _diag_mask(tq, tk, j):
    return lax.broadcasted_iota(I32, (tq, tk), 0) >= lax.broadcasted_iota(I32, (tq, tk), 1) + j * tk


def _fox_kernel(q_ref, k_ref, v_ref, c_ref, o_ref, *, tq, tk):
    qi = pl.program_id(1)
    q = q_ref[...]

    def block(off, carry, mask):
        return _softmax_step(q, k_ref[pl.ds(off, tk), :], v_ref[pl.ds(off, tk), :], carry,
                             bias=c_ref[0, :, pl.ds(off, tk)], mask=mask)

    carry = lax.fori_loop(0, qi * (tq // tk),
                          lambda ki, c: block(pl.multiple_of(ki * tk, tk), c, None),
                          _init_carry(tq, v_ref.shape[1]))
    for j in range(tq // tk):
        carry = block(pl.multiple_of(qi * tq + j * tk, tk), carry, _diag_mask(tq, tk, j))
    _, l, acc = carry
    o_ref[...] = (acc / l).astype(o_ref.dtype)


def _fox_attention(lin, fcum, n_heads, col0):
    s = lin.shape[0]
    tq, tk = min(ATT_TQ, s), min(ATT_TK, s)
    return pl.pallas_call(
        functools.partial(_fox_kernel, tq=tq, tk=tk),
        out_shape=jax.ShapeDtypeStruct((s, n_heads * HEAD_DIM), BF16),
        grid=(n_heads, s // tq),
        in_specs=[pl.BlockSpec((tq, HEAD_DIM), lambda h, i: (i, col0 + h)),
                  pl.BlockSpec((s, HEAD_DIM), lambda h, i: (0, col0 + n_heads + h)),
                  pl.BlockSpec((s, HEAD_DIM), lambda h, i: (0, col0 + 2 * n_heads + h)),
                  pl.BlockSpec((1, 1, s), lambda h, i: (h, 0, 0))],
        out_specs=pl.BlockSpec((tq, HEAD_DIM), lambda h, i: (i, h)),
        compiler_params=_params(("parallel", "arbitrary")),
        name="fox_attention",
    )(lin, lin, lin, fcum.reshape(n_heads, 1, s))


def _diff_kernel(q_ref, k_ref, v_ref, lam_ref, g_ref, o_ref, *, tq, tk, lam_init):
    qi = pl.program_id(1)
    q1, q2 = q_ref[:, :HEAD_DIM], q_ref[:, HEAD_DIM:]

    def block(off, carry, mask):
        c1, c2 = carry
        v = v_ref[pl.ds(off, tk), :]
        c1 = _softmax_step(q1, k_ref[pl.ds(off, tk), :HEAD_DIM], v, c1, mask=mask)
        c2 = _softmax_step(q2, k_ref[pl.ds(off, tk), HEAD_DIM:], v, c2, mask=mask)
        return c1, c2

    dv = v_ref.shape[1]
    carry = lax.fori_loop(0, qi * (tq // tk),
                          lambda ki, c: block(pl.multiple_of(ki * tk, tk), c, None),
                          (_init_carry(tq, dv), _init_carry(tq, dv)))
    for j in range(tq // tk):
        carry = block(pl.multiple_of(qi * tq + j * tk, tk), carry, _diag_mask(tq, tk, j))
    (_, l1, acc1), (_, l2, acc2) = carry
    lam_v = lam_ref[...]
    lam = (jnp.exp(jnp.sum(lam_v[0:1] * lam_v[1:2], axis=1, keepdims=True))
           - jnp.exp(jnp.sum(lam_v[2:3] * lam_v[3:4], axis=1, keepdims=True)) + lam_init)
    o = acc1 / l1 - lam * (acc2 / l2)
    o = o * lax.rsqrt(jnp.mean(o * o, axis=1, keepdims=True) + LN_EPS) * g_ref[0] * (1.0 - lam_init)
    o_ref[...] = o.astype(o_ref.dtype)


def _diff_attention(qk, lin, lam_rows, norm_g, n_heads, lam_init):
    s = qk.shape[0]
    tq, tk = min(ATT_TQ, s), min(ATT_TK, s)
    return pl.pallas_call(
        functools.partial(_diff_kernel, tq=tq, tk=tk, lam_init=lam_init),
        out_shape=jax.ShapeDtypeStruct((s, n_heads * DIFF_V_DIM), BF16),
        grid=(n_heads, s // tq),
        in_specs=[pl.BlockSpec((tq, DIFF_V_DIM), lambda h, i: (i, h)),
                  pl.BlockSpec((s, DIFF_V_DIM), lambda h, i: (0, n_heads + h)),
                  pl.BlockSpec((s, DIFF_V_DIM), lambda h, i: (0, h)),
                  pl.BlockSpec((4, HEAD_DIM), lambda h, i: (0, 0)),
                  pl.BlockSpec((1, 1, DIFF_V_DIM), lambda h, i: (h, 0, 0))],
        out_specs=pl.BlockSpec((tq, DIFF_V_DIM), lambda h, i: (i, h)),
        compiler_params=_params(("parallel", "arbitrary")),
        name="diff_attention",
    )(qk, qk, lin, lam_rows, norm_g.reshape(n_heads, 1, DIFF_V_DIM))


def _merge_kernel(od_ref, of_ref, gd_ref, gf_ref, wd_ref, wf_ref, o_ref):
    ud = jnp.dot(od_ref[...], wd_ref[...], preferred_element_type=F32)
    uf = jnp.dot(of_ref[...], wf_ref[...], preferred_element_type=F32)
    o_ref[...] = (gd_ref[...].astype(F32) * ud + gf_ref[...].astype(F32) * uf).astype(o_ref.dtype)


def _merge(o_diff, o_fox, gates, wd, wf):
    s, d = o_diff.shape[0], wd.shape[1]
    tm = min(MERGE_TM, s)
    row = lambda c: pl.BlockSpec((tm, c), lambda i: (i, 0))
    full = lambda a: pl.BlockSpec(a.shape, lambda i: (0, 0))
    return pl.pallas_call(
        _merge_kernel,
        out_shape=jax.ShapeDtypeStruct((s, d), BF16),
        grid=(s // tm,),
        in_specs=[row(o_diff.shape[1]), row(o_fox.shape[1]),
                  pl.BlockSpec((tm, d), lambda i: (i, 0)), pl.BlockSpec((tm, d), lambda i: (i, 1)),
                  full(wd), full(wf)],
        out_specs=row(d),
        compiler_params=_params(("parallel",)),
        name="merge_branches",
    )(o_diff, o_fox, gates, gates, wd, wf)


def _layer_norm(y, g, b):
    mu = jnp.mean(y, axis=1, keepdims=True)
    yc = y - mu
    var = jnp.mean(yc * yc, axis=1, keepdims=True)
    return yc * lax.rsqrt(var + LN_EPS) * g + b


def _route(lg):
    lane = lax.broadcasted_iota(I32, lg.shape, 1)
    far = jnp.int32(4 * LANES)
    is_g = lane < N_GROUPS
    gl = jnp.where(is_g, lg, NEG)
    gmax = jnp.max(gl, axis=1, keepdims=True)
    gidx = jnp.min(jnp.where(gl == gmax, lane, far), axis=1, keepdims=True)
    top_gp = 1.0 / jnp.sum(jnp.where(is_g, jnp.exp(gl - gmax), 0.0), axis=1, keepdims=True)
    lo = N_GROUPS + gidx * EXPERTS_PER_GROUP
    el = jnp.where((lane >= lo) & (lane < lo + EXPERTS_PER_GROUP), lg, NEG)
    m1 = jnp.max(el, axis=1, keepdims=True)
    i1 = jnp.min(jnp.where(el == m1, lane, far), axis=1, keepdims=True)
    el2 = jnp.where(lane == i1, 2.0 * NEG, el)
    m2 = jnp.max(el2, axis=1, keepdims=True)
    i2 = jnp.min(jnp.where(el2 == m2, lane, far), axis=1, keepdims=True)
    d = jnp.exp(m2 - m1)
    w1 = top_gp / (1.0 + d)
    w2 = w1 * d
    ids = jnp.where(lane == 0, i1 - N_GROUPS, jnp.where(lane == 1, i2 - N_GROUPS, 0))
    wts = jnp.where(lane == 0, w1, jnp.where(lane == 1, w2, 0.0))
    return ids, wts


def _outln_kernel(mg_ref, x_ref, wo_ref, g_ref, b_ref, wr_ref, br_ref, x1_ref, ids_ref, wts_ref, *, alpha):
    mix = jnp.dot(mg_ref[...], wo_ref[...], preferred_element_type=F32)
    x1 = _layer_norm(alpha * x_ref[...] + mix, g_ref[...], b_ref[...])
    x1_ref[...] = x1
    xh = x1.astype(BF16)
    xl = (x1 - xh.astype(F32)).astype(BF16)
    wr = wr_ref[...]
    a = jnp.dot(xh, wr, preferred_element_type=F32)
    b = jnp.dot(xl, wr[:, :LANES], preferred_element_type=F32)
    ids, wts = _route(a[:, :LANES] + a[:, LANES:] + b + br_ref[...])
    ids_ref[...] = ids
    wts_ref[...] = wts


def _out_ln_route(merged, x, w_out, g, b, wr2, br, alpha):
    s, d = x.shape
    tm = min(OUTLN_TM, s)
    row = lambda c: pl.BlockSpec((tm, c), lambda i: (i, 0))
    full = lambda a: pl.BlockSpec(a.shape, lambda i: (0, 0))
    return pl.pallas_call(
        functools.partial(_outln_kernel, alpha=alpha),
        out_shape=(jax.ShapeDtypeStruct((s, d), F32), jax.ShapeDtypeStruct((s, LANES), I32),
                   jax.ShapeDtypeStruct((s, LANES), F32)),
        grid=(s // tm,),
        in_specs=[row(d), row(d), full(w_out), full(g), full(b), full(wr2), full(br)],
        out_specs=(row(d), row(LANES), row(LANES)),
        compiler_params=_params(("parallel",)),
        name="out_proj_ln_route",
    )(merged, x, w_out, g, b, wr2, br)


def _rank_kernel(ids_ref, dest_ref, bexp_ref, meta_ref, cnt_ref, carry_ref, start_ref, *, block, n_blocks):
    ph, i = pl.program_id(0), pl.program_id(1)
    ids = ids_ref[...]
    tb = ids.shape[0]
    lane = lax.broadcasted_iota(I32, (tb, LANES), 1)
    oh1 = (lane == ids[:, 0:1]).astype(F32)
    oh2 = (lane == ids[:, 1:2]).astype(F32)
    oh = oh1 + oh2
    colsum = jnp.sum(oh, axis=0, keepdims=True)

    @pl.when((ph == 0) & (i == 0))
    def _():
        cnt_ref[...] = jnp.zeros_like(cnt_ref)

    @pl.when(ph == 0)
    def _():
        cnt_ref[...] += jnp.broadcast_to(colsum, cnt_ref.shape)

    @pl.when((ph == 1) & (i == 0))
    def _():
        cnt = cnt_ref[0:1, :]
        nblk = jnp.floor((cnt + (block - 1)) * (1.0 / block))
        r = lax.broadcasted_iota(I32, (LANES, LANES), 0)
        c = lax.broadcasted_iota(I32, (LANES, LANES), 1)
        upper = (r < c).astype(BF16)
        nb8 = jnp.broadcast_to(nblk, (8, LANES)).astype(BF16)
        bstart = jnp.dot(nb8, upper, preferred_element_type=F32)
        start_ref[...] = bstart * block
        carry_ref[...] = jnp.zeros_like(carry_ref)
        lane1 = lax.broadcasted_iota(I32, (1, LANES), 1)
        bend = jnp.where(lane1 < N_EXPERTS, bstart[0:1] + nblk, 4.0 * n_blocks)
        bidx = lax.broadcasted_iota(I32, (n_blocks, LANES), 0).astype(F32)
        be = jnp.sum((jnp.broadcast_to(bend, (n_blocks, LANES)) <= bidx).astype(F32), axis=1, keepdims=True)
        bexp_ref[...] = jnp.broadcast_to(jnp.minimum(be, N_EXPERTS - 1.0), bexp_ref.shape).astype(I32)
        total = jnp.sum(jnp.where(lane1 < N_EXPERTS, nblk, 0.0), axis=1, keepdims=True)
        row = lax.broadcasted_iota(I32, (8, LANES), 0)
        first_pad = start_ref[0:1, :] + cnt
        meta = jnp.where(row == 0, jnp.broadcast_to(total, (8, LANES)),
                         jnp.where(row == 1, jnp.broadcast_to(first_pad, (8, LANES)),
                                   jnp.broadcast_to(bstart[0:1] * block + nblk * block, (8, LANES))))
        meta_ref[...] = meta.astype(I32)

    @pl.when(ph == 1)
    def _():
        rr = lax.broadcasted_iota(I32, (tb, tb), 0)
        cc = lax.broadcasted_iota(I32, (tb, tb), 1)
        lower = (cc < rr).astype(BF16)
        prefix = jnp.dot(lower, oh.astype(BF16), preferred_element_type=F32)
        pos = prefix + carry_ref[0:1, :] + start_ref[0:1, :]
        d1 = jnp.sum(pos * oh1, axis=1, keepdims=True)
        d2 = jnp.sum(pos * oh2, axis=1, keepdims=True)
        dest_ref[...] = jnp.where(lane == 0, d1, jnp.where(lane == 1, d2, 0.0)).astype(I32)
        carry_ref[...] += jnp.broadcast_to(colsum, carry_ref.shape)


def _rank(ids, block, n_blocks):
    t = ids.shape[0]
    tb = min(RANK_TB, t)
    return pl.pallas_call(
        functools.partial(_rank_kernel, block=block, n_blocks=n_blocks),
        out_shape=(jax.ShapeDtypeStruct((t, LANES), I32), jax.ShapeDtypeStruct((n_blocks, LANES), I32),
                   jax.ShapeDtypeStruct((8, LANES), I32)),
        grid=(2, t // tb),
        in_specs=[pl.BlockSpec((tb, LANES), lambda p, i: (i, 0))],
        out_specs=(pl.BlockSpec((tb, LANES), lambda p, i: (i * p, 0)),
                   pl.BlockSpec((n_blocks, LANES), lambda p, i: (0, 0)),
                   pl.BlockSpec((8, LANES), lambda p, i: (0, 0))),
        scratch_shapes=[pltpu.VMEM((8, LANES), F32)] * 3,
        compiler_params=_params(("arbitrary", "arbitrary")),
        name="moe_rank",
    )(ids)


def _dispatch_kernel(pad_ref, dest_ref, x_ref, xs_ref, zero_ref, sem, zsem, *, block):
    i = pl.program_id(0)
    tm = x_ref.shape[0]

    def row_copy(r, k):
        return pltpu.make_async_copy(x_ref.at[pl.ds(r, 1)], xs_ref.at[pl.ds(dest_ref[0, 0, TOP_K * r + k], 1)], sem)

    def start(r, c):
        for k in range(TOP_K):
            row_copy(r, k).start()
        return c

    def wait(r, c):
        for k in range(TOP_K):
            row_copy(r, k).wait()
        return c

    lax.fori_loop(0, tm, start, 0)

    @pl.when(i == 0)
    def _():
        zero_ref[...] = jnp.zeros_like(zero_ref)

        def zero_row(r):
            return pltpu.make_async_copy(zero_ref.at[pl.ds(0, 1)], xs_ref.at[pl.ds(r, 1)], zsem)

        def zero_block(b):
            return pltpu.make_async_copy(zero_ref, xs_ref.at[pl.ds(pl.multiple_of(b * block, block), block)], zsem)

        def expert_pad(e, c):
            lo, hi = pad_ref[1, e], pad_ref[2, e]
            lax.fori_loop(lo, hi, lambda r, cc: (zero_row(r).start(), cc)[1], 0)
            lax.fori_loop(lo, hi, lambda r, cc: (zero_row(r).wait(), cc)[1], 0)
            return c

        lax.fori_loop(0, N_EXPERTS, expert_pad, 0)
        n_active, n_blocks = pad_ref[0, 0], xs_ref.shape[0] // block
        lax.fori_loop(n_active, n_blocks, lambda b, c: (zero_block(b).start(), c)[1], 0)
        lax.fori_loop(n_active, n_blocks, lambda b, c: (zero_block(b).wait(), c)[1], 0)

    lax.fori_loop(0, tm, wait, 0)


def _dispatch(x1, dest3, pad_rows, rows, block):
    t, d = x1.shape
    tm = min(ROW_TM, t)
    return pl.pallas_call(
        functools.partial(_dispatch_kernel, block=block),
        out_shape=jax.ShapeDtypeStruct((rows, d), F32),
        grid_spec=pltpu.PrefetchScalarGridSpec(
            num_scalar_prefetch=1, grid=(t // tm,),
            in_specs=[pl.BlockSpec((1, 1, TOP_K * tm), lambda i, p: (i, 0, 0), memory_space=pltpu.SMEM),
                      pl.BlockSpec((tm, d), lambda i, p: (i, 0))],
            out_specs=pl.BlockSpec(memory_space=pl.ANY),
            scratch_shapes=[pltpu.VMEM((block, d), F32), pltpu.SemaphoreType.DMA, pltpu.SemaphoreType.DMA]),
        compiler_params=_params(("arbitrary",)),
        name="moe_dispatch",
    )(pad_rows, dest3, x1)


def _expert_kernel(bexp_ref, nact_ref, xs_ref, wg_ref, wu_ref, wd_ref, ys_ref, wg_bf, wu_bf, wd_bf):
    b = pl.program_id(0)
    active = b < nact_ref[0]
    prev = bexp_ref[jnp.maximum(b - 1, 0)]

    @pl.when(active & ((b == 0) | (bexp_ref[b] != prev)))
    def _():
        wg_bf[...] = wg_ref[0].astype(BF16)
        wu_bf[...] = wu_ref[0].astype(BF16)
        wd_bf[...] = wd_ref[0].astype(BF16)

    @pl.when(active)
    def _():
        x = xs_ref[...].astype(BF16)
        g = jnp.dot(x, wg_bf[...], preferred_element_type=F32)
        u = jnp.dot(x, wu_bf[...], preferred_element_type=F32)
        h = (g / (1.0 + jnp.exp(-g)) * u).astype(BF16)
        ys_ref[...] = jnp.dot(h, wd_bf[...], preferred_element_type=F32)

    @pl.when(jnp.logical_not(active))
    def _():
        ys_ref[...] = jnp.zeros_like(ys_ref)


def _experts(xs, bexp, nact, w_gate, w_up, w_down, block):
    rows, d = xs.shape
    ff = w_gate.shape[2]
    n_blocks = rows // block
    blk = lambda b, be, na: (jnp.minimum(b, na[0] - 1), 0)
    out_blk = lambda b, be, na: (b, 0)
    wmap = lambda b, be, na: (be[jnp.minimum(b, na[0] - 1)], 0, 0)
    return pl.pallas_call(
        _expert_kernel,
        out_shape=jax.ShapeDtypeStruct((rows, d), F32),
        grid_spec=pltpu.PrefetchScalarGridSpec(
            num_scalar_prefetch=2, grid=(n_blocks,),
            in_specs=[pl.BlockSpec((block, d), blk),
                      pl.BlockSpec((1, d, ff), wmap), pl.BlockSpec((1, d, ff), wmap),
                      pl.BlockSpec((1, ff, d), wmap)],
            out_specs=pl.BlockSpec((block, d), out_blk),
            scratch_shapes=[pltpu.VMEM((d, ff), BF16), pltpu.VMEM((d, ff), BF16), pltpu.VMEM((ff, d), BF16)]),
        compiler_params=_params(("arbitrary",)),
        name="moe_experts",
    )(bexp, nact, xs, w_gate, w_up, w_down)


def _combine_kernel(dest_ref, x1_ref, wts_ref, g_ref, b_ref, ys_ref, o_ref, buf, sem, *, alpha):
    tm = x1_ref.shape[0]

    def row_copy(r, k):
        return pltpu.make_async_copy(ys_ref.at[pl.ds(dest_ref[0, 0, TOP_K * r + k], 1)], buf.at[k, pl.ds(r, 1)], sem)

    def start(r, c):
        for k in range(TOP_K):
            row_copy(r, k).start()
        return c

    def wait(r, c):
        for k in range(TOP_K):
            row_copy(r, k).wait()
        return c

    lax.fori_loop(0, tm, start, 0)
    lax.fori_loop(0, tm, wait, 0)
    w = wts_ref[...]
    ffn = w[:, 0:1] * buf[0] + w[:, 1:2] * buf[1]
    o_ref[...] = _layer_norm(alpha * x1_ref[...] + ffn, g_ref[...], b_ref[...])


def _combine(ys, dest3, x1, wts, g, b, alpha):
    t, d = x1.shape
    tm = min(ROW_TM, t)
    row = lambda c: pl.BlockSpec((tm, c), lambda i: (i, 0))
    full = lambda a: pl.BlockSpec(a.shape, lambda i: (0, 0))
    return pl.pallas_call(
        functools.partial(_combine_kernel, alpha=alpha),
        out_shape=jax.ShapeDtypeStruct((t, d), F32),
        grid=(t // tm,),
        in_specs=[pl.BlockSpec((1, 1, TOP_K * tm), lambda i: (i, 0, 0), memory_space=pltpu.SMEM),
                  row(d), row(LANES), full(g), full(b), pl.BlockSpec(memory_space=pl.ANY)],
        out_specs=row(d),
        scratch_shapes=[pltpu.VMEM((TOP_K, tm, d), F32), pltpu.SemaphoreType.DMA],
        compiler_params=_params(("arbitrary",)),
        name="moe_combine_ln",
    )(dest3, x1, wts, g, b, ys)


def _rope_tables(s, q_scale):
    inv_freq = ROPE_THETA ** (-jnp.arange(0, HEAD_DIM, 2, dtype=F32) / HEAD_DIM)
    ang = jnp.arange(s, dtype=F32)[:, None] * inv_freq[None, :]
    cos, sin = jnp.cos(ang), jnp.sin(ang)
    cosf = jnp.concatenate([cos, cos], axis=1)
    sinf = jnp.concatenate([-sin, sin], axis=1)
    return jnp.stack([cosf * q_scale, cosf]), jnp.stack([sinf * q_scale, sinf])


def _split_bf16_pair(w, width):
    hi = w.astype(BF16)
    lo = (w - hi.astype(F32)).astype(BF16)
    pad = lambda a: jnp.pad(a, ((0, 0), (0, width - a.shape[1])))
    return jnp.concatenate([pad(hi), pad(lo)], axis=1)


def _layer(x2, layer, depth, w_in, b_forget, lam_q1, lam_k1, lam_q2, lam_k2, diff_norm_g, w_proj_diff,
           w_proj_fox, w_out, ln1_g, ln1_b, w_rg, b_rg, w_re, b_re, w_gate, w_up, w_down, ln2_g, ln2_b):
    s, d = x2.shape
    alpha = (2 * depth) ** 0.25
    lam_init = 0.8 - 0.6 * math.exp(-0.3 * layer)
    n_diff, n_fox = diff_norm_g.shape[0], b_forget.shape[0]
    qk_cols = n_diff * 2 * HEAD_DIM
    lin_cols = n_diff * DIFF_V_DIM + 3 * n_fox * HEAD_DIM
    q_scale = HEAD_DIM ** -0.5 * LOG2E

    x_bf = x2.astype(BF16)
    cos_t, sin_t = _rope_tables(s, q_scale)
    tm = min(PROJ_TM, s)
    tab = pl.BlockSpec((1, tm, HEAD_DIM), lambda j, i: (j, i, 0))
    assert qk_cols == PROJ_TN, "one column block each for dq and dk selects the scaled / plain rotary table"
    qk = _proj(x_bf, w_in[:, :2 * qk_cols].astype(BF16), "rope", (cos_t, sin_t), (tab, tab))
    col_scale = jnp.ones((1, lin_cols), F32).at[:, n_diff * DIFF_V_DIM:n_diff * DIFF_V_DIM + n_fox * HEAD_DIM].set(q_scale)
    lin = _proj(x_bf, w_in[:, 2 * qk_cols:2 * qk_cols + lin_cols].astype(BF16), "scale", (col_scale,),
                (pl.BlockSpec((1, min(PROJ_TN, lin_cols)), lambda j, i: (0, j)),))
    f0 = 2 * qk_cols + lin_cols
    gates = _proj(x_bf, w_in[:, f0 + n_fox:].astype(BF16), "sigmoid")

    fcum = _forget_cumsum(x2, w_in[:, f0:f0 + n_fox], b_forget)

    lam_rows = jnp.stack([lam_q1, lam_k1, lam_q2, lam_k2]).astype(F32)
    o_diff = _diff_attention(qk, lin, lam_rows, diff_norm_g.astype(F32), n_diff, lam_init)
    o_fox = _fox_attention(lin, fcum, n_fox, n_diff * DIFF_V_DIM // HEAD_DIM)

    merged = _merge(o_diff, o_fox, gates, w_proj_diff.astype(BF16), w_proj_fox.astype(BF16))
    w_router = jnp.concatenate([w_rg, jnp.moveaxis(w_re, 0, 1).reshape(d, N_EXPERTS)], axis=1)
    b_router = jnp.pad(jnp.concatenate([b_rg, b_re.reshape(N_EXPERTS)]).astype(F32),
                       (0, LANES - N_GROUPS - N_EXPERTS)).reshape(1, LANES)
    x1, ids, wts = _out_ln_route(merged, x2, w_out.astype(BF16), ln1_g.reshape(1, d), ln1_b.reshape(1, d),
                                 _split_bf16_pair(w_router, LANES), b_router, alpha)

    block = MOE_BLOCK
    n_blocks = -(-(s * TOP_K) // block) + N_EXPERTS
    dest, bexp, meta = _rank(ids, block, n_blocks)
    tmr = min(ROW_TM, s)
    dest3 = dest[:, :TOP_K].reshape(s // tmr, 1, TOP_K * tmr)
    xs = _dispatch(x1, dest3, meta[0:3, :N_EXPERTS], n_blocks * block, block)
    ys = _experts(xs, bexp[:, 0], meta[0, :1], w_gate, w_up, w_down, block)
    return _combine(ys, dest3, x1, wts, ln2_g.reshape(1, d), ln2_b.reshape(1, d), alpha)


def kernel(x, w_in, b_forget, lam_q1, lam_k1, lam_q2, lam_k2, diff_norm_g, w_proj_diff, w_proj_fox, w_out,
           ln1_g, ln1_b, w_router_group, b_router_group, w_router_expert, b_router_expert, w_gate, w_up,
           w_down, ln2_g, ln2_b):
    batch, s, d = x.shape
    depth = w_in.shape[0]
    params = (w_in, b_forget, lam_q1, lam_k1, lam_q2, lam_k2, diff_norm_g, w_proj_diff, w_proj_fox, w_out,
              ln1_g, ln1_b, w_router_group, b_router_group, w_router_expert, b_router_expert, w_gate, w_up,
              w_down, ln2_g, ln2_b)
    outs = []
    for bi in range(batch):
        h = x[bi]
        for layer in range(depth):
            h = _layer(h, layer, depth, *(p[layer] for p in params))
        outs.append(h)
    return jnp.stack(outs)
```

```python
import functools
import math

import jax
import jax.numpy as jnp
from jax import lax
from jax.experimental import pallas as pl
from jax.experimental.pallas import tpu as pltpu

F32, BF16, I32 = jnp.float32, jnp.bfloat16, jnp.int32

HEAD_DIM = 128
DIFF_V_DIM = 2 * HEAD_DIM
ROPE_THETA = 10000.0
N_GROUPS = 4
EXPERTS_PER_GROUP = 8
N_EXPERTS = N_GROUPS * EXPERTS_PER_GROUP
TOP_K = 2
LN_EPS = 1e-5
LOG2E = 1.4426950408889634
NEG = -1e30
LANES = 128
VMEM_LIMIT = 56 * 1024 * 1024

MOE_BLOCK = 128
PROJ_TM, PROJ_TN = 1024, 1024
ATT_TQ = 512
FORGET_TS = 512
MERGE_TM = 512
OUTLN_TM = 256
RANK_TB = 512
ROW_TM = 256


def _params(sem, vmem=VMEM_LIMIT):
    return pltpu.CompilerParams(dimension_semantics=sem, vmem_limit_bytes=vmem)


def _proj_kernel(x_ref, w_ref, *rest, epilogue):
    acc = jnp.dot(x_ref[...], w_ref[...], preferred_element_type=F32)
    if epilogue == "rope":
        cos_ref, sin_ref, o_ref = rest
        cosf, sinf = cos_ref[0], sin_ref[0]
        for h in range(acc.shape[1] // HEAD_DIM):
            t = acc[:, h * HEAD_DIM:(h + 1) * HEAD_DIM]
            o_ref[:, h * HEAD_DIM:(h + 1) * HEAD_DIM] = (
                t * cosf + pltpu.roll(t, HEAD_DIM // 2, 1) * sinf).astype(o_ref.dtype)
    elif epilogue == "scale":
        s_ref, o_ref = rest
        o_ref[...] = (acc * s_ref[...]).astype(o_ref.dtype)
    else:
        (o_ref,) = rest
        o_ref[...] = (1.0 / (1.0 + jnp.exp(-acc))).astype(o_ref.dtype)


def _proj(x_bf, w_bf, epilogue, extra=(), extra_specs=()):
    m, k = x_bf.shape
    n = w_bf.shape[1]
    tm, tn = min(PROJ_TM, m), min(PROJ_TN, n)
    return pl.pallas_call(
        functools.partial(_proj_kernel, epilogue=epilogue),
        out_shape=jax.ShapeDtypeStruct((m, n), BF16),
        grid=(n // tn, m // tm),
        in_specs=[pl.BlockSpec((tm, k), lambda j, i: (i, 0)),
                  pl.BlockSpec((k, tn), lambda j, i: (0, j)), *extra_specs],
        out_specs=pl.BlockSpec((tm, tn), lambda j, i: (i, j)),
        compiler_params=_params(("parallel", "parallel")),
        name=f"in_proj_{epilogue}",
    )(x_bf, w_bf, *extra)


def _split3(v):
    h = v.astype(BF16)
    r = v - h.astype(F32)
    m = r.astype(BF16)
    return h, m, (r - m.astype(F32)).astype(BF16)


def _forget_kernel(x_ref, w_ref, b_ref, o_ref, carry_ref):
    i = pl.program_id(0)

    @pl.when(i == 0)
    def _():
        carry_ref[...] = jnp.zeros_like(carry_ref)

    x = x_ref[...]
    xh = x.astype(BF16)
    xl = (x - xh.astype(F32)).astype(BF16)
    w = w_ref[...]
    a = jnp.dot(xh, w, preferred_element_type=F32)
    b = jnp.dot(xl, w[:, :LANES], preferred_element_type=F32)
    z = a[:, :LANES] + a[:, LANES:] + b + b_ref[...]
    logf = jnp.minimum(z, 0.0) - jnp.log1p(jnp.exp(-jnp.abs(z)))
    ts = logf.shape[0]
    tri = (lax.broadcasted_iota(I32, (ts, ts), 1) <= lax.broadcasted_iota(I32, (ts, ts), 0)).astype(BF16)
    c = carry_ref[0:1, :]
    for piece in _split3(logf):
        c = c + jnp.dot(tri, piece, preferred_element_type=F32)
    o_ref[...] = c * LOG2E
    carry_ref[...] = jnp.broadcast_to(c[ts - 1:ts, :], carry_ref.shape)


def _forget_cumsum(x, w_f, b_f):
    s, d = x.shape
    nh = w_f.shape[1]
    wh = w_f.astype(BF16)
    wl = (w_f - wh.astype(F32)).astype(BF16)
    pad = lambda a: jnp.pad(a, ((0, 0), (0, LANES - nh)))
    w2 = jnp.concatenate([pad(wh), pad(wl)], axis=1)
    b2 = jnp.pad(b_f.astype(F32), (0, LANES - nh)).reshape(1, LANES)
    ts = min(FORGET_TS, s)
    return pl.pallas_call(
        _forget_kernel,
        out_shape=jax.ShapeDtypeStruct((s, LANES), F32),
        grid=(s // ts,),
        in_specs=[pl.BlockSpec((ts, d), lambda i: (i, 0)),
                  pl.BlockSpec((d, 2 * LANES), lambda i: (0, 0)),
                  pl.BlockSpec((1, LANES), lambda i: (0, 0))],
        out_specs=pl.BlockSpec((ts, LANES), lambda i: (i, 0)),
        scratch_shapes=[pltpu.VMEM((8, LANES), F32)],
        compiler_params=_params(("arbitrary",)),
        name="forget_cumsum",
    )(x, w2, b2)


def _attend(qi, t, n_maps, scores, pv, st, p, acc, finalize):
    maps = range(n_maps)

    def score_stage(b, slot):
        cmax = []
        for j in maps:
            s = scores(b, j)
            st[slot][j] = s
            cmax.append(jnp.max(s, axis=0, keepdims=True))
        return tuple(cmax)

    def softmax_stage(slot, cmax, m, l, mask=None):
        a_new, m_new, l_new = [], [], []
        for j in maps:
            s = st[slot][j]
            if mask is not None:
                s = jnp.where(mask, s, NEG)
                cm = jnp.max(s, axis=0, keepdims=True)
            else:
                cm = cmax[j]
            mj = jnp.maximum(m[j], cm)
            aj = jnp.exp2(m[j] - mj)
            pj = jnp.exp2(s - mj)
            p[slot][j] = pj.astype(BF16)
            a_new.append(aj)
            m_new.append(mj)
            l_new.append(aj * l[j] + jnp.sum(pj, axis=0, keepdims=True))
        return tuple(a_new), tuple(m_new), tuple(l_new)

    def pv_stage(slot, a, b):
        for j in maps:
            acc[j] = a[j] * acc[j] + pv(p[slot][j], b)

    def step(i, slot, carry):
        cmax, a, m, l = carry
        pv_stage(1 - slot, a, jnp.maximum(i - 1, 0))
        cmax_next = score_stage(i + 1, 1 - slot)
        a, m, l = softmax_stage(slot, cmax, m, l)
        return cmax_next, a, m, l

    def finish(slot, carry):
        _, a, m, l = carry
        pv_stage(1 - slot, a, jnp.maximum(qi - 1, 0))
        mask = lax.broadcasted_iota(I32, (t, t), 1) >= lax.broadcasted_iota(I32, (t, t), 0)
        a, m, l = softmax_stage(slot, None, m, l, mask)
        pv_stage(slot, a, qi)
        finalize(l)

    p[1][...] = jnp.zeros(p[1].shape, p[1].dtype)
    acc[...] = jnp.zeros(acc.shape, acc.dtype)
    row = lambda v: tuple(jnp.full((1, t), v, F32) for _ in maps)
    carry = (score_stage(0, 0), row(1.0), row(NEG), row(0.0))
    carry = lax.fori_loop(0, qi // 2, lambda h, c: step(2 * h + 1, 1, step(2 * h, 0, c)), carry)

    @pl.when(qi % 2 == 0)
    def _():
        finish(0, carry)

    @pl.when(qi % 2 == 1)
    def _():
        finish(1, step(qi - 1, 0, carry))


def _fox_kernel(q_ref, k_ref, v_ref, c_ref, o_ref, kaug, vt, st0, st1, p0, p1, acc, *, t):
    h, qi = pl.program_id(0), pl.program_id(1)
    n_chunks = k_ref.shape[0] // t

    @pl.when(qi == 0)
    def _():
        def chunk(c, carry):
            rows = pl.ds(pl.multiple_of(c * t, t), t)
            lane = lax.broadcasted_iota(I32, (t, LANES), 1)
            neg_c = -jnp.sum(jnp.where(lane == h, c_ref[rows, :], 0.0), axis=1, keepdims=True)
            hi, mid, lo = (x.astype(F32) for x in _split3(neg_c))
            aug = jnp.where(lane == 0, hi, jnp.where(lane == 1, mid, jnp.where(lane == 2, lo, 0.0)))
            kaug[rows, :HEAD_DIM] = k_ref[rows, :]
            kaug[rows, HEAD_DIM:] = aug.astype(BF16)
            vt[:, rows] = v_ref[rows, :].astype(F32).T.astype(BF16)
            return carry

        lax.fori_loop(0, n_chunks, chunk, 0)

    ones = (lax.broadcasted_iota(I32, (HEAD_DIM, t), 0) < 3).astype(F32)
    qt = jnp.concatenate([q_ref[...].astype(F32).T, ones], axis=0).astype(BF16)
    blk = lambda b: pl.ds(pl.multiple_of(b * t, t), t)
    scores = lambda b, j: jnp.dot(kaug[blk(b), :], qt, preferred_element_type=F32)
    pv = lambda pb, b: jnp.dot(vt[:, blk(b)], pb, preferred_element_type=F32)

    def finalize(l):
        o_ref[...] = (acc[0] / l[0]).T.astype(o_ref.dtype)

    _attend(qi, t, 1, scores, pv, (st0, st1), (p0, p1), acc, finalize)


def _fox_attention(lin, fcum, n_heads, col0):
    s = lin.shape[0]
    t = min(ATT_TQ, s)
    return pl.pallas_call(
        functools.partial(_fox_kernel, t=t),
        out_shape=jax.ShapeDtypeStruct((s, n_heads * HEAD_DIM), BF16),
        grid=(n_heads, s // t),
        in_specs=[pl.BlockSpec((t, HEAD_DIM), lambda h, i: (i, col0 + h)),
                  pl.BlockSpec((s, HEAD_DIM), lambda h, i: (0, col0 + n_heads + h)),
                  pl.BlockSpec((s, HEAD_DIM), lambda h, i: (0, col0 + 2 * n_heads + h)),
                  pl.BlockSpec((s, LANES), lambda h, i: (0, 0))],
        out_specs=pl.BlockSpec((t, HEAD_DIM), lambda h, i: (i, h)),
        scratch_shapes=[pltpu.VMEM((s, 2 * HEAD_DIM), BF16), pltpu.VMEM((HEAD_DIM, s), BF16),
                        pltpu.VMEM((1, t, t), F32), pltpu.VMEM((1, t, t), F32),
                        pltpu.VMEM((1, t, t), BF16), pltpu.VMEM((1, t, t), BF16),
                        pltpu.VMEM((1, HEAD_DIM, t), F32)],
        compiler_params=_params(("arbitrary", "arbitrary")),
        name="fox_attention",
    )(lin, lin, lin, fcum)


def _diff_kernel(q_ref, k_ref, v_ref, lam_ref, g_ref, o_ref, vt, st0, st1, p0, p1, acc, *, t, lam_init):
    qi = pl.program_id(1)
    n_chunks = k_ref.shape[0] // t

    @pl.when(qi == 0)
    def _():
        def chunk(c, carry):
            rows = pl.ds(pl.multiple_of(c * t, t), t)
            vt[:, rows] = v_ref[rows, :].astype(F32).T.astype(BF16)
            return carry

        lax.fori_loop(0, n_chunks, chunk, 0)

    qt = q_ref[...].astype(F32).T.astype(BF16)
    blk = lambda b: pl.ds(pl.multiple_of(b * t, t), t)
    scores = lambda b, j: jnp.dot(k_ref[blk(b), j * HEAD_DIM:(j + 1) * HEAD_DIM],
                                  qt[j * HEAD_DIM:(j + 1) * HEAD_DIM], preferred_element_type=F32)
    pv = lambda pb, b: jnp.dot(vt[:, blk(b)], pb, preferred_element_type=F32)

    def finalize(l):
        lam_v = lam_ref[...]
        lam = (jnp.exp(jnp.sum(lam_v[0:1] * lam_v[1:2], axis=1, keepdims=True))
               - jnp.exp(jnp.sum(lam_v[2:3] * lam_v[3:4], axis=1, keepdims=True)) + lam_init)
        o = (acc[0] / l[0] - lam * (acc[1] / l[1])).T
        o = o * lax.rsqrt(jnp.mean(o * o, axis=1, keepdims=True) + LN_EPS) * g_ref[0] * (1.0 - lam_init)
        o_ref[...] = o.astype(o_ref.dtype)

    _attend(qi, t, 2, scores, pv, (st0, st1), (p0, p1), acc, finalize)


def _diff_attention(qk, lin, lam_rows, norm_g, n_heads, lam_init):
    s = qk.shape[0]
    t = min(ATT_TQ, s)
    return pl.pallas_call(
        functools.partial(_diff_kernel, t=t, lam_init=lam_init),
        out_shape=jax.ShapeDtypeStruct((s, n_heads * DIFF_V_DIM), BF16),
        grid=(n_heads, s // t),
        in_specs=[pl.BlockSpec((t, DIFF_V_DIM), lambda h, i: (i, h)),
                  pl.BlockSpec((s, DIFF_V_DIM), lambda h, i: (0, n_heads + h)),
                  pl.BlockSpec((s, DIFF_V_DIM), lambda h, i: (0, h)),
                  pl.BlockSpec((4, HEAD_DIM), lambda h, i: (0, 0)),
                  pl.BlockSpec((1, 1, DIFF_V_DIM), lambda h, i: (h, 0, 0))],
        out_specs=pl.BlockSpec((t, DIFF_V_DIM), lambda h, i: (i, h)),
        scratch_shapes=[pltpu.VMEM((DIFF_V_DIM, s), BF16),
                        pltpu.VMEM((2, t, t), F32), pltpu.VMEM((2, t, t), F32),
                        pltpu.VMEM((2, t, t), BF16), pltpu.VMEM((2, t, t), BF16),
                        pltpu.VMEM((2, DIFF_V_DIM, t), F32)],
        compiler_params=_params(("arbitrary", "arbitrary")),
        name="diff_attention",
    )(qk, qk, lin, lam_rows, norm_g.reshape(n_heads, 1, DIFF_V_DIM))


def _merge_kernel(od_ref, of_ref, gd_ref, gf_ref, wd_ref, wf_ref, o_ref):
    ud = jnp.dot(od_ref[...], wd_ref[...], preferred_element_type=F32)
    uf = jnp.dot(of_ref[...], wf_ref[...], preferred_element_type=F32)
    o_ref[...] = (gd_ref[...].astype(F32) * ud + gf_ref[...].astype(F32) * uf).astype(o_ref.dtype)


def _merge(o_diff, o_fox, gates, wd, wf):
    s, d = o_diff.shape[0], wd.shape[1]
    tm = min(MERGE_TM, s)
    row = lambda c: pl.BlockSpec((tm, c), lambda i: (i, 0))
    full = lambda a: pl.BlockSpec(a.shape, lambda i: (0, 0))
    return pl.pallas_call(
        _merge_kernel,
        out_shape=jax.ShapeDtypeStruct((s, d), BF16),
        grid=(s // tm,),
        in_specs=[row(o_diff.shape[1]), row(o_fox.shape[1]),
                  pl.BlockSpec((tm, d), lambda i: (i, 0)), pl.BlockSpec((tm, d), lambda i: (i, 1)),
                  full(wd), full(wf)],
        out_specs=row(d),
        compiler_params=_params(("parallel",)),
        name="merge_branches",
    )(o_diff, o_fox, gates, gates, wd, wf)


def _layer_norm(y, g, b):
    mu = jnp.mean(y, axis=1, keepdims=True)
    yc = y - mu
    var = jnp.mean(yc * yc, axis=1, keepdims=True)
    return yc * lax.rsqrt(var + LN_EPS) * g + b


def _route(lg):
    lane = lax.broadcasted_iota(I32, lg.shape, 1)
    far = jnp.int32(4 * LANES)
    is_g = lane < N_GROUPS
    gl = jnp.where(is_g, lg, NEG)
    gmax = jnp.max(gl, axis=1, keepdims=True)
    gidx = jnp.min(jnp.where(gl == gmax, lane, far), axis=1, keepdims=True)
    top_gp = 1.0 / jnp.sum(jnp.where(is_g, jnp.exp(gl - gmax), 0.0), axis=1, keepdims=True)
    lo = N_GROUPS + gidx * EXPERTS_PER_GROUP
    el = jnp.where((lane >= lo) & (lane < lo + EXPERTS_PER_GROUP), lg, NEG)
    m1 = jnp.max(el, axis=1, keepdims=True)
    i1 = jnp.min(jnp.where(el == m1, lane, far), axis=1, keepdims=True)
    el2 = jnp.where(lane == i1, 2.0 * NEG, el)
    m2 = jnp.max(el2, axis=1, keepdims=True)
    i2 = jnp.min(jnp.where(el2 == m2, lane, far), axis=1, keepdims=True)
    d = jnp.exp(m2 - m1)
    w1 = top_gp / (1.0 + d)
    w2 = w1 * d
    ids = jnp.where(lane == 0, i1 - N_GROUPS, jnp.where(lane == 1, i2 - N_GROUPS, 0))
    wts = jnp.where(lane == 0, w1, jnp.where(lane == 1, w2, 0.0))
    return ids, wts


def _outln_kernel(mg_ref, x_ref, wo_ref, g_ref, b_ref, wr_ref, br_ref, x1_ref, ids_ref, wts_ref, *, alpha):
    mix = jnp.dot(mg_ref[...], wo_ref[...], preferred_element_type=F32)
    x1 = _layer_norm(alpha * x_ref[...] + mix, g_ref[...], b_ref[...])
    x1_ref[...] = x1
    xh = x1.astype(BF16)
    xl = (x1 - xh.astype(F32)).astype(BF16)
    wr = wr_ref[...]
    a = jnp.dot(xh, wr, preferred_element_type=F32)
    b = jnp.dot(xl, wr[:, :LANES], preferred_element_type=F32)
    ids, wts = _route(a[:, :LANES] + a[:, LANES:] + b + br_ref[...])
    ids_ref[...] = ids
    wts_ref[...] = wts


def _out_ln_route(merged, x, w_out, g, b, wr2, br, alpha):
    s, d = x.shape
    tm = min(OUTLN_TM, s)
    row = lambda c: pl.BlockSpec((tm, c), lambda i: (i, 0))
    full = lambda a: pl.BlockSpec(a.shape, lambda i: (0, 0))
    return pl.pallas_call(
        functools.partial(_outln_kernel, alpha=alpha),
        out_shape=(jax.ShapeDtypeStruct((s, d), F32), jax.ShapeDtypeStruct((s, LANES), I32),
                   jax.ShapeDtypeStruct((s, LANES), F32)),
        grid=(s // tm,),
        in_specs=[row(d), row(d), full(w_out), full(g), full(b), full(wr2), full(br)],
        out_specs=(row(d), row(LANES), row(LANES)),
        compiler_params=_params(("parallel",)),
        name="out_proj_ln_route",
    )(merged, x, w_out, g, b, wr2, br)


def _rank_kernel(ids_ref, dest_ref, bexp_ref, meta_ref, cnt_ref, carry_ref, start_ref, *, block, n_blocks):
    ph, i = pl.program_id(0), pl.program_id(1)
    ids = ids_ref[...]
    tb = ids.shape[0]
    lane = lax.broadcasted_iota(I32, (tb, LANES), 1)
    oh1 = (lane == ids[:, 0:1]).astype(F32)
    oh2 = (lane == ids[:, 1:2]).astype(F32)
    oh = oh1 + oh2
    colsum = jnp.sum(oh, axis=0, keepdims=True)

    @pl.when((ph == 0) & (i == 0))
    def _():
        cnt_ref[...] = jnp.zeros_like(cnt_ref)

    @pl.when(ph == 0)
    def _():
        cnt_ref[...] += jnp.broadcast_to(colsum, cnt_ref.shape)

    @pl.when((ph == 1) & (i == 0))
    def _():
        cnt = cnt_ref[0:1, :]
        nblk = jnp.floor((cnt + (block - 1)) * (1.0 / block))
        r = lax.broadcasted_iota(I32, (LANES, LANES), 0)
        c = lax.broadcasted_iota(I32, (LANES, LANES), 1)
        upper = (r < c).astype(BF16)
        nb8 = jnp.broadcast_to(nblk, (8, LANES)).astype(BF16)
        bstart = jnp.dot(nb8, upper, preferred_element_type=F32)
        start_ref[...] = bstart * block
        carry_ref[...] = jnp.zeros_like(carry_ref)
        lane1 = lax.broadcasted_iota(I32, (1, LANES), 1)
        bend = jnp.where(lane1 < N_EXPERTS, bstart[0:1] + nblk, 4.0 * n_blocks)
        bidx = lax.broadcasted_iota(I32, (n_blocks, LANES), 0).astype(F32)
        be = jnp.sum((jnp.broadcast_to(bend, (n_blocks, LANES)) <= bidx).astype(F32), axis=1, keepdims=True)
        bexp_ref[...] = jnp.broadcast_to(jnp.minimum(be, N_EXPERTS - 1.0), bexp_ref.shape).astype(I32)
        total = jnp.sum(jnp.where(lane1 < N_EXPERTS, nblk, 0.0), axis=1, keepdims=True)
        row = lax.broadcasted_iota(I32, (8, LANES), 0)
        first_pad = start_ref[0:1, :] + cnt
        meta = jnp.where(row == 0, jnp.broadcast_to(total, (8, LANES)),
                         jnp.where(row == 1, jnp.broadcast_to(first_pad, (8, LANES)),
                                   jnp.broadcast_to(bstart[0:1] * block + nblk * block, (8, LANES))))
        meta_ref[...] = meta.astype(I32)

    @pl.when(ph == 1)
    def _():
        rr = lax.broadcasted_iota(I32, (tb, tb), 0)
        cc = lax.broadcasted_iota(I32, (tb, tb), 1)
        lower = (cc < rr).astype(BF16)
        prefix = jnp.dot(lower, oh.astype(BF16), preferred_element_type=F32)
        pos = prefix + carry_ref[0:1, :] + start_ref[0:1, :]
        d1 = jnp.sum(pos * oh1, axis=1, keepdims=True)
        d2 = jnp.sum(pos * oh2, axis=1, keepdims=True)
        dest_ref[...] = jnp.where(lane == 0, d1, jnp.where(lane == 1, d2, 0.0)).astype(I32)
        carry_ref[...] += jnp.broadcast_to(colsum, carry_ref.shape)


def _rank(ids, block, n_blocks):
    t = ids.shape[0]
    tb = min(RANK_TB, t)
    return pl.pallas_call(
        functools.partial(_rank_kernel, block=block, n_blocks=n_blocks),
        out_shape=(jax.ShapeDtypeStruct((t, LANES), I32), jax.ShapeDtypeStruct((n_blocks, LANES), I32),
                   jax.ShapeDtypeStruct((8, LANES), I32)),
        grid=(2, t // tb),
        in_specs=[pl.BlockSpec((tb, LANES), lambda p, i: (i, 0))],
        out_specs=(pl.BlockSpec((tb, LANES), lambda p, i: (i * p, 0)),
                   pl.BlockSpec((n_blocks, LANES), lambda p, i: (0, 0)),
                   pl.BlockSpec((8, LANES), lambda p, i: (0, 0))),
        scratch_shapes=[pltpu.VMEM((8, LANES), F32)] * 3,
        compiler_params=_params(("arbitrary", "arbitrary")),
        name="moe_rank",
    )(ids)


def _dispatch_kernel(pad_ref, dest_ref, x_ref, xs_ref, zero_ref, sem, zsem, *, block):
    i = pl.program_id(0)
    tm = x_ref.shape[0]

    def row_copy(r, k):
        return pltpu.make_async_copy(x_ref.at[pl.ds(r, 1)], xs_ref.at[pl.ds(dest_ref[0, 0, TOP_K * r + k], 1)], sem)

    def start(r, c):
        for k in range(TOP_K):
            row_copy(r, k).start()
        return c

    def wait(r, c):
        for k in range(TOP_K):
            row_copy(r, k).wait()
        return c

    lax.fori_loop(0, tm, start, 0)

    @pl.when(i == 0)
    def _():
        zero_ref[...] = jnp.zeros_like(zero_ref)

        def zero_row(r):
            return pltpu.make_async_copy(zero_ref.at[pl.ds(0, 1)], xs_ref.at[pl.ds(r, 1)], zsem)

        def zero_block(b):
            return pltpu.make_async_copy(zero_ref, xs_ref.at[pl.ds(pl.multiple_of(b * block, block), block)], zsem)

        def expert_pad(e, c):
            lo, hi = pad_ref[1, e], pad_ref[2, e]
            lax.fori_loop(lo, hi, lambda r, cc: (zero_row(r).start(), cc)[1], 0)
            lax.fori_loop(lo, hi, lambda r, cc: (zero_row(r).wait(), cc)[1], 0)
            return c

        lax.fori_loop(0, N_EXPERTS, expert_pad, 0)
        n_active, n_blocks = pad_ref[0, 0], xs_ref.shape[0] // block
        lax.fori_loop(n_active, n_blocks, lambda b, c: (zero_block(b).start(), c)[1], 0)
        lax.fori_loop(n_active, n_blocks, lambda b, c: (zero_block(b).wait(), c)[1], 0)

    lax.fori_loop(0, tm, wait, 0)


def _dispatch(x1, dest3, pad_rows, rows, block):
    t, d = x1.shape
    tm = min(ROW_TM, t)
    return pl.pallas_call(
        functools.partial(_dispatch_kernel, block=block),
        out_shape=jax.ShapeDtypeStruct((rows, d), F32),
        grid_spec=pltpu.PrefetchScalarGridSpec(
            num_scalar_prefetch=1, grid=(t // tm,),
            in_specs=[pl.BlockSpec((1, 1, TOP_K * tm), lambda i, p: (i, 0, 0), memory_space=pltpu.SMEM),
                      pl.BlockSpec((tm, d), lambda i, p: (i, 0))],
            out_specs=pl.BlockSpec(memory_space=pl.ANY),
            scratch_shapes=[pltpu.VMEM((block, d), F32), pltpu.SemaphoreType.DMA, pltpu.SemaphoreType.DMA]),
        compiler_params=_params(("arbitrary",)),
        name="moe_dispatch",
    )(pad_rows, dest3, x1)


def _expert_kernel(bexp_ref, nact_ref, xs_ref, wg_ref, wu_ref, wd_ref, ys_ref, wg_bf, wu_bf, wd_bf):
    b = pl.program_id(0)
    active = b < nact_ref[0]
    prev = bexp_ref[jnp.maximum(b - 1, 0)]

    @pl.when(active & ((b == 0) | (bexp_ref[b] != prev)))
    def _():
        wg_bf[...] = wg_ref[0].astype(BF16)
        wu_bf[...] = wu_ref[0].astype(BF16)
        wd_bf[...] = wd_ref[0].astype(BF16)

    @pl.when(active)
    def _():
        x = xs_ref[...].astype(BF16)
        g = jnp.dot(x, wg_bf[...], preferred_element_type=F32)
        u = jnp.dot(x, wu_bf[...], preferred_element_type=F32)
        h = (g / (1.0 + jnp.exp(-g)) * u).astype(BF16)
        ys_ref[...] = jnp.dot(h, wd_bf[...], preferred_element_type=F32)

    @pl.when(jnp.logical_not(active))
    def _():
        ys_ref[...] = jnp.zeros_like(ys_ref)


def _experts(xs, bexp, nact, w_gate, w_up, w_down, block):
    rows, d = xs.shape
    ff = w_gate.shape[2]
    n_blocks = rows // block
    blk = lambda b, be, na: (jnp.minimum(b, na[0] - 1), 0)
    out_blk = lambda b, be, na: (b, 0)
    wmap = lambda b, be, na: (be[jnp.minimum(b, na[0] - 1)], 0, 0)
    return pl.pallas_call(
        _expert_kernel,
        out_shape=jax.ShapeDtypeStruct((rows, d), F32),
        grid_spec=pltpu.PrefetchScalarGridSpec(
            num_scalar_prefetch=2, grid=(n_blocks,),
            in_specs=[pl.BlockSpec((block, d), blk),
                      pl.BlockSpec((1, d, ff), wmap), pl.BlockSpec((1, d, ff), wmap),
                      pl.BlockSpec((1, ff, d), wmap)],
            out_specs=pl.BlockSpec((block, d), out_blk),
            scratch_shapes=[pltpu.VMEM((d, ff), BF16), pltpu.VMEM((d, ff), BF16), pltpu.VMEM((ff, d), BF16)]),
        compiler_params=_params(("arbitrary",)),
        name="moe_experts",
    )(bexp, nact, xs, w_gate, w_up, w_down)


def _combine_kernel(dest_ref, x1_ref, wts_ref, g_ref, b_ref, ys_ref, o_ref, buf, sem, *, alpha):
    tm = x1_ref.shape[0]

    def row_copy(r, k):
        return pltpu.make_async_copy(ys_ref.at[pl.ds(dest_ref[0, 0, TOP_K * r + k], 1)], buf.at[k, pl.ds(r, 1)], sem)

    def start(r, c):
        for k in range(TOP_K):
            row_copy(r, k).start()
        return c

    def wait(r, c):
        for k in range(TOP_K):
            row_copy(r, k).wait()
        return c

    lax.fori_loop(0, tm, start, 0)
    lax.fori_loop(0, tm, wait, 0)
    w = wts_ref[...]
    ffn = w[:, 0:1] * buf[0] + w[:, 1:2] * buf[1]
    o_ref[...] = _layer_norm(alpha * x1_ref[...] + ffn, g_ref[...], b_ref[...])


def _combine(ys, dest3, x1, wts, g, b, alpha):
    t, d = x1.shape
    tm = min(ROW_TM, t)
    row = lambda c: pl.BlockSpec((tm, c), lambda i: (i, 0))
    full = lambda a: pl.BlockSpec(a.shape, lambda i: (0, 0))
    return pl.pallas_call(
        functools.partial(_combine_kernel, alpha=alpha),
        out_shape=jax.ShapeDtypeStruct((t, d), F32),
        grid=(t // tm,),
        in_specs=[pl.BlockSpec((1, 1, TOP_K * tm), lambda i: (i, 0, 0), memory_space=pltpu.SMEM),
                  row(d), row(LANES), full(g), full(b), pl.BlockSpec(memory_space=pl.ANY)],
        out_specs=row(d),
        scratch_shapes=[pltpu.VMEM((TOP_K, tm, d), F32), pltpu.SemaphoreType.DMA],
        compiler_params=_params(("arbitrary",)),
        name="moe_combine_ln",
    )(dest3, x1, wts, g, b, ys)


def _rope_tables(s, q_scale):
    inv_freq = ROPE_THETA ** (-jnp.arange(0, HEAD_DIM, 2, dtype=F32) / HEAD_DIM)
    ang = jnp.arange(s, dtype=F32)[:, None] * inv_freq[None, :]
    cos, sin = jnp.cos(ang), jnp.sin(ang)
    cosf = jnp.concatenate([cos, cos], axis=1)
    sinf = jnp.concatenate([-sin, sin], axis=1)
    return jnp.stack([cosf * q_scale, cosf]), jnp.stack([sinf * q_scale, sinf])


def _split_bf16_pair(w, width):
    hi = w.astype(BF16)
    lo = (w - hi.astype(F32)).astype(BF16)
    pad = lambda a: jnp.pad(a, ((0, 0), (0, width - a.shape[1])))
    return jnp.concatenate([pad(hi), pad(lo)], axis=1)


def _layer(x2, layer, depth, w_in, b_forget, lam_q1, lam_k1, lam_q2, lam_k2, diff_norm_g, w_proj_diff,
           w_proj_fox, w_out, ln1_g, ln1_b, w_rg, b_rg, w_re, b_re, w_gate, w_up, w_down, ln2_g, ln2_b):
    s, d = x2.shape
    alpha = (2 * depth) ** 0.25
    lam_init = 0.8 - 0.6 * math.exp(-0.3 * layer)
    n_diff, n_fox = diff_norm_g.shape[0], b_forget.shape[0]
    qk_cols = n_diff * 2 * HEAD_DIM
    lin_cols = n_diff * DIFF_V_DIM + 3 * n_fox * HEAD_DIM
    q_scale = HEAD_DIM ** -0.5 * LOG2E

    x_bf = x2.astype(BF16)
    cos_t, sin_t = _rope_tables(s, q_scale)
    tm = min(PROJ_TM, s)
    tab = pl.BlockSpec((1, tm, HEAD_DIM), lambda j, i: (j, i, 0))
    assert qk_cols == PROJ_TN, "one column block each for dq and dk selects the scaled / plain rotary table"
    qk = _proj(x_bf, w_in[:, :2 * qk_cols].astype(BF16), "rope", (cos_t, sin_t), (tab, tab))
    col_scale = jnp.ones((1, lin_cols), F32).at[:, n_diff * DIFF_V_DIM:n_diff * DIFF_V_DIM + n_fox * HEAD_DIM].set(q_scale)
    lin = _proj(x_bf, w_in[:, 2 * qk_cols:2 * qk_cols + lin_cols].astype(BF16), "scale", (col_scale,),
                (pl.BlockSpec((1, min(PROJ_TN, lin_cols)), lambda j, i: (0, j)),))
    f0 = 2 * qk_cols + lin_cols
    gates = _proj(x_bf, w_in[:, f0 + n_fox:].astype(BF16), "sigmoid")

    fcum = _forget_cumsum(x2, w_in[:, f0:f0 + n_fox], b_forget)

    lam_rows = jnp.stack([lam_q1, lam_k1, lam_q2, lam_k2]).astype(F32)
    o_diff = _diff_attention(qk, lin, lam_rows, diff_norm_g.astype(F32), n_diff, lam_init)
    o_fox = _fox_attention(lin, fcum, n_fox, n_diff * DIFF_V_DIM // HEAD_DIM)

    merged = _merge(o_diff, o_fox, gates, w_proj_diff.astype(BF16), w_proj_fox.astype(BF16))
    w_router = jnp.concatenate([w_rg, jnp.moveaxis(w_re, 0, 1).reshape(d, N_EXPERTS)], axis=1)
    b_router = jnp.pad(jnp.concatenate([b_rg, b_re.reshape(N_EXPERTS)]).astype(F32),
                       (0, LANES - N_GROUPS - N_EXPERTS)).reshape(1, LANES)
    x1, ids, wts = _out_ln_route(merged, x2, w_out.astype(BF16), ln1_g.reshape(1, d), ln1_b.reshape(1, d),
                                 _split_bf16_pair(w_router, LANES), b_router, alpha)

    block = MOE_BLOCK
    n_blocks = -(-(s * TOP_K) // block) + N_EXPERTS
    dest, bexp, meta = _rank(ids, block, n_blocks)
    tmr = min(ROW_TM, s)
    dest3 = dest[:, :TOP_K].reshape(s // tmr, 1, TOP_K * tmr)
    xs = _dispatch(x1, dest3, meta[0:3, :N_EXPERTS], n_blocks * block, block)
    ys = _experts(xs, bexp[:, 0], meta[0, :1], w_gate, w_up, w_down, block)
    return _combine(ys, dest3, x1, wts, ln2_g.reshape(1, d), ln2_b.reshape(1, d), alpha)


def kernel(x, w_in, b_forget, lam_q1, lam_k1, lam_q2, lam_k2, diff_norm_g, w_proj_diff, w_proj_fox, w_out,
           ln1_g, ln1_b, w_router_group, b_router_group, w_router_expert, b_router_expert, w_gate, w_up,
           w_down, ln2_g, ln2_b):
    batch, s, d = x.shape
    depth = w_in.shape[0]
    params = (w_in, b_forget, lam_q1, lam_k1, lam_q2, lam_k2, diff_norm_g, w_proj_diff, w_proj_fox, w_out,
              ln1_g, ln1_b, w_router_group, b_router_group, w_router_expert, b_router_expert, w_gate, w_up,
              w_down, ln2_g, ln2_b)
    outs = []
    for bi in range(batch):
        h = x[bi]
        for layer in range(depth):
            h = _layer(h, layer, depth, *(p[layer] for p in params))
        outs.append(h)
    return jnp.stack(outs)
```

```python
import functools
import math

import jax
import jax.numpy as jnp
from jax import lax
from jax.experimental import pallas as pl
from jax.experimental.pallas import tpu as pltpu

F32, BF16, I32 = jnp.float32, jnp.bfloat16, jnp.int32

HEAD_DIM = 128
DIFF_V_DIM = 2 * HEAD_DIM
ROPE_THETA = 10000.0
N_GROUPS = 4
EXPERTS_PER_GROUP = 8
N_EXPERTS = N_GROUPS * EXPERTS_PER_GROUP
TOP_K = 2
LN_EPS = 1e-5
LOG2E = 1.4426950408889634
NEG = -1e30
LANES = 128
VMEM_LIMIT = 56 * 1024 * 1024

MOE_BLOCK = 128
PROJ_TM, PROJ_TN = 1024, 1024
ATT_TQ = 512
FORGET_TS = 512
MERGE_TM = 512
OUTLN_TM = 256
RANK_TB = 512
ROW_TM = 256
INVERT_CHUNK = 1024
GATHER_DEPTH = 3


def _params(sem, vmem=VMEM_LIMIT):
    return pltpu.CompilerParams(dimension_semantics=sem, vmem_limit_bytes=vmem)


def _proj_kernel(x_ref, w_ref, *rest, epilogue):
    acc = jnp.dot(x_ref[...], w_ref[...], preferred_element_type=F32)
    if epilogue == "rope":
        cos_ref, sin_ref, o_ref = rest
        cosf, sinf = cos_ref[0], sin_ref[0]
        for h in range(acc.shape[1] // HEAD_DIM):
            t = acc[:, h * HEAD_DIM:(h + 1) * HEAD_DIM]
            o_ref[:, h * HEAD_DIM:(h + 1) * HEAD_DIM] = (
                t * cosf + pltpu.roll(t, HEAD_DIM // 2, 1) * sinf).astype(o_ref.dtype)
    elif epilogue == "scale":
        s_ref, o_ref = rest
        o_ref[...] = (acc * s_ref[...]).astype(o_ref.dtype)
    else:
        (o_ref,) = rest
        o_ref[...] = (1.0 / (1.0 + jnp.exp(-acc))).astype(o_ref.dtype)


def _proj(x_bf, w_bf, epilogue, extra=(), extra_specs=()):
    m, k = x_bf.shape
    n = w_bf.shape[1]
    tm, tn = min(PROJ_TM, m), min(PROJ_TN, n)
    return pl.pallas_call(
        functools.partial(_proj_kernel, epilogue=epilogue),
        out_shape=jax.ShapeDtypeStruct((m, n), BF16),
        grid=(n // tn, m // tm),
        in_specs=[pl.BlockSpec((tm, k), lambda j, i: (i, 0)),
                  pl.BlockSpec((k, tn), lambda j, i: (0, j)), *extra_specs],
        out_specs=pl.BlockSpec((tm, tn), lambda j, i: (i, j)),
        compiler_params=_params(("parallel", "parallel")),
        name=f"in_proj_{epilogue}",
    )(x_bf, w_bf, *extra)


def _split3(v):
    h = v.astype(BF16)
    r = v - h.astype(F32)
    m = r.astype(BF16)
    return h, m, (r - m.astype(F32)).astype(BF16)


def _forget_kernel(x_ref, w_ref, b_ref, o_ref, carry_ref):
    i = pl.program_id(0)

    @pl.when(i == 0)
    def _():
        carry_ref[...] = jnp.zeros_like(carry_ref)

    x = x_ref[...]
    xh = x.astype(BF16)
    xl = (x - xh.astype(F32)).astype(BF16)
    w = w_ref[...]
    a = jnp.dot(xh, w, preferred_element_type=F32)
    b = jnp.dot(xl, w[:, :LANES], preferred_element_type=F32)
    z = a[:, :LANES] + a[:, LANES:] + b + b_ref[...]
    logf = jnp.minimum(z, 0.0) - jnp.log1p(jnp.exp(-jnp.abs(z)))
    ts = logf.shape[0]
    tri = (lax.broadcasted_iota(I32, (ts, ts), 1) <= lax.broadcasted_iota(I32, (ts, ts), 0)).astype(BF16)
    c = carry_ref[0:1, :]
    for piece in _split3(logf):
        c = c + jnp.dot(tri, piece, preferred_element_type=F32)
    o_ref[...] = c * LOG2E
    carry_ref[...] = jnp.broadcast_to(c[ts - 1:ts, :], carry_ref.shape)


def _forget_cumsum(x, w_f, b_f):
    s, d = x.shape
    nh = w_f.shape[1]
    wh = w_f.astype(BF16)
    wl = (w_f - wh.astype(F32)).astype(BF16)
    pad = lambda a: jnp.pad(a, ((0, 0), (0, LANES - nh)))
    w2 = jnp.concatenate([pad(wh), pad(wl)], axis=1)
    b2 = jnp.pad(b_f.astype(F32), (0, LANES - nh)).reshape(1, LANES)
    ts = min(FORGET_TS, s)
    return pl.pallas_call(
        _forget_kernel,
        out_shape=jax.ShapeDtypeStruct((s, LANES), F32),
        grid=(s // ts,),
        in_specs=[pl.BlockSpec((ts, d), lambda i: (i, 0)),
                  pl.BlockSpec((d, 2 * LANES), lambda i: (0, 0)),
                  pl.BlockSpec((1, LANES), lambda i: (0, 0))],
        out_specs=pl.BlockSpec((ts, LANES), lambda i: (i, 0)),
        scratch_shapes=[pltpu.VMEM((8, LANES), F32)],
        compiler_params=_params(("arbitrary",)),
        name="forget_cumsum",
    )(x, w2, b2)


def _attend(qi, t, n_maps, scores, pv, st, p, acc, finalize):
    maps = range(n_maps)

    def score_stage(b, slot):
        cmax = []
        for j in maps:
            s = scores(b, j)
            st[slot][j] = s
            cmax.append(jnp.max(s, axis=0, keepdims=True))
        return tuple(cmax)

    def softmax_stage(slot, cmax, m, l, mask=None):
        a_new, m_new, l_new = [], [], []
        for j in maps:
            s = st[slot][j]
            if mask is not None:
                s = jnp.where(mask, s, NEG)
                cm = jnp.max(s, axis=0, keepdims=True)
            else:
                cm = cmax[j]
            mj = jnp.maximum(m[j], cm)
            aj = jnp.exp2(m[j] - mj)
            pj = jnp.exp2(s - mj)
            p[slot][j] = pj.astype(BF16)
            a_new.append(aj)
            m_new.append(mj)
            l_new.append(aj * l[j] + jnp.sum(pj, axis=0, keepdims=True))
        return tuple(a_new), tuple(m_new), tuple(l_new)

    def pv_stage(slot, a, b):
        for j in maps:
            acc[j] = a[j] * acc[j] + pv(p[slot][j], b)

    def step(i, slot, carry):
        cmax, a, m, l = carry
        pv_stage(1 - slot, a, jnp.maximum(i - 1, 0))
        cmax_next = score_stage(i + 1, 1 - slot)
        a, m, l = softmax_stage(slot, cmax, m, l)
        return cmax_next, a, m, l

    def finish(slot, carry):
        _, a, m, l = carry
        pv_stage(1 - slot, a, jnp.maximum(qi - 1, 0))
        mask = lax.broadcasted_iota(I32, (t, t), 1) >= lax.broadcasted_iota(I32, (t, t), 0)
        a, m, l = softmax_stage(slot, None, m, l, mask)
        pv_stage(slot, a, qi)
        finalize(l)

    p[1][...] = jnp.zeros(p[1].shape, p[1].dtype)
    acc[...] = jnp.zeros(acc.shape, acc.dtype)
    row = lambda v: tuple(jnp.full((1, t), v, F32) for _ in maps)
    carry = (score_stage(0, 0), row(1.0), row(NEG), row(0.0))
    carry = lax.fori_loop(0, qi // 2, lambda h, c: step(2 * h + 1, 1, step(2 * h, 0, c)), carry)

    @pl.when(qi % 2 == 0)
    def _():
        finish(0, carry)

    @pl.when(qi % 2 == 1)
    def _():
        finish(1, step(qi - 1, 0, carry))


def _fox_kernel(q_ref, k_ref, v_ref, c_ref, o_ref, kaug, vt, st0, st1, p0, p1, acc, *, t):
    h, qi = pl.program_id(0), pl.program_id(1)
    n_chunks = k_ref.shape[0] // t

    @pl.when(qi == 0)
    def _():
        def chunk(c, carry):
            rows = pl.ds(pl.multiple_of(c * t, t), t)
            lane = lax.broadcasted_iota(I32, (t, LANES), 1)
            neg_c = -jnp.sum(jnp.where(lane == h, c_ref[rows, :], 0.0), axis=1, keepdims=True)
            hi, mid, lo = (x.astype(F32) for x in _split3(neg_c))
            aug = jnp.where(lane == 0, hi, jnp.where(lane == 1, mid, jnp.where(lane == 2, lo, 0.0)))
            kaug[rows, :HEAD_DIM] = k_ref[rows, :]
            kaug[rows, HEAD_DIM:] = aug.astype(BF16)
            vt[:, rows] = v_ref[rows, :].astype(F32).T.astype(BF16)
            return carry

        lax.fori_loop(0, n_chunks, chunk, 0)

    ones = (lax.broadcasted_iota(I32, (HEAD_DIM, t), 0) < 3).astype(F32)
    qt = jnp.concatenate([q_ref[...].astype(F32).T, ones], axis=0).astype(BF16)
    blk = lambda b: pl.ds(pl.multiple_of(b * t, t), t)
    scores = lambda b, j: jnp.dot(kaug[blk(b), :], qt, preferred_element_type=F32)
    pv = lambda pb, b: jnp.dot(vt[:, blk(b)], pb, preferred_element_type=F32)

    def finalize(l):
        o_ref[...] = (acc[0] / l[0]).T.astype(o_ref.dtype)

    _attend(qi, t, 1, scores, pv, (st0, st1), (p0, p1), acc, finalize)


def _fox_attention(lin, fcum, n_heads, col0):
    s = lin.shape[0]
    t = min(ATT_TQ, s)
    return pl.pallas_call(
        functools.partial(_fox_kernel, t=t),
        out_shape=jax.ShapeDtypeStruct((s, n_heads * HEAD_DIM), BF16),
        grid=(n_heads, s // t),
        in_specs=[pl.BlockSpec((t, HEAD_DIM), lambda h, i: (i, col0 + h)),
                  pl.BlockSpec((s, HEAD_DIM), lambda h, i: (0, col0 + n_heads + h)),
                  pl.BlockSpec((s, HEAD_DIM), lambda h, i: (0, col0 + 2 * n_heads + h)),
                  pl.BlockSpec((s, LANES), lambda h, i: (0, 0))],
        out_specs=pl.BlockSpec((t, HEAD_DIM), lambda h, i: (i, h)),
        scratch_shapes=[pltpu.VMEM((s, 2 * HEAD_DIM), BF16), pltpu.VMEM((HEAD_DIM, s), BF16),
                        pltpu.VMEM((1, t, t), F32), pltpu.VMEM((1, t, t), F32),
                        pltpu.VMEM((1, t, t), BF16), pltpu.VMEM((1, t, t), BF16),
                        pltpu.VMEM((1, HEAD_DIM, t), F32)],
        compiler_params=_params(("arbitrary", "arbitrary")),
        name="fox_attention",
    )(lin, lin, lin, fcum)


def _diff_kernel(q_ref, k_ref, v_ref, lam_ref, g_ref, o_ref, vt, st0, st1, p0, p1, acc, *, t, lam_init):
    qi = pl.program_id(1)
    n_chunks = k_ref.shape[0] // t

    @pl.when(qi == 0)
    def _():
        def chunk(c, carry):
            rows = pl.ds(pl.multiple_of(c * t, t), t)
            vt[:, rows] = v_ref[rows, :].astype(F32).T.astype(BF16)
            return carry

        lax.fori_loop(0, n_chunks, chunk, 0)

    qt = q_ref[...].astype(F32).T.astype(BF16)
    blk = lambda b: pl.ds(pl.multiple_of(b * t, t), t)
    scores = lambda b, j: jnp.dot(k_ref[blk(b), j * HEAD_DIM:(j + 1) * HEAD_DIM],
                                  qt[j * HEAD_DIM:(j + 1) * HEAD_DIM], preferred_element_type=F32)
    pv = lambda pb, b: jnp.dot(vt[:, blk(b)], pb, preferred_element_type=F32)

    def finalize(l):
        lam_v = lam_ref[...]
        lam = (jnp.exp(jnp.sum(lam_v[0:1] * lam_v[1:2], axis=1, keepdims=True))
               - jnp.exp(jnp.sum(lam_v[2:3] * lam_v[3:4], axis=1, keepdims=True)) + lam_init)
        o = (acc[0] / l[0] - lam * (acc[1] / l[1])).T
        o = o * lax.rsqrt(jnp.mean(o * o, axis=1, keepdims=True) + LN_EPS) * g_ref[0] * (1.0 - lam_init)
        o_ref[...] = o.astype(o_ref.dtype)

    _attend(qi, t, 2, scores, pv, (st0, st1), (p0, p1), acc, finalize)


def _diff_attention(qk, lin, lam_rows, norm_g, n_heads, lam_init):
    s = qk.shape[0]
    t = min(ATT_TQ, s)
    return pl.pallas_call(
        functools.partial(_diff_kernel, t=t, lam_init=lam_init),
        out_shape=jax.ShapeDtypeStruct((s, n_heads * DIFF_V_DIM), BF16),
        grid=(n_heads, s // t),
        in_specs=[pl.BlockSpec((t, DIFF_V_DIM), lambda h, i: (i, h)),
                  pl.BlockSpec((s, DIFF_V_DIM), lambda h, i: (0, n_heads + h)),
                  pl.BlockSpec((s, DIFF_V_DIM), lambda h, i: (0, h)),
                  pl.BlockSpec((4, HEAD_DIM), lambda h, i: (0, 0)),
                  pl.BlockSpec((1, 1, DIFF_V_DIM), lambda h, i: (h, 0, 0))],
        out_specs=pl.BlockSpec((t, DIFF_V_DIM), lambda h, i: (i, h)),
        scratch_shapes=[pltpu.VMEM((DIFF_V_DIM, s), BF16),
                        pltpu.VMEM((2, t, t), F32), pltpu.VMEM((2, t, t), F32),
                        pltpu.VMEM((2, t, t), BF16), pltpu.VMEM((2, t, t), BF16),
                        pltpu.VMEM((2, DIFF_V_DIM, t), F32)],
        compiler_params=_params(("arbitrary", "arbitrary")),
        name="diff_attention",
    )(qk, qk, lin, lam_rows, norm_g.reshape(n_heads, 1, DIFF_V_DIM))


def _merge_kernel(od_ref, of_ref, gd_ref, gf_ref, wd_ref, wf_ref, o_ref):
    ud = jnp.dot(od_ref[...], wd_ref[...], preferred_element_type=F32)
    uf = jnp.dot(of_ref[...], wf_ref[...], preferred_element_type=F32)
    o_ref[...] = (gd_ref[...].astype(F32) * ud + gf_ref[...].astype(F32) * uf).astype(o_ref.dtype)


def _merge(o_diff, o_fox, gates, wd, wf):
    s, d = o_diff.shape[0], wd.shape[1]
    tm = min(MERGE_TM, s)
    row = lambda c: pl.BlockSpec((tm, c), lambda i: (i, 0))
    full = lambda a: pl.BlockSpec(a.shape, lambda i: (0, 0))
    return pl.pallas_call(
        _merge_kernel,
        out_shape=jax.ShapeDtypeStruct((s, d), BF16),
        grid=(s // tm,),
        in_specs=[row(o_diff.shape[1]), row(o_fox.shape[1]),
                  pl.BlockSpec((tm, d), lambda i: (i, 0)), pl.BlockSpec((tm, d), lambda i: (i, 1)),
                  full(wd), full(wf)],
        out_specs=row(d),
        compiler_params=_params(("parallel",)),
        name="merge_branches",
    )(o_diff, o_fox, gates, gates, wd, wf)


def _layer_norm(y, g, b):
    mu = jnp.mean(y, axis=1, keepdims=True)
    yc = y - mu
    var = jnp.mean(yc * yc, axis=1, keepdims=True)
    return yc * lax.rsqrt(var + LN_EPS) * g + b


def _route(lg):
    lane = lax.broadcasted_iota(I32, lg.shape, 1)
    far = jnp.int32(4 * LANES)
    is_g = lane < N_GROUPS
    gl = jnp.where(is_g, lg, NEG)
    gmax = jnp.max(gl, axis=1, keepdims=True)
    gidx = jnp.min(jnp.where(gl == gmax, lane, far), axis=1, keepdims=True)
    top_gp = 1.0 / jnp.sum(jnp.where(is_g, jnp.exp(gl - gmax), 0.0), axis=1, keepdims=True)
    lo = N_GROUPS + gidx * EXPERTS_PER_GROUP
    el = jnp.where((lane >= lo) & (lane < lo + EXPERTS_PER_GROUP), lg, NEG)
    m1 = jnp.max(el, axis=1, keepdims=True)
    i1 = jnp.min(jnp.where(el == m1, lane, far), axis=1, keepdims=True)
    el2 = jnp.where(lane == i1, 2.0 * NEG, el)
    m2 = jnp.max(el2, axis=1, keepdims=True)
    i2 = jnp.min(jnp.where(el2 == m2, lane, far), axis=1, keepdims=True)
    d = jnp.exp(m2 - m1)
    w1 = top_gp / (1.0 + d)
    w2 = w1 * d
    ids = jnp.where(lane == 0, i1 - N_GROUPS, jnp.where(lane == 1, i2 - N_GROUPS, 0))
    wts = jnp.where(lane == 0, w1, jnp.where(lane == 1, w2, 0.0))
    return ids, wts


def _outln_kernel(mg_ref, x_ref, wo_ref, g_ref, b_ref, wr_ref, br_ref, x1_ref, ids_ref, wts_ref, *, alpha):
    mix = jnp.dot(mg_ref[...], wo_ref[...], preferred_element_type=F32)
    x1 = _layer_norm(alpha * x_ref[...] + mix, g_ref[...], b_ref[...])
    x1_ref[...] = x1
    xh = x1.astype(BF16)
    xl = (x1 - xh.astype(F32)).astype(BF16)
    wr = wr_ref[...]
    a = jnp.dot(xh, wr, preferred_element_type=F32)
    b = jnp.dot(xl, wr[:, :LANES], preferred_element_type=F32)
    ids, wts = _route(a[:, :LANES] + a[:, LANES:] + b + br_ref[...])
    ids_ref[...] = ids
    wts_ref[...] = wts


def _out_ln_route(merged, x, w_out, g, b, wr2, br, alpha):
    s, d = x.shape
    tm = min(OUTLN_TM, s)
    row = lambda c: pl.BlockSpec((tm, c), lambda i: (i, 0))
    full = lambda a: pl.BlockSpec(a.shape, lambda i: (0, 0))
    return pl.pallas_call(
        functools.partial(_outln_kernel, alpha=alpha),
        out_shape=(jax.ShapeDtypeStruct((s, d), F32), jax.ShapeDtypeStruct((s, LANES), I32),
                   jax.ShapeDtypeStruct((s, LANES), F32)),
        grid=(s // tm,),
        in_specs=[row(d), row(d), full(w_out), full(g), full(b), full(wr2), full(br)],
        out_specs=(row(d), row(LANES), row(LANES)),
        compiler_params=_params(("parallel",)),
        name="out_proj_ln_route",
    )(merged, x, w_out, g, b, wr2, br)


def _rank_kernel(ids_ref, dest_ref, bexp_ref, meta_ref, cnt_ref, carry_ref, start_ref, *, block, n_blocks):
    ph, i = pl.program_id(0), pl.program_id(1)
    ids = ids_ref[...]
    tb = ids.shape[0]
    lane = lax.broadcasted_iota(I32, (tb, LANES), 1)
    oh1 = (lane == ids[:, 0:1]).astype(F32)
    oh2 = (lane == ids[:, 1:2]).astype(F32)
    oh = oh1 + oh2
    colsum = jnp.sum(oh, axis=0, keepdims=True)

    @pl.when((ph == 0) & (i == 0))
    def _():
        cnt_ref[...] = jnp.zeros_like(cnt_ref)

    @pl.when(ph == 0)
    def _():
        cnt_ref[...] += jnp.broadcast_to(colsum, cnt_ref.shape)

    @pl.when((ph == 1) & (i == 0))
    def _():
        cnt = cnt_ref[0:1, :]
        nblk = jnp.floor((cnt + (block - 1)) * (1.0 / block))
        r = lax.broadcasted_iota(I32, (LANES, LANES), 0)
        c = lax.broadcasted_iota(I32, (LANES, LANES), 1)
        upper = (r < c).astype(BF16)
        nb8 = jnp.broadcast_to(nblk, (8, LANES)).astype(BF16)
        bstart = jnp.dot(nb8, upper, preferred_element_type=F32)
        start_ref[...] = bstart * block
        carry_ref[...] = jnp.zeros_like(carry_ref)
        lane1 = lax.broadcasted_iota(I32, (1, LANES), 1)
        bend = jnp.where(lane1 < N_EXPERTS, bstart[0:1] + nblk, 4.0 * n_blocks)
        bidx = lax.broadcasted_iota(I32, (n_blocks, LANES), 0).astype(F32)
        be = jnp.sum((jnp.broadcast_to(bend, (n_blocks, LANES)) <= bidx).astype(F32), axis=1, keepdims=True)
        be = jnp.minimum(be, N_EXPERTS - 1.0)
        lane_b = lax.broadcasted_iota(I32, (n_blocks, LANES), 1).astype(F32)
        owns = jnp.broadcast_to((nblk > 0.0) & (lane1 < N_EXPERTS), (n_blocks, LANES))
        nxt = jnp.min(jnp.where(owns & (lane_b > be), lane_b, 1.0 * LANES), axis=1, keepdims=True)
        nxt = jnp.where(nxt >= 1.0 * LANES, -1.0, nxt)
        before = jnp.sum((owns & (lane_b < be)).astype(F32), axis=1, keepdims=True)
        parity = before - 2.0 * jnp.floor(before * 0.5)
        bexp_ref[...] = jnp.where(lane_b == 0.0, be, jnp.where(lane_b == 1.0, nxt, parity)).astype(I32)
        total = jnp.sum(jnp.where(lane1 < N_EXPERTS, nblk, 0.0), axis=1, keepdims=True)
        row = lax.broadcasted_iota(I32, (8, LANES), 0)
        first_pad = start_ref[0:1, :] + cnt
        meta = jnp.where(row == 0, jnp.broadcast_to(total, (8, LANES)),
                         jnp.where(row == 1, jnp.broadcast_to(first_pad, (8, LANES)),
                                   jnp.broadcast_to(bstart[0:1] * block + nblk * block, (8, LANES))))
        meta_ref[...] = meta.astype(I32)

    @pl.when(ph == 1)
    def _():
        rr = lax.broadcasted_iota(I32, (tb, tb), 0)
        cc = lax.broadcasted_iota(I32, (tb, tb), 1)
        lower = (cc < rr).astype(BF16)
        prefix = jnp.dot(lower, oh.astype(BF16), preferred_element_type=F32)
        pos = prefix + carry_ref[0:1, :] + start_ref[0:1, :]
        d1 = jnp.sum(pos * oh1, axis=1, keepdims=True)
        d2 = jnp.sum(pos * oh2, axis=1, keepdims=True)
        dest_ref[...] = jnp.where(lane == 0, d1, jnp.where(lane == 1, d2, 0.0)).astype(I32)
        carry_ref[...] += jnp.broadcast_to(colsum, carry_ref.shape)


def _rank(ids, block, n_blocks):
    t = ids.shape[0]
    tb = min(RANK_TB, t)
    return pl.pallas_call(
        functools.partial(_rank_kernel, block=block, n_blocks=n_blocks),
        out_shape=(jax.ShapeDtypeStruct((t, LANES), I32), jax.ShapeDtypeStruct((n_blocks, LANES), I32),
                   jax.ShapeDtypeStruct((8, LANES), I32)),
        grid=(2, t // tb),
        in_specs=[pl.BlockSpec((tb, LANES), lambda p, i: (i, 0))],
        out_specs=(pl.BlockSpec((tb, LANES), lambda p, i: (i * p, 0)),
                   pl.BlockSpec((n_blocks, LANES), lambda p, i: (0, 0)),
                   pl.BlockSpec((8, LANES), lambda p, i: (0, 0))),
        scratch_shapes=[pltpu.VMEM((8, LANES), F32)] * 3,
        compiler_params=_params(("arbitrary", "arbitrary")),
        name="moe_rank",
    )(ids)


def _invert_kernel(dest_ref, zeros_hbm, inv_hbm, inv_ref, sem):
    i = pl.program_id(0)
    n = dest_ref.shape[0]

    @pl.when(i == 0)
    def _():
        fill = pltpu.make_async_copy(zeros_hbm, inv_ref, sem)
        fill.start()
        fill.wait()

    def body(j, c):
        inv_ref[dest_ref[j]] = i * n + j
        return c

    lax.fori_loop(0, n, body, 0, unroll=8)

    @pl.when(i == pl.num_programs(0) - 1)
    def _():
        flush = pltpu.make_async_copy(inv_ref, inv_hbm, sem)
        flush.start()
        flush.wait()


def _invert(dest_flat, rows):
    a = dest_flat.shape[0]
    chunk = min(INVERT_CHUNK, a)
    return pl.pallas_call(
        _invert_kernel,
        out_shape=jax.ShapeDtypeStruct((rows,), I32),
        grid=(a // chunk,),
        in_specs=[pl.BlockSpec((chunk,), lambda i: (i,), memory_space=pltpu.SMEM),
                  pl.BlockSpec(memory_space=pl.ANY)],
        out_specs=pl.BlockSpec(memory_space=pl.ANY),
        scratch_shapes=[pltpu.SMEM((rows,), I32), pltpu.SemaphoreType.DMA],
        compiler_params=_params(("arbitrary",)),
        name="moe_invert",
    )(dest_flat, jnp.zeros((rows,), I32))


def _expert_kernel(tab_ref, nact_ref, inv0_ref, inv1_ref, inv2_ref, x1_ref, wg_hbm, wu_hbm, wd_hbm, ys_ref,
                   xbuf, wg_f, wu_f, wd_f, wg_b, wu_b, wd_b, gsem, wsem):
    b = pl.program_id(0)
    n_active = nact_ref[0]
    active = b < n_active
    block = xbuf.shape[1]
    expert, next_expert, wslot = tab_ref[0, b], tab_ref[1, b], tab_ref[2, b]
    first_of_expert = (b == 0) | (tab_ref[0, jnp.maximum(b - 1, 0)] != expert)

    def row_copy(inv_ref, blk, r):
        token = lax.shift_right_logical(inv_ref[0, 0, r], 1)
        slot = lax.rem(blk, GATHER_DEPTH)
        return pltpu.make_async_copy(x1_ref.at[pl.ds(token, 1)], xbuf.at[slot, pl.ds(r, 1)], gsem.at[slot])

    def gather(inv_ref, blk):
        for r in range(block):
            row_copy(inv_ref, blk, r).start()

    def weight_copies(e, s):
        return (pltpu.make_async_copy(wg_hbm.at[e], wg_f.at[s], wsem.at[s, 0]),
                pltpu.make_async_copy(wu_hbm.at[e], wu_f.at[s], wsem.at[s, 1]),
                pltpu.make_async_copy(wd_hbm.at[e], wd_f.at[s], wsem.at[s, 2]))

    @pl.when(b == 0)
    def _():
        for c in weight_copies(expert, wslot):
            c.start()
        gather(inv0_ref, b)

    @pl.when((b == 0) & (n_active > 1))
    def _():
        gather(inv1_ref, b + 1)

    @pl.when(b + 2 < n_active)
    def _():
        gather(inv2_ref, b + 2)

    @pl.when(active & first_of_expert)
    def _():
        for c in weight_copies(expert, wslot):
            c.wait()

        @pl.when(next_expert >= 0)
        def _():
            for c in weight_copies(next_expert, 1 - wslot):
                c.start()

        wg_b[...] = wg_f[wslot].astype(BF16)
        wu_b[...] = wu_f[wslot].astype(BF16)
        wd_b[...] = wd_f[wslot].astype(BF16)

    @pl.when(active)
    def _():
        for r in range(block):
            row_copy(inv0_ref, b, r).wait()
        x = xbuf[lax.rem(b, GATHER_DEPTH)].astype(BF16)
        g = jnp.dot(x, wg_b[...], preferred_element_type=F32)
        u = jnp.dot(x, wu_b[...], preferred_element_type=F32)
        h = (g / (1.0 + jnp.exp(-g)) * u).astype(BF16)
        ys_ref[...] = jnp.dot(h, wd_b[...], preferred_element_type=F32)

    @pl.when(jnp.logical_not(active))
    def _():
        ys_ref[...] = jnp.zeros_like(ys_ref)


def _experts(x1, inv, tab, nact, w_gate, w_up, w_down, block):
    n_blocks = tab.shape[1]
    d, ff = w_gate.shape[1], w_gate.shape[2]
    inv3 = inv.reshape(n_blocks, 1, block)
    ahead = lambda k: pl.BlockSpec((1, 1, block), lambda b, tb, na: (jnp.minimum(b + k, n_blocks - 1), 0, 0),
                                   memory_space=pltpu.SMEM)
    hbm = pl.BlockSpec(memory_space=pl.ANY)
    return pl.pallas_call(
        _expert_kernel,
        out_shape=jax.ShapeDtypeStruct((n_blocks * block, d), F32),
        grid_spec=pltpu.PrefetchScalarGridSpec(
            num_scalar_prefetch=2, grid=(n_blocks,),
            in_specs=[ahead(0), ahead(1), ahead(2), hbm, hbm, hbm, hbm],
            out_specs=pl.BlockSpec((block, d), lambda b, tb, na: (b, 0)),
            scratch_shapes=[pltpu.VMEM((GATHER_DEPTH, block, d), F32),
                            pltpu.VMEM((2, d, ff), F32), pltpu.VMEM((2, d, ff), F32), pltpu.VMEM((2, ff, d), F32),
                            pltpu.VMEM((d, ff), BF16), pltpu.VMEM((d, ff), BF16), pltpu.VMEM((ff, d), BF16),
                            pltpu.SemaphoreType.DMA((GATHER_DEPTH,)), pltpu.SemaphoreType.DMA((2, 3))]),
        compiler_params=_params(("arbitrary",)),
        name="moe_experts",
    )(tab, nact, inv3, inv3, inv3, x1, w_gate, w_up, w_down)


def _combine_kernel(dest_ref, x1_ref, wts_ref, g_ref, b_ref, ys_ref, o_ref, buf, sem, *, alpha):
    tm = x1_ref.shape[0]

    def row_copy(r, k):
        return pltpu.make_async_copy(ys_ref.at[pl.ds(dest_ref[0, 0, TOP_K * r + k], 1)], buf.at[k, pl.ds(r, 1)], sem)

    def start(r, c):
        for k in range(TOP_K):
            row_copy(r, k).start()
        return c

    lax.fori_loop(0, tm, start, 0, unroll=8)
    for r in range(tm):
        for k in range(TOP_K):
            row_copy(r, k).wait()
    w = wts_ref[...]
    ffn = w[:, 0:1] * buf[0] + w[:, 1:2] * buf[1]
    o_ref[...] = _layer_norm(alpha * x1_ref[...] + ffn, g_ref[...], b_ref[...])


def _combine(ys, dest3, x1, wts, g, b, alpha):
    t, d = x1.shape
    tm = dest3.shape[2] // TOP_K
    row = lambda c: pl.BlockSpec((tm, c), lambda i: (i, 0))
    full = lambda a: pl.BlockSpec(a.shape, lambda i: (0, 0))
    return pl.pallas_call(
        functools.partial(_combine_kernel, alpha=alpha),
        out_shape=jax.ShapeDtypeStruct((t, d), F32),
        grid=(t // tm,),
        in_specs=[pl.BlockSpec((1, 1, TOP_K * tm), lambda i: (i, 0, 0), memory_space=pltpu.SMEM),
                  row(d), row(LANES), full(g), full(b), pl.BlockSpec(memory_space=pl.ANY)],
        out_specs=row(d),
        scratch_shapes=[pltpu.VMEM((TOP_K, tm, d), F32), pltpu.SemaphoreType.DMA],
        compiler_params=_params(("arbitrary",)),
        name="moe_combine_ln",
    )(dest3, x1, wts, g, b, ys)


def _rope_tables(s, q_scale):
    inv_freq = ROPE_THETA ** (-jnp.arange(0, HEAD_DIM, 2, dtype=F32) / HEAD_DIM)
    ang = jnp.arange(s, dtype=F32)[:, None] * inv_freq[None, :]
    cos, sin = jnp.cos(ang), jnp.sin(ang)
    cosf = jnp.concatenate([cos, cos], axis=1)
    sinf = jnp.concatenate([-sin, sin], axis=1)
    return jnp.stack([cosf * q_scale, cosf]), jnp.stack([sinf * q_scale, sinf])


def _split_bf16_pair(w, width):
    hi = w.astype(BF16)
    lo = (w - hi.astype(F32)).astype(BF16)
    pad = lambda a: jnp.pad(a, ((0, 0), (0, width - a.shape[1])))
    return jnp.concatenate([pad(hi), pad(lo)], axis=1)


def _layer(x2, layer, depth, w_in, b_forget, lam_q1, lam_k1, lam_q2, lam_k2, diff_norm_g, w_proj_diff,
           w_proj_fox, w_out, ln1_g, ln1_b, w_rg, b_rg, w_re, b_re, w_gate, w_up, w_down, ln2_g, ln2_b):
    s, d = x2.shape
    alpha = (2 * depth) ** 0.25
    lam_init = 0.8 - 0.6 * math.exp(-0.3 * layer)
    n_diff, n_fox = diff_norm_g.shape[0], b_forget.shape[0]
    qk_cols = n_diff * 2 * HEAD_DIM
    lin_cols = n_diff * DIFF_V_DIM + 3 * n_fox * HEAD_DIM
    q_scale = HEAD_DIM ** -0.5 * LOG2E

    x_bf = x2.astype(BF16)
    cos_t, sin_t = _rope_tables(s, q_scale)
    tm = min(PROJ_TM, s)
    tab = pl.BlockSpec((1, tm, HEAD_DIM), lambda j, i: (j, i, 0))
    assert qk_cols == PROJ_TN, "one column block each for dq and dk selects the scaled / plain rotary table"
    qk = _proj(x_bf, w_in[:, :2 * qk_cols].astype(BF16), "rope", (cos_t, sin_t), (tab, tab))
    col_scale = jnp.ones((1, lin_cols), F32).at[:, n_diff * DIFF_V_DIM:n_diff * DIFF_V_DIM + n_fox * HEAD_DIM].set(q_scale)
    lin = _proj(x_bf, w_in[:, 2 * qk_cols:2 * qk_cols + lin_cols].astype(BF16), "scale", (col_scale,),
                (pl.BlockSpec((1, min(PROJ_TN, lin_cols)), lambda j, i: (0, j)),))
    f0 = 2 * qk_cols + lin_cols
    gates = _proj(x_bf, w_in[:, f0 + n_fox:].astype(BF16), "sigmoid")

    fcum = _forget_cumsum(x2, w_in[:, f0:f0 + n_fox], b_forget)

    lam_rows = jnp.stack([lam_q1, lam_k1, lam_q2, lam_k2]).astype(F32)
    o_diff = _diff_attention(qk, lin, lam_rows, diff_norm_g.astype(F32), n_diff, lam_init)
    o_fox = _fox_attention(lin, fcum, n_fox, n_diff * DIFF_V_DIM // HEAD_DIM)

    merged = _merge(o_diff, o_fox, gates, w_proj_diff.astype(BF16), w_proj_fox.astype(BF16))
    w_router = jnp.concatenate([w_rg, jnp.moveaxis(w_re, 0, 1).reshape(d, N_EXPERTS)], axis=1)
    b_router = jnp.pad(jnp.concatenate([b_rg, b_re.reshape(N_EXPERTS)]).astype(F32),
                       (0, LANES - N_GROUPS - N_EXPERTS)).reshape(1, LANES)
    x1, ids, wts = _out_ln_route(merged, x2, w_out.astype(BF16), ln1_g.reshape(1, d), ln1_b.reshape(1, d),
                                 _split_bf16_pair(w_router, LANES), b_router, alpha)

    block = MOE_BLOCK
    n_blocks = -(-(s * TOP_K) // block) + N_EXPERTS
    dest, bexp, meta = _rank(ids, block, n_blocks)
    tmr = min(ROW_TM, s)
    dest2 = dest[:, :TOP_K]
    inv = _invert(dest2.reshape(s * TOP_K), n_blocks * block)
    ys = _experts(x1, inv, bexp[:, :3].T, meta[0, :1], w_gate, w_up, w_down, block)
    dest3 = dest2.reshape(s // tmr, 1, TOP_K * tmr)
    return _combine(ys, dest3, x1, wts, ln2_g.reshape(1, d), ln2_b.reshape(1, d), alpha)


def kernel(x, w_in, b_forget, lam_q1, lam_k1, lam_q2, lam_k2, diff_norm_g, w_proj_diff, w_proj_fox, w_out,
           ln1_g, ln1_b, w_router_group, b_router_group, w_router_expert, b_router_expert, w_gate, w_up,
           w_down, ln2_g, ln2_b):
    batch, s, d = x.shape
    depth = w_in.shape[0]
    params = (w_in, b_forget, lam_q1, lam_k1, lam_q2, lam_k2, diff_norm_g, w_proj_diff, w_proj_fox, w_out,
              ln1_g, ln1_b, w_router_group, b_router_group, w_router_expert, b_router_expert, w_gate, w_up,
              w_down, ln2_g, ln2_b)
    outs = []
    for bi in range(batch):
        h = x[bi]
        for layer in range(depth):
            h = _layer(h, layer, depth, *(p[layer] for p in params))
        outs.append(h)
    return jnp.stack(outs)
```

```python
import functools
import math

import jax
import jax.numpy as jnp
from jax import lax
from jax.experimental import pallas as pl
from jax.experimental.pallas import tpu as pltpu

F32, BF16, I32 = jnp.float32, jnp.bfloat16, jnp.int32

HEAD_DIM = 128
DIFF_V_DIM = 2 * HEAD_DIM
ROPE_THETA = 10000.0
N_GROUPS = 4
EXPERTS_PER_GROUP = 8
N_EXPERTS = N_GROUPS * EXPERTS_PER_GROUP
TOP_K = 2
LN_EPS = 1e-5
LOG2E = 1.4426950408889634
NEG = -1e30
LANES = 128
SUBLANES = 8
VMEM_LIMIT = 56 * 1024 * 1024

MOE_BLOCK = 128
PROJ_TM, PROJ_TN = 1024, 1024
PROJ_CAST_ROWS = 256
ATT_TQ = 512
FORGET_TS = 512
MERGE_TM = 512
OUTLN_TM = 256
RANK_TB = 512
ROW_TM = 256
INVERT_CHUNK = 1024
GATHER_DEPTH = 3


def _params(sem, vmem=VMEM_LIMIT):
    return pltpu.CompilerParams(dimension_semantics=sem, vmem_limit_bytes=vmem)


def _proj_kernel(x_ref, wt_hbm, *rest, epilogue, row0, q_scale):
    *extra, o_ref, w_f32, w_bf, sem = rest
    j, i = pl.program_id(0), pl.program_id(1)
    tn = w_bf.shape[0]
    slot = lax.rem(j, 2)

    def fetch(jj, s):
        rows = pl.ds(pl.multiple_of(row0 + jj * tn, SUBLANES), tn)
        return pltpu.make_async_copy(wt_hbm.at[rows, :], w_f32.at[s], sem.at[s])

    @pl.when(i == 0)
    def _():
        @pl.when(j == 0)
        def _():
            fetch(j, slot).start()

        fetch(j, slot).wait()

        @pl.when(j + 1 < pl.num_programs(0))
        def _():
            fetch(j + 1, 1 - slot).start()

        rows = min(PROJ_CAST_ROWS, tn)

        def chunk(c, carry):
            r = pl.ds(pl.multiple_of(c * rows, rows), rows)
            w_bf[r, :] = w_f32[slot, r, :].astype(BF16)
            return carry

        lax.fori_loop(0, tn // rows, chunk, 0)

    acc = lax.dot_general(x_ref[...], w_bf[...], (((1,), (1,)), ((), ())), preferred_element_type=F32)
    if epilogue == "rope":
        cos_ref, sin_ref = extra
        scale = jnp.where(j == 0, q_scale, 1.0)
        cosf, sinf = cos_ref[...] * scale, sin_ref[...] * scale
        for h in range(tn // HEAD_DIM):
            t = acc[:, h * HEAD_DIM:(h + 1) * HEAD_DIM]
            o_ref[:, h * HEAD_DIM:(h + 1) * HEAD_DIM] = (
                t * cosf + pltpu.roll(t, HEAD_DIM // 2, 1) * sinf).astype(o_ref.dtype)
    elif epilogue == "scale":
        o_ref[...] = (acc * extra[0][...]).astype(o_ref.dtype)
    else:
        o_ref[...] = (1.0 / (1.0 + jnp.exp(-acc))).astype(o_ref.dtype)


def _proj(x_bf, w_t, col0, n, epilogue, extra=(), extra_specs=(), q_scale=1.0):
    m, k = x_bf.shape
    tm, tn = min(PROJ_TM, m), min(PROJ_TN, n)
    assert col0 % SUBLANES == 0 and n % tn == 0
    return pl.pallas_call(
        functools.partial(_proj_kernel, epilogue=epilogue, row0=col0, q_scale=q_scale),
        out_shape=jax.ShapeDtypeStruct((m, n), BF16),
        grid=(n // tn, m // tm),
        in_specs=[pl.BlockSpec((tm, k), lambda j, i: (i, 0)), pl.BlockSpec(memory_space=pl.ANY), *extra_specs],
        out_specs=pl.BlockSpec((tm, tn), lambda j, i: (i, j)),
        scratch_shapes=[pltpu.VMEM((2, tn, k), F32), pltpu.VMEM((tn, k), BF16), pltpu.SemaphoreType.DMA((2,))],
        compiler_params=_params(("arbitrary", "arbitrary")),
        name=f"in_proj_{epilogue}",
    )(x_bf, w_t, *extra)


def _split3(v):
    h = v.astype(BF16)
    r = v - h.astype(F32)
    m = r.astype(BF16)
    return h, m, (r - m.astype(F32)).astype(BF16)


def _forget_kernel(x_ref, w_ref, b_ref, o_ref, xbf_ref, carry_ref):
    i = pl.program_id(0)

    @pl.when(i == 0)
    def _():
        carry_ref[...] = jnp.zeros_like(carry_ref)

    x = x_ref[...]
    xh = x.astype(BF16)
    xbf_ref[...] = xh
    xl = (x - xh.astype(F32)).astype(BF16)
    w = w_ref[...]
    a = jnp.dot(xh, w, preferred_element_type=F32)
    b = jnp.dot(xl, w[:, :LANES], preferred_element_type=F32)
    z = a[:, :LANES] + a[:, LANES:] + b + b_ref[...]
    logf = jnp.minimum(z, 0.0) - jnp.log1p(jnp.exp(-jnp.abs(z)))
    ts = logf.shape[0]
    tri = (lax.broadcasted_iota(I32, (ts, ts), 1) <= lax.broadcasted_iota(I32, (ts, ts), 0)).astype(BF16)
    c = carry_ref[0:1, :]
    for piece in _split3(logf):
        c = c + jnp.dot(tri, piece, preferred_element_type=F32)
    o_ref[...] = c * LOG2E
    carry_ref[...] = jnp.broadcast_to(c[ts - 1:ts, :], carry_ref.shape)


def _forget_cumsum(x, w_f, b_f):
    s, d = x.shape
    nh = w_f.shape[1]
    wh = w_f.astype(BF16)
    wl = (w_f - wh.astype(F32)).astype(BF16)
    pad = lambda a: jnp.pad(a, ((0, 0), (0, LANES - nh)))
    w2 = jnp.concatenate([pad(wh), pad(wl)], axis=1)
    b2 = jnp.pad(b_f.astype(F32), (0, LANES - nh)).reshape(1, LANES)
    ts = min(FORGET_TS, s)
    return pl.pallas_call(
        _forget_kernel,
        out_shape=(jax.ShapeDtypeStruct((s, LANES), F32), jax.ShapeDtypeStruct((s, d), BF16)),
        grid=(s // ts,),
        in_specs=[pl.BlockSpec((ts, d), lambda i: (i, 0)),
                  pl.BlockSpec((d, 2 * LANES), lambda i: (0, 0)),
                  pl.BlockSpec((1, LANES), lambda i: (0, 0))],
        out_specs=(pl.BlockSpec((ts, LANES), lambda i: (i, 0)), pl.BlockSpec((ts, d), lambda i: (i, 0))),
        scratch_shapes=[pltpu.VMEM((8, LANES), F32)],
        compiler_params=_params(("arbitrary",)),
        name="forget_cumsum",
    )(x, w2, b2)


def _attend(qi, t, n_maps, scores, pv, st, p, acc, finalize):
    maps = range(n_maps)

    def score_stage(b, slot):
        cmax = []
        for j in maps:
            s = scores(b, j)
            st[slot][j] = s
            cmax.append(jnp.max(s, axis=0, keepdims=True))
        return tuple(cmax)

    def softmax_stage(slot, cmax, m, l, mask=None):
        a_new, m_new, l_new = [], [], []
        for j in maps:
            s = st[slot][j]
            if mask is not None:
                s = jnp.where(mask, s, NEG)
                cm = jnp.max(s, axis=0, keepdims=True)
            else:
                cm = cmax[j]
            mj = jnp.maximum(m[j], cm)
            aj = jnp.exp2(m[j] - mj)
            pj = jnp.exp2(s - mj)
            p[slot][j] = pj.astype(BF16)
            a_new.append(aj)
            m_new.append(mj)
            l_new.append(aj * l[j] + jnp.sum(pj, axis=0, keepdims=True))
        return tuple(a_new), tuple(m_new), tuple(l_new)

    def pv_stage(slot, a, b):
        for j in maps:
            acc[j] = a[j] * acc[j] + pv(p[slot][j], b)

    def step(i, slot, carry):
        cmax, a, m, l = carry
        pv_stage(1 - slot, a, jnp.maximum(i - 1, 0))
        cmax_next = score_stage(i + 1, 1 - slot)
        a, m, l = softmax_stage(slot, cmax, m, l)
        return cmax_next, a, m, l

    def finish(slot, carry):
        _, a, m, l = carry
        pv_stage(1 - slot, a, jnp.maximum(qi - 1, 0))
        mask = lax.broadcasted_iota(I32, (t, t), 1) >= lax.broadcasted_iota(I32, (t, t), 0)
        a, m, l = softmax_stage(slot, None, m, l, mask)
        pv_stage(slot, a, qi)
        finalize(l)

    p[1][...] = jnp.zeros(p[1].shape, p[1].dtype)
    acc[...] = jnp.zeros(acc.shape, acc.dtype)
    row = lambda v: tuple(jnp.full((1, t), v, F32) for _ in maps)
    carry = (score_stage(0, 0), row(1.0), row(NEG), row(0.0))
    carry = lax.fori_loop(0, qi // 2, lambda h, c: step(2 * h + 1, 1, step(2 * h, 0, c)), carry)

    @pl.when(qi % 2 == 0)
    def _():
        finish(0, carry)

    @pl.when(qi % 2 == 1)
    def _():
        finish(1, step(qi - 1, 0, carry))


def _fox_kernel(q_ref, k_ref, v_ref, c_ref, o_ref, kaug, vt, st0, st1, p0, p1, acc, *, t):
    h, qi = pl.program_id(0), pl.program_id(1)
    n_chunks = k_ref.shape[0] // t

    @pl.when(qi == 0)
    def _():
        def chunk(c, carry):
            rows = pl.ds(pl.multiple_of(c * t, t), t)
            lane = lax.broadcasted_iota(I32, (t, LANES), 1)
            neg_c = -jnp.sum(jnp.where(lane == h, c_ref[rows, :], 0.0), axis=1, keepdims=True)
            hi, mid, lo = (x.astype(F32) for x in _split3(neg_c))
            aug = jnp.where(lane == 0, hi, jnp.where(lane == 1, mid, jnp.where(lane == 2, lo, 0.0)))
            kaug[rows, :HEAD_DIM] = k_ref[rows, :]
            kaug[rows, HEAD_DIM:] = aug.astype(BF16)
            vt[:, rows] = v_ref[rows, :].astype(F32).T.astype(BF16)
            return carry

        lax.fori_loop(0, n_chunks, chunk, 0)

    ones = (lax.broadcasted_iota(I32, (HEAD_DIM, t), 0) < 3).astype(F32)
    qt = jnp.concatenate([q_ref[...].astype(F32).T, ones], axis=0).astype(BF16)
    blk = lambda b: pl.ds(pl.multiple_of(b * t, t), t)
    scores = lambda b, j: jnp.dot(kaug[blk(b), :], qt, preferred_element_type=F32)
    pv = lambda pb, b: jnp.dot(vt[:, blk(b)], pb, preferred_element_type=F32)

    def finalize(l):
        o_ref[...] = (acc[0] / l[0]).T.astype(o_ref.dtype)

    _attend(qi, t, 1, scores, pv, (st0, st1), (p0, p1), acc, finalize)


def _fox_attention(lin, fcum, n_heads, col0):
    s = lin.shape[0]
    t = min(ATT_TQ, s)
    return pl.pallas_call(
        functools.partial(_fox_kernel, t=t),
        out_shape=jax.ShapeDtypeStruct((s, n_heads * HEAD_DIM), BF16),
        grid=(n_heads, s // t),
        in_specs=[pl.BlockSpec((t, HEAD_DIM), lambda h, i: (i, col0 + h)),
                  pl.BlockSpec((s, HEAD_DIM), lambda h, i: (0, col0 + n_heads + h)),
                  pl.BlockSpec((s, HEAD_DIM), lambda h, i: (0, col0 + 2 * n_heads + h)),
                  pl.BlockSpec((s, LANES), lambda h, i: (0, 0))],
        out_specs=pl.BlockSpec((t, HEAD_DIM), lambda h, i: (i, h)),
        scratch_shapes=[pltpu.VMEM((s, 2 * HEAD_DIM), BF16), pltpu.VMEM((HEAD_DIM, s), BF16),
                        pltpu.VMEM((1, t, t), F32), pltpu.VMEM((1, t, t), F32),
                        pltpu.VMEM((1, t, t), BF16), pltpu.VMEM((1, t, t), BF16),
                        pltpu.VMEM((1, HEAD_DIM, t), F32)],
        compiler_params=_params(("arbitrary", "arbitrary")),
        name="fox_attention",
    )(lin, lin, lin, fcum)


def _diff_kernel(q_ref, k_ref, v_ref, lam_ref, g_ref, o_ref, vt, st0, st1, p0, p1, acc, *, t, lam_init):
    qi = pl.program_id(1)
    n_chunks = k_ref.shape[0] // t

    @pl.when(qi == 0)
    def _():
        def chunk(c, carry):
            rows = pl.ds(pl.multiple_of(c * t, t), t)
            vt[:, rows] = v_ref[rows, :].astype(F32).T.astype(BF16)
            return carry

        lax.fori_loop(0, n_chunks, chunk, 0)

    qt = q_ref[...].astype(F32).T.astype(BF16)
    blk = lambda b: pl.ds(pl.multiple_of(b * t, t), t)
    scores = lambda b, j: jnp.dot(k_ref[blk(b), j * HEAD_DIM:(j + 1) * HEAD_DIM],
                                  qt[j * HEAD_DIM:(j + 1) * HEAD_DIM], preferred_element_type=F32)
    pv = lambda pb, b: jnp.dot(vt[:, blk(b)], pb, preferred_element_type=F32)

    def finalize(l):
        lam_v = lam_ref[...]
        lam = (jnp.exp(jnp.sum(lam_v[0:1] * lam_v[1:2], axis=1, keepdims=True))
               - jnp.exp(jnp.sum(lam_v[2:3] * lam_v[3:4], axis=1, keepdims=True)) + lam_init)
        o = (acc[0] / l[0] - lam * (acc[1] / l[1])).T
        o = o * lax.rsqrt(jnp.mean(o * o, axis=1, keepdims=True) + LN_EPS) * g_ref[0] * (1.0 - lam_init)
        o_ref[...] = o.astype(o_ref.dtype)

    _attend(qi, t, 2, scores, pv, (st0, st1), (p0, p1), acc, finalize)


def _diff_attention(qk, lin, lam_rows, norm_g, n_heads, lam_init):
    s = qk.shape[0]
    t = min(ATT_TQ, s)
    return pl.pallas_call(
        functools.partial(_diff_kernel, t=t, lam_init=lam_init),
        out_shape=jax.ShapeDtypeStruct((s, n_heads * DIFF_V_DIM), BF16),
        grid=(n_heads, s // t),
        in_specs=[pl.BlockSpec((t, DIFF_V_DIM), lambda h, i: (i, h)),
                  pl.BlockSpec((s, DIFF_V_DIM), lambda h, i: (0, n_heads + h)),
                  pl.BlockSpec((s, DIFF_V_DIM), lambda h, i: (0, h)),
                  pl.BlockSpec((4, HEAD_DIM), lambda h, i: (0, 0)),
                  pl.BlockSpec((1, 1, DIFF_V_DIM), lambda h, i: (h, 0, 0))],
        out_specs=pl.BlockSpec((t, DIFF_V_DIM), lambda h, i: (i, h)),
        scratch_shapes=[pltpu.VMEM((DIFF_V_DIM, s), BF16),
                        pltpu.VMEM((2, t, t), F32), pltpu.VMEM((2, t, t), F32),
                        pltpu.VMEM((2, t, t), BF16), pltpu.VMEM((2, t, t), BF16),
                        pltpu.VMEM((2, DIFF_V_DIM, t), F32)],
        compiler_params=_params(("arbitrary", "arbitrary")),
        name="diff_attention",
    )(qk, qk, lin, lam_rows, norm_g.reshape(n_heads, 1, DIFF_V_DIM))


def _merge_kernel(od_ref, of_ref, gd_ref, gf_ref, wd_ref, wf_ref, o_ref):
    ud = jnp.dot(od_ref[...], wd_ref[...], preferred_element_type=F32)
    uf = jnp.dot(of_ref[...], wf_ref[...], preferred_element_type=F32)
    o_ref[...] = (gd_ref[...].astype(F32) * ud + gf_ref[...].astype(F32) * uf).astype(o_ref.dtype)


def _merge(o_diff, o_fox, gates, wd, wf):
    s, d = o_diff.shape[0], wd.shape[1]
    tm = min(MERGE_TM, s)
    row = lambda c: pl.BlockSpec((tm, c), lambda i: (i, 0))
    full = lambda a: pl.BlockSpec(a.shape, lambda i: (0, 0))
    return pl.pallas_call(
        _merge_kernel,
        out_shape=jax.ShapeDtypeStruct((s, d), BF16),
        grid=(s // tm,),
        in_specs=[row(o_diff.shape[1]), row(o_fox.shape[1]),
                  pl.BlockSpec((tm, d), lambda i: (i, 0)), pl.BlockSpec((tm, d), lambda i: (i, 1)),
                  full(wd), full(wf)],
        out_specs=row(d),
        compiler_params=_params(("parallel",)),
        name="merge_branches",
    )(o_diff, o_fox, gates, gates, wd, wf)


def _layer_norm(y, g, b):
    mu = jnp.mean(y, axis=1, keepdims=True)
    yc = y - mu
    var = jnp.mean(yc * yc, axis=1, keepdims=True)
    return yc * lax.rsqrt(var + LN_EPS) * g + b


def _route(lg):
    lane = lax.broadcasted_iota(I32, lg.shape, 1)
    far = jnp.int32(4 * LANES)
    is_g = lane < N_GROUPS
    gl = jnp.where(is_g, lg, NEG)
    gmax = jnp.max(gl, axis=1, keepdims=True)
    gidx = jnp.min(jnp.where(gl == gmax, lane, far), axis=1, keepdims=True)
    top_gp = 1.0 / jnp.sum(jnp.where(is_g, jnp.exp(gl - gmax), 0.0), axis=1, keepdims=True)
    lo = N_GROUPS + gidx * EXPERTS_PER_GROUP
    el = jnp.where((lane >= lo) & (lane < lo + EXPERTS_PER_GROUP), lg, NEG)
    m1 = jnp.max(el, axis=1, keepdims=True)
    i1 = jnp.min(jnp.where(el == m1, lane, far), axis=1, keepdims=True)
    el2 = jnp.where(lane == i1, 2.0 * NEG, el)
    m2 = jnp.max(el2, axis=1, keepdims=True)
    i2 = jnp.min(jnp.where(el2 == m2, lane, far), axis=1, keepdims=True)
    d = jnp.exp(m2 - m1)
    w1 = top_gp / (1.0 + d)
    w2 = w1 * d
    ids = jnp.where(lane == 0, i1 - N_GROUPS, jnp.where(lane == 1, i2 - N_GROUPS, 0))
    wts = jnp.where(lane == 0, w1, jnp.where(lane == 1, w2, 0.0))
    return ids, wts


def _outln_kernel(mg_ref, x_ref, wo_ref, g_ref, b_ref, wr_ref, br_ref, x1_ref, ids_ref, wts_ref, *, alpha):
    mix = jnp.dot(mg_ref[...], wo_ref[...], preferred_element_type=F32)
    x1 = _layer_norm(alpha * x_ref[...] + mix, g_ref[...], b_ref[...])
    x1_ref[...] = x1
    xh = x1.astype(BF16)
    xl = (x1 - xh.astype(F32)).astype(BF16)
    wr = wr_ref[...]
    a = jnp.dot(xh, wr, preferred_element_type=F32)
    b = jnp.dot(xl, wr[:, :LANES], preferred_element_type=F32)
    ids, wts = _route(a[:, :LANES] + a[:, LANES:] + b + br_ref[...])
    ids_ref[...] = ids
    wts_ref[...] = wts


def _out_ln_route(merged, x, w_out, g, b, wr2, br, alpha):
    s, d = x.shape
    tm = min(OUTLN_TM, s)
    row = lambda c: pl.BlockSpec((tm, c), lambda i: (i, 0))
    full = lambda a: pl.BlockSpec(a.shape, lambda i: (0, 0))
    return pl.pallas_call(
        functools.partial(_outln_kernel, alpha=alpha),
        out_shape=(jax.ShapeDtypeStruct((s, d), F32), jax.ShapeDtypeStruct((s, LANES), I32),
                   jax.ShapeDtypeStruct((s, LANES), F32)),
        grid=(s // tm,),
        in_specs=[row(d), row(d), full(w_out), full(g), full(b), full(wr2), full(br)],
        out_specs=(row(d), row(LANES), row(LANES)),
        compiler_params=_params(("parallel",)),
        name="out_proj_ln_route",
    )(merged, x, w_out, g, b, wr2, br)


def _rank_kernel(ids_ref, dest_ref, bexp_ref, meta_ref, cnt_ref, carry_ref, start_ref, *, block, n_blocks):
    ph, i = pl.program_id(0), pl.program_id(1)
    ids = ids_ref[...]
    tb = ids.shape[0]
    lane = lax.broadcasted_iota(I32, (tb, LANES), 1)
    oh1 = (lane == ids[:, 0:1]).astype(F32)
    oh2 = (lane == ids[:, 1:2]).astype(F32)
    oh = oh1 + oh2
    colsum = jnp.sum(oh, axis=0, keepdims=True)

    @pl.when((ph == 0) & (i == 0))
    def _():
        cnt_ref[...] = jnp.zeros_like(cnt_ref)

    @pl.when(ph == 0)
    def _():
        cnt_ref[...] += jnp.broadcast_to(colsum, cnt_ref.shape)

    @pl.when((ph == 1) & (i == 0))
    def _():
        cnt = cnt_ref[0:1, :]
        nblk = jnp.floor((cnt + (block - 1)) * (1.0 / block))
        r = lax.broadcasted_iota(I32, (LANES, LANES), 0)
        c = lax.broadcasted_iota(I32, (LANES, LANES), 1)
        upper = (r < c).astype(BF16)
        nb8 = jnp.broadcast_to(nblk, (8, LANES)).astype(BF16)
        bstart = jnp.dot(nb8, upper, preferred_element_type=F32)
        start_ref[...] = bstart * block
        carry_ref[...] = jnp.zeros_like(carry_ref)
        lane1 = lax.broadcasted_iota(I32, (1, LANES), 1)
        bend = jnp.where(lane1 < N_EXPERTS, bstart[0:1] + nblk, 4.0 * n_blocks)
        bidx = lax.broadcasted_iota(I32, (n_blocks, LANES), 0).astype(F32)
        be = jnp.sum((jnp.broadcast_to(bend, (n_blocks, LANES)) <= bidx).astype(F32), axis=1, keepdims=True)
        be = jnp.minimum(be, N_EXPERTS - 1.0)
        lane_b = lax.broadcasted_iota(I32, (n_blocks, LANES), 1).astype(F32)
        owns = jnp.broadcast_to((nblk > 0.0) & (lane1 < N_EXPERTS), (n_blocks, LANES))
        nxt = jnp.min(jnp.where(owns & (lane_b > be), lane_b, 1.0 * LANES), axis=1, keepdims=True)
        nxt = jnp.where(nxt >= 1.0 * LANES, -1.0, nxt)
        before = jnp.sum((owns & (lane_b < be)).astype(F32), axis=1, keepdims=True)
        parity = before - 2.0 * jnp.floor(before * 0.5)
        bexp_ref[...] = jnp.where(lane_b == 0.0, be, jnp.where(lane_b == 1.0, nxt, parity)).astype(I32)
        total = jnp.sum(jnp.where(lane1 < N_EXPERTS, nblk, 0.0), axis=1, keepdims=True)
        row = lax.broadcasted_iota(I32, (8, LANES), 0)
        first_pad = start_ref[0:1, :] + cnt
        meta = jnp.where(row == 0, jnp.broadcast_to(total, (8, LANES)),
                         jnp.where(row == 1, jnp.broadcast_to(first_pad, (8, LANES)),
                                   jnp.broadcast_to(bstart[0:1] * block + nblk * block, (8, LANES))))
        meta_ref[...] = meta.astype(I32)

    @pl.when(ph == 1)
    def _():
        rr = lax.broadcasted_iota(I32, (tb, tb), 0)
        cc = lax.broadcasted_iota(I32, (tb, tb), 1)
        lower = (cc < rr).astype(BF16)
        prefix = jnp.dot(lower, oh.astype(BF16), preferred_element_type=F32)
        pos = prefix + carry_ref[0:1, :] + start_ref[0:1, :]
        d1 = jnp.sum(pos * oh1, axis=1, keepdims=True)
        d2 = jnp.sum(pos * oh2, axis=1, keepdims=True)
        dest_ref[...] = jnp.where(lane == 0, d1, jnp.where(lane == 1, d2, 0.0)).astype(I32)
        carry_ref[...] += jnp.broadcast_to(colsum, carry_ref.shape)


def _rank(ids, block, n_blocks):
    t = ids.shape[0]
    tb = min(RANK_TB, t)
    return pl.pallas_call(
        functools.partial(_rank_kernel, block=block, n_blocks=n_blocks),
        out_shape=(jax.ShapeDtypeStruct((t, LANES), I32), jax.ShapeDtypeStruct((n_blocks, LANES), I32),
                   jax.ShapeDtypeStruct((8, LANES), I32)),
        grid=(2, t // tb),
        in_specs=[pl.BlockSpec((tb, LANES), lambda p, i: (i, 0))],
        out_specs=(pl.BlockSpec((tb, LANES), lambda p, i: (i * p, 0)),
                   pl.BlockSpec((n_blocks, LANES), lambda p, i: (0, 0)),
                   pl.BlockSpec((8, LANES), lambda p, i: (0, 0))),
        scratch_shapes=[pltpu.VMEM((8, LANES), F32)] * 3,
        compiler_params=_params(("arbitrary", "arbitrary")),
        name="moe_rank",
    )(ids)


def _invert_kernel(dest_ref, zeros_hbm, inv_hbm, inv_ref, sem):
    i = pl.program_id(0)
    n = dest_ref.shape[0]

    @pl.when(i == 0)
    def _():
        fill = pltpu.make_async_copy(zeros_hbm, inv_ref, sem)
        fill.start()
        fill.wait()

    def body(j, c):
        inv_ref[dest_ref[j]] = i * n + j
        return c

    lax.fori_loop(0, n, body, 0, unroll=8)

    @pl.when(i == pl.num_programs(0) - 1)
    def _():
        flush = pltpu.make_async_copy(inv_ref, inv_hbm, sem)
        flush.start()
        flush.wait()


def _invert(dest_flat, rows):
    a = dest_flat.shape[0]
    chunk = min(INVERT_CHUNK, a)
    return pl.pallas_call(
        _invert_kernel,
        out_shape=jax.ShapeDtypeStruct((rows,), I32),
        grid=(a // chunk,),
        in_specs=[pl.BlockSpec((chunk,), lambda i: (i,), memory_space=pltpu.SMEM),
                  pl.BlockSpec(memory_space=pl.ANY)],
        out_specs=pl.BlockSpec(memory_space=pl.ANY),
        scratch_shapes=[pltpu.SMEM((rows,), I32), pltpu.SemaphoreType.DMA],
        compiler_params=_params(("arbitrary",)),
        name="moe_invert",
    )(dest_flat, jnp.zeros((rows,), I32))


def _expert_kernel(tab_ref, nact_ref, inv0_ref, inv1_ref, inv2_ref, x1_ref, wg_hbm, wu_hbm, wd_hbm, ys_ref,
                   xbuf, wg_f, wu_f, wd_f, wg_b, wu_b, wd_b, gsem, wsem):
    b = pl.program_id(0)
    n_active = nact_ref[0]
    active = b < n_active
    block = xbuf.shape[1]
    expert, next_expert, wslot = tab_ref[0, b], tab_ref[1, b], tab_ref[2, b]
    first_of_expert = (b == 0) | (tab_ref[0, jnp.maximum(b - 1, 0)] != expert)

    def row_copy(inv_ref, blk, r):
        token = lax.shift_right_logical(inv_ref[0, 0, r], 1)
        slot = lax.rem(blk, GATHER_DEPTH)
        return pltpu.make_async_copy(x1_ref.at[pl.ds(token, 1)], xbuf.at[slot, pl.ds(r, 1)], gsem.at[slot])

    def gather(inv_ref, blk):
        for r in range(block):
            row_copy(inv_ref, blk, r).start()

    def weight_copies(e, s):
        return (pltpu.make_async_copy(wg_hbm.at[e], wg_f.at[s], wsem.at[s, 0]),
                pltpu.make_async_copy(wu_hbm.at[e], wu_f.at[s], wsem.at[s, 1]),
                pltpu.make_async_copy(wd_hbm.at[e], wd_f.at[s], wsem.at[s, 2]))

    @pl.when(b == 0)
    def _():
        for c in weight_copies(expert, wslot):
            c.start()
        gather(inv0_ref, b)

    @pl.when((b == 0) & (n_active > 1))
    def _():
        gather(inv1_ref, b + 1)

    @pl.when(b + 2 < n_active)
    def _():
        gather(inv2_ref, b + 2)

    @pl.when(active & first_of_expert)
    def _():
        for c in weight_copies(expert, wslot):
            c.wait()

        @pl.when(next_expert >= 0)
        def _():
            for c in weight_copies(next_expert, 1 - wslot):
                c.start()

        wg_b[...] = wg_f[wslot].astype(BF16)
        wu_b[...] = wu_f[wslot].astype(BF16)
        wd_b[...] = wd_f[wslot].astype(BF16)

    @pl.when(active)
    def _():
        for r in range(block):
            row_copy(inv0_ref, b, r).wait()
        x = xbuf[lax.rem(b, GATHER_DEPTH)].astype(BF16)
        g = jnp.dot(x, wg_b[...], preferred_element_type=F32)
        u = jnp.dot(x, wu_b[...], preferred_element_type=F32)
        h = (g / (1.0 + jnp.exp(-g)) * u).astype(BF16)
        ys_ref[...] = jnp.dot(h, wd_b[...], preferred_element_type=F32)

    @pl.when(jnp.logical_not(active))
    def _():
        ys_ref[...] = jnp.zeros_like(ys_ref)


def _experts(x1, inv, tab, nact, w_gate, w_up, w_down, block):
    n_blocks = tab.shape[1]
    d, ff = w_gate.shape[1], w_gate.shape[2]
    inv3 = inv.reshape(n_blocks, 1, block)
    ahead = lambda k: pl.BlockSpec((1, 1, block), lambda b, tb, na: (jnp.minimum(b + k, n_blocks - 1), 0, 0),
                                   memory_space=pltpu.SMEM)
    hbm = pl.BlockSpec(memory_space=pl.ANY)
    return pl.pallas_call(
        _expert_kernel,
        out_shape=jax.ShapeDtypeStruct((n_blocks * block, d), F32),
        grid_spec=pltpu.PrefetchScalarGridSpec(
            num_scalar_prefetch=2, grid=(n_blocks,),
            in_specs=[ahead(0), ahead(1), ahead(2), hbm, hbm, hbm, hbm],
            out_specs=pl.BlockSpec((block, d), lambda b, tb, na: (b, 0)),
            scratch_shapes=[pltpu.VMEM((GATHER_DEPTH, block, d), F32),
                            pltpu.VMEM((2, d, ff), F32), pltpu.VMEM((2, d, ff), F32), pltpu.VMEM((2, ff, d), F32),
                            pltpu.VMEM((d, ff), BF16), pltpu.VMEM((d, ff), BF16), pltpu.VMEM((ff, d), BF16),
                            pltpu.SemaphoreType.DMA((GATHER_DEPTH,)), pltpu.SemaphoreType.DMA((2, 3))]),
        compiler_params=_params(("arbitrary",)),
        name="moe_experts",
    )(tab, nact, inv3, inv3, inv3, x1, w_gate, w_up, w_down)


def _combine_kernel(dest_ref, next_ref, x1_ref, wts_ref, g_ref, b_ref, ys_ref, o_ref, buf, sem, *, alpha):
    i = pl.program_id(0)
    tm = x1_ref.shape[0]
    slot = lax.rem(i, 2)

    def row_copy(idx_ref, s, r, k):
        return pltpu.make_async_copy(ys_ref.at[pl.ds(idx_ref[0, 0, TOP_K * r + k], 1)],
                                     buf.at[s, k, pl.ds(r, 1)], sem.at[s])

    def gather(idx_ref, s):
        def start(r, c):
            for k in range(TOP_K):
                row_copy(idx_ref, s, r, k).start()
            return c

        lax.fori_loop(0, tm, start, 0, unroll=8)

    @pl.when(i == 0)
    def _():
        gather(dest_ref, slot)

    @pl.when(i + 1 < pl.num_programs(0))
    def _():
        gather(next_ref, 1 - slot)

    for r in range(tm):
        for k in range(TOP_K):
            row_copy(dest_ref, slot, r, k).wait()
    w = wts_ref[...]
    ffn = w[:, 0:1] * buf[slot, 0] + w[:, 1:2] * buf[slot, 1]
    o_ref[...] = _layer_norm(alpha * x1_ref[...] + ffn, g_ref[...], b_ref[...])


def _combine(ys, dest3, x1, wts, g, b, alpha):
    t, d = x1.shape
    tm = dest3.shape[2] // TOP_K
    row = lambda c: pl.BlockSpec((tm, c), lambda i: (i, 0))
    full = lambda a: pl.BlockSpec(a.shape, lambda i: (0, 0))
    return pl.pallas_call(
        functools.partial(_combine_kernel, alpha=alpha),
        out_shape=jax.ShapeDtypeStruct((t, d), F32),
        grid=(t // tm,),
        in_specs=[pl.BlockSpec((1, 1, TOP_K * tm), lambda i: (i, 0, 0), memory_space=pltpu.SMEM),
                  pl.BlockSpec((1, 1, TOP_K * tm), lambda i: (jnp.minimum(i + 1, t // tm - 1), 0, 0),
                               memory_space=pltpu.SMEM),
                  row(d), row(LANES), full(g), full(b), pl.BlockSpec(memory_space=pl.ANY)],
        out_specs=row(d),
        scratch_shapes=[pltpu.VMEM((2, TOP_K, tm, d), F32), pltpu.SemaphoreType.DMA((2,))],
        compiler_params=_params(("arbitrary",)),
        name="moe_combine_ln",
    )(dest3, dest3, x1, wts, g, b, ys)


def _rope_tables(s):
    inv_freq = ROPE_THETA ** (-jnp.arange(0, HEAD_DIM, 2, dtype=F32) / HEAD_DIM)
    ang = jnp.arange(s, dtype=F32)[:, None] * inv_freq[None, :]
    cos, sin = jnp.cos(ang), jnp.sin(ang)
    return jnp.concatenate([cos, cos], axis=1), jnp.concatenate([-sin, sin], axis=1)


def _split_bf16_pair(w, width):
    hi = w.astype(BF16)
    lo = (w - hi.astype(F32)).astype(BF16)
    pad = lambda a: jnp.pad(a, ((0, 0), (0, width - a.shape[1])))
    return jnp.concatenate([pad(hi), pad(lo)], axis=1)


def _layer(x2, layer, depth, w_in, b_forget, lam_q1, lam_k1, lam_q2, lam_k2, diff_norm_g, w_proj_diff,
           w_proj_fox, w_out, ln1_g, ln1_b, w_rg, b_rg, w_re, b_re, w_gate, w_up, w_down, ln2_g, ln2_b):
    s, d = x2.shape
    alpha = (2 * depth) ** 0.25
    lam_init = 0.8 - 0.6 * math.exp(-0.3 * layer)
    n_diff, n_fox = diff_norm_g.shape[0], b_forget.shape[0]
    qk_cols = n_diff * 2 * HEAD_DIM
    lin_cols = n_diff * DIFF_V_DIM + 3 * n_fox * HEAD_DIM
    q_scale = HEAD_DIM ** -0.5 * LOG2E

    f0 = 2 * qk_cols + lin_cols
    w_t = w_in.T
    fcum, x_bf = _forget_cumsum(x2, w_t[f0:f0 + n_fox].T, b_forget)

    cos_t, sin_t = _rope_tables(s)
    tm = min(PROJ_TM, s)
    tab = pl.BlockSpec((tm, HEAD_DIM), lambda j, i: (i, 0))
    assert qk_cols == PROJ_TN, "column block 0 of the rotary call must be exactly the queries"
    qk = _proj(x_bf, w_t, 0, 2 * qk_cols, "rope", (cos_t, sin_t), (tab, tab), q_scale)
    col_scale = jnp.ones((1, lin_cols), F32).at[:, n_diff * DIFF_V_DIM:n_diff * DIFF_V_DIM + n_fox * HEAD_DIM].set(q_scale)
    lin = _proj(x_bf, w_t, 2 * qk_cols, lin_cols, "scale", (col_scale,),
                (pl.BlockSpec((1, min(PROJ_TN, lin_cols)), lambda j, i: (0, j)),))
    gates = _proj(x_bf, w_t, f0 + n_fox, w_in.shape[1] - f0 - n_fox, "sigmoid")

    lam_rows = jnp.stack([lam_q1, lam_k1, lam_q2, lam_k2]).astype(F32)
    o_diff = _diff_attention(qk, lin, lam_rows, diff_norm_g.astype(F32), n_diff, lam_init)
    o_fox = _fox_attention(lin, fcum, n_fox, n_diff * DIFF_V_DIM // HEAD_DIM)

    merged = _merge(o_diff, o_fox, gates, w_proj_diff.astype(BF16), w_proj_fox.astype(BF16))
    w_router = jnp.concatenate([w_rg, jnp.moveaxis(w_re, 0, 1).reshape(d, N_EXPERTS)], axis=1)
    b_router = jnp.pad(jnp.concatenate([b_rg, b_re.reshape(N_EXPERTS)]).astype(F32),
                       (0, LANES - N_GROUPS - N_EXPERTS)).reshape(1, LANES)
    x1, ids, wts = _out_ln_route(merged, x2, w_out.astype(BF16), ln1_g.reshape(1, d), ln1_b.reshape(1, d),
                                 _split_bf16_pair(w_router, LANES), b_router, alpha)

    block = MOE_BLOCK
    n_blocks = -(-(s * TOP_K) // block) + N_EXPERTS
    dest, bexp, meta = _rank(ids, block, n_blocks)
    tmr = min(ROW_TM, s)
    dest2 = dest[:, :TOP_K]
    inv = _invert(dest2.reshape(s * TOP_K), n_blocks * block)
    ys = _experts(x1, inv, bexp[:, :3].T, meta[0, :1], w_gate, w_up, w_down, block)
    dest3 = dest2.reshape(s // tmr, 1, TOP_K * tmr)
    return _combine(ys, dest3, x1, wts, ln2_g.reshape(1, d), ln2_b.reshape(1, d), alpha)


def kernel(x, w_in, b_forget, lam_q1, lam_k1, lam_q2, lam_k2, diff_norm_g, w_proj_diff, w_proj_fox, w_out,
           ln1_g, ln1_b, w_router_group, b_router_group, w_router_expert, b_router_expert, w_gate, w_up,
           w_down, ln2_g, ln2_b):
    batch, s, d = x.shape
    depth = w_in.shape[0]
    params = (w_in, b_forget, lam_q1, lam_k1, lam_q2, lam_k2, diff_norm_g, w_proj_diff, w_proj_fox, w_out,
              ln1_g, ln1_b, w_router_group, b_router_group, w_router_expert, b_router_expert, w_gate, w_up,
              w_down, ln2_g, ln2_b)
    outs = []
    for bi in range(batch):
        h = x[bi]
        for layer in range(depth):
            h = _layer(h, layer, depth, *(p[layer] for p in params))
        outs.append(h)
    return jnp.stack(outs)
```

```python
import functools
import math

import jax
import jax.numpy as jnp
from jax import lax
from jax.experimental import pallas as pl
from jax.experimental.pallas import tpu as pltpu

F32, BF16, I32 = jnp.float32, jnp.bfloat16, jnp.int32

HEAD_DIM = 128
DIFF_V_DIM = 2 * HEAD_DIM
ROPE_THETA = 10000.0
N_GROUPS = 4
EXPERTS_PER_GROUP = 8
N_EXPERTS = N_GROUPS * EXPERTS_PER_GROUP
TOP_K = 2
LN_EPS = 1e-5
LOG2E = 1.4426950408889634
NEG = -1e30
LANES = 128
SUBLANES = 8
VMEM_LIMIT = 56 * 1024 * 1024

MOE_BLOCK = 128
PROJ_TM, PROJ_TN = 1024, 1024
PROJ_CAST_ROWS = 256
ATT_TQ = 512
ONES_ROWS = 16
PIPE_UNROLL = 4
FORGET_TS = 512
MERGE_TM = 512
OUTLN_TM = 256
RANK_TB = 512
ROW_TM = 256
INVERT_CHUNK = 1024
GATHER_DEPTH = 3


def _params(sem, vmem=VMEM_LIMIT):
    return pltpu.CompilerParams(dimension_semantics=sem, vmem_limit_bytes=vmem)


def _proj_kernel(x_ref, wt_hbm, *rest, epilogue, row0, q_scale):
    *extra, o_ref, w_f32, w_bf, sem = rest
    j, i = pl.program_id(0), pl.program_id(1)
    tn = w_bf.shape[0]
    slot = lax.rem(j, 2)

    def fetch(jj, s):
        rows = pl.ds(pl.multiple_of(row0 + jj * tn, SUBLANES), tn)
        return pltpu.make_async_copy(wt_hbm.at[rows, :], w_f32.at[s], sem.at[s])

    @pl.when(i == 0)
    def _():
        @pl.when(j == 0)
        def _():
            fetch(j, slot).start()

        fetch(j, slot).wait()

        @pl.when(j + 1 < pl.num_programs(0))
        def _():
            fetch(j + 1, 1 - slot).start()

        rows = min(PROJ_CAST_ROWS, tn)

        def chunk(c, carry):
            r = pl.ds(pl.multiple_of(c * rows, rows), rows)
            w_bf[r, :] = w_f32[slot, r, :].astype(BF16)
            return carry

        lax.fori_loop(0, tn // rows, chunk, 0)

    acc = lax.dot_general(x_ref[...], w_bf[...], (((1,), (1,)), ((), ())), preferred_element_type=F32)
    if epilogue == "rope":
        cos_ref, sin_ref = extra
        scale = jnp.where(j == 0, q_scale, 1.0)
        cosf, sinf = cos_ref[...] * scale, sin_ref[...] * scale
        for h in range(tn // HEAD_DIM):
            t = acc[:, h * HEAD_DIM:(h + 1) * HEAD_DIM]
            o_ref[:, h * HEAD_DIM:(h + 1) * HEAD_DIM] = (
                t * cosf + pltpu.roll(t, HEAD_DIM // 2, 1) * sinf).astype(o_ref.dtype)
    elif epilogue == "scale":
        o_ref[...] = (acc * extra[0][...]).astype(o_ref.dtype)
    else:
        o_ref[...] = (1.0 / (1.0 + jnp.exp(-acc))).astype(o_ref.dtype)


def _proj(x_bf, w_t, col0, n, epilogue, extra=(), extra_specs=(), q_scale=1.0):
    m, k = x_bf.shape
    tm, tn = min(PROJ_TM, m), min(PROJ_TN, n)
    assert col0 % SUBLANES == 0 and n % tn == 0
    return pl.pallas_call(
        functools.partial(_proj_kernel, epilogue=epilogue, row0=col0, q_scale=q_scale),
        out_shape=jax.ShapeDtypeStruct((m, n), BF16),
        grid=(n // tn, m // tm),
        in_specs=[pl.BlockSpec((tm, k), lambda j, i: (i, 0)), pl.BlockSpec(memory_space=pl.ANY), *extra_specs],
        out_specs=pl.BlockSpec((tm, tn), lambda j, i: (i, j)),
        scratch_shapes=[pltpu.VMEM((2, tn, k), F32), pltpu.VMEM((tn, k), BF16), pltpu.SemaphoreType.DMA((2,))],
        compiler_params=_params(("arbitrary", "arbitrary")),
        name=f"in_proj_{epilogue}",
    )(x_bf, w_t, *extra)


def _split3(v):
    h = v.astype(BF16)
    r = v - h.astype(F32)
    m = r.astype(BF16)
    return h, m, (r - m.astype(F32)).astype(BF16)


def _forget_kernel(x_ref, w_ref, b_ref, o_ref, xbf_ref, carry_ref):
    i = pl.program_id(0)

    @pl.when(i == 0)
    def _():
        carry_ref[...] = jnp.zeros_like(carry_ref)

    x = x_ref[...]
    xh = x.astype(BF16)
    xbf_ref[...] = xh
    xl = (x - xh.astype(F32)).astype(BF16)
    w = w_ref[...]
    a = jnp.dot(xh, w, preferred_element_type=F32)
    b = jnp.dot(xl, w[:, :LANES], preferred_element_type=F32)
    z = a[:, :LANES] + a[:, LANES:] + b + b_ref[...]
    logf = jnp.minimum(z, 0.0) - jnp.log1p(jnp.exp(-jnp.abs(z)))
    ts = logf.shape[0]
    tri = (lax.broadcasted_iota(I32, (ts, ts), 1) <= lax.broadcasted_iota(I32, (ts, ts), 0)).astype(BF16)
    c = carry_ref[0:1, :]
    for piece in _split3(logf):
        c = c + jnp.dot(tri, piece, preferred_element_type=F32)
    o_ref[...] = c * LOG2E
    carry_ref[...] = jnp.broadcast_to(c[ts - 1:ts, :], carry_ref.shape)


def _forget_cumsum(x, w_f, b_f):
    s, d = x.shape
    nh = w_f.shape[1]
    wh = w_f.astype(BF16)
    wl = (w_f - wh.astype(F32)).astype(BF16)
    pad = lambda a: jnp.pad(a, ((0, 0), (0, LANES - nh)))
    w2 = jnp.concatenate([pad(wh), pad(wl)], axis=1)
    b2 = jnp.pad(b_f.astype(F32), (0, LANES - nh)).reshape(1, LANES)
    ts = min(FORGET_TS, s)
    return pl.pallas_call(
        _forget_kernel,
        out_shape=(jax.ShapeDtypeStruct((s, LANES), F32), jax.ShapeDtypeStruct((s, d), BF16)),
        grid=(s // ts,),
        in_specs=[pl.BlockSpec((ts, d), lambda i: (i, 0)),
                  pl.BlockSpec((d, 2 * LANES), lambda i: (0, 0)),
                  pl.BlockSpec((1, LANES), lambda i: (0, 0))],
        out_specs=(pl.BlockSpec((ts, LANES), lambda i: (i, 0)), pl.BlockSpec((ts, d), lambda i: (i, 0))),
        scratch_shapes=[pltpu.VMEM((8, LANES), F32)],
        compiler_params=_params(("arbitrary",)),
        name="forget_cumsum",
    )(x, w2, b2)


def _pipeline3(n, stage_a, stage_b, stage_c, carry):
    def step(i, slot, state, do_a=True):
        aux, post, carry = state
        stage_c(i - 1, 1 - slot, post)
        aux_next = stage_a(i + 1, 1 - slot) if do_a else aux
        carry, post = stage_b(i, slot, aux, carry)
        return aux_next, post, carry

    aux = stage_a(0, 0)
    if n == 1:
        carry, post = stage_b(0, 0, aux, carry)
        stage_c(0, 0, post)
        return carry
    aux_next = stage_a(1, 1)
    carry, post = stage_b(0, 0, aux, carry)
    state = (aux_next, post, carry)
    mid = n - 2

    def unrolled(h, s):
        for u in range(PIPE_UNROLL):
            s = step(PIPE_UNROLL * h + 1 + u, (1 + u) % 2, s)
        return s

    state = lax.fori_loop(0, mid // PIPE_UNROLL, unrolled, state)
    for i in range(mid - mid % PIPE_UNROLL + 1, n - 1):
        state = step(i, i % 2, state)
    _, post, carry = step(n - 1, (n - 1) % 2, state, do_a=False)
    stage_c(n - 1, (n - 1) % 2, post)
    return carry


def _attn_kernel(qi_tab, kb_tab, first_tab, q_ref, k_ref, v_ref, *rest, t, has_bias):
    if has_bias:
        c_ref, o_ref, kaug, qt, vt, st0, st1, p0, p1, acc_all, m_all = rest
    else:
        o_ref, qt, vt, st0, st1, p0, p1, acc_all, m_all = rest
        kaug = k_ref
    st, p = (st0, st1), (p0, p1)
    seq, dv = v_ref.shape
    nq = seq // t
    dq = q_ref.shape[1]
    blk = lambda b: pl.ds(pl.multiple_of(b * t, t), t)
    head = pl.program_id(0)

    def prep(c, carry):
        rows = blk(c)
        lane = lax.broadcasted_iota(I32, (t, LANES), 1)
        if has_bias:
            neg_c = -jnp.sum(jnp.where(lane == head, c_ref[rows, :], 0.0), axis=1, keepdims=True)
            hi, mid, lo = (x.astype(F32) for x in _split3(neg_c))
            aug = jnp.where(lane == 0, hi, jnp.where(lane == 1, mid, jnp.where(lane == 2, lo, 0.0)))
            kaug[rows, :dq] = k_ref[rows, :]
            kaug[rows, dq:] = aug.astype(BF16)
            ones = (lax.broadcasted_iota(I32, (LANES, t), 0) < 3).astype(F32)
            qt[:, rows] = jnp.concatenate([q_ref[rows, :].astype(F32).T, ones], axis=0).astype(BF16)
        else:
            qt[:, rows] = q_ref[rows, :].astype(F32).T.astype(BF16)
        one_row = (lax.broadcasted_iota(I32, (ONES_ROWS, t), 0) == 0).astype(F32)
        vt[:, rows] = jnp.concatenate([v_ref[rows, :].astype(F32).T, one_row], axis=0).astype(BF16)
        return carry

    lax.fori_loop(0, nq, prep, 0)

    def scores(qi, kb):
        return jnp.dot(kaug[blk(kb), :], qt[:, blk(qi)], preferred_element_type=F32)

    def values(kb, slot):
        return jnp.dot(vt[:, blk(kb)], p[slot][...], preferred_element_type=F32)

    def save_max(qi, m):
        m_all[qi] = jnp.broadcast_to(m, m_all.shape[1:])

    def diag_scores(i, slot):
        mask = lax.broadcasted_iota(I32, (t, t), 1) >= lax.broadcasted_iota(I32, (t, t), 0)
        s = jnp.where(mask, scores(i, i), NEG)
        st[slot][...] = s
        return jnp.max(s, axis=0, keepdims=True)

    def diag_softmax(i, slot, cmax, carry):
        p[slot][...] = jnp.exp2(st[slot][...] - cmax).astype(BF16)
        save_max(i, cmax)
        return carry, None

    def diag_values(i, slot, _):
        acc_all[i] = values(i, slot)

    _pipeline3(nq, diag_scores, diag_softmax, diag_values, 0)

    n_low = nq * (nq - 1) // 2
    if n_low:
        last = n_low - 1
        tab = lambda ref, f: ref[jnp.minimum(f, last)]

        def low_scores(f, slot):
            s = scores(tab(qi_tab, f), tab(kb_tab, f))
            st[slot][...] = s
            return jnp.max(s, axis=0, keepdims=True)

        def low_softmax(f, slot, cmax, m):
            qi = tab(qi_tab, f)
            m_prev = jnp.where(tab(first_tab, f) == 1, m_all[qi][0:1], m)
            m_new = jnp.maximum(m_prev, cmax)
            p[slot][...] = jnp.exp2(st[slot][...] - m_new).astype(BF16)
            return m_new, jnp.exp2(m_prev - m_new)

        def low_values(f, slot, a):
            qi = tab(qi_tab, f)
            acc_all[qi] = a * acc_all[qi] + values(tab(kb_tab, f), slot)

        _pipeline3(n_low, low_scores, low_softmax, low_values, jnp.full((1, t), NEG, F32))

    def finish(qi, carry):
        acc = acc_all[qi]
        o_ref[blk(qi), :] = (acc[:dv] / acc[dv:dv + 1]).T.astype(o_ref.dtype)
        return carry

    lax.fori_loop(0, nq, finish, 0)


def _attention(q_src, k_src, v_src, q_col, k_col, v_col, n_heads, dv, fcum=None):
    s = q_src.shape[0]
    t = min(ATT_TQ, s)
    nq = s // t
    pairs = [(qi, kb) for qi in range(nq) for kb in range(qi)] or [(0, 0)]
    qi_tab = jnp.array([a for a, _ in pairs], I32)
    kb_tab = jnp.array([b for _, b in pairs], I32)
    first_tab = jnp.array([int(b == 0) for _, b in pairs], I32)
    has_bias = fcum is not None
    dk = 2 * HEAD_DIM if has_bias else HEAD_DIM
    in_specs = [pl.BlockSpec((s, HEAD_DIM), lambda h, *_: (0, q_col + h)),
                pl.BlockSpec((s, HEAD_DIM), lambda h, *_: (0, k_col + h)),
                pl.BlockSpec((s, dv), lambda h, *_: (0, v_col(h)))]
    args = [q_src, k_src, v_src]
    scratch = []
    if has_bias:
        in_specs.append(pl.BlockSpec((s, LANES), lambda h, *_: (0, 0)))
        args.append(fcum)
        scratch.append(pltpu.VMEM((s, dk), BF16))
    scratch += [pltpu.VMEM((dk, s), BF16), pltpu.VMEM((dv + ONES_ROWS, s), BF16),
                pltpu.VMEM((t, t), F32), pltpu.VMEM((t, t), F32), pltpu.VMEM((t, t), BF16), pltpu.VMEM((t, t), BF16),
                pltpu.VMEM((nq, dv + ONES_ROWS, t), F32), pltpu.VMEM((nq, SUBLANES, t), F32)]
    return pl.pallas_call(
        functools.partial(_attn_kernel, t=t, has_bias=has_bias),
        out_shape=jax.ShapeDtypeStruct((s, n_heads * dv), BF16),
        grid_spec=pltpu.PrefetchScalarGridSpec(
            num_scalar_prefetch=3, grid=(n_heads,), in_specs=in_specs,
            out_specs=pl.BlockSpec((s, dv), lambda h, *_: (0, h)), scratch_shapes=scratch),
        compiler_params=_params(("arbitrary",)),
        name="fox_attention" if has_bias else "diff_attention",
    )(qi_tab, kb_tab, first_tab, *args)


def _diff_merge_kernel(o_ref, lam_ref, g_ref, out_ref, *, lam_init):
    lam_v = lam_ref[...]
    lam = (jnp.exp(jnp.sum(lam_v[0:1] * lam_v[1:2], axis=1, keepdims=True))
           - jnp.exp(jnp.sum(lam_v[2:3] * lam_v[3:4], axis=1, keepdims=True)) + lam_init)
    for h in range(out_ref.shape[1] // DIFF_V_DIM):
        o1 = o_ref[:, (2 * h) * DIFF_V_DIM:(2 * h + 1) * DIFF_V_DIM].astype(F32)
        o2 = o_ref[:, (2 * h + 1) * DIFF_V_DIM:(2 * h + 2) * DIFF_V_DIM].astype(F32)
        o = o1 - lam * o2
        o = o * lax.rsqrt(jnp.mean(o * o, axis=1, keepdims=True) + LN_EPS) * g_ref[h:h + 1, :] * (1.0 - lam_init)
        out_ref[:, h * DIFF_V_DIM:(h + 1) * DIFF_V_DIM] = o.astype(out_ref.dtype)


def _diff_merge(o_maps, lam_rows, norm_g, lam_init):
    s, w = o_maps.shape
    tm = min(MERGE_TM, s)
    return pl.pallas_call(
        functools.partial(_diff_merge_kernel, lam_init=lam_init),
        out_shape=jax.ShapeDtypeStruct((s, w // 2), BF16),
        grid=(s // tm,),
        in_specs=[pl.BlockSpec((tm, w), lambda i: (i, 0)), pl.BlockSpec(lam_rows.shape, lambda i: (0, 0)),
                  pl.BlockSpec(norm_g.shape, lambda i: (0, 0))],
        out_specs=pl.BlockSpec((tm, w // 2), lambda i: (i, 0)),
        compiler_params=_params(("parallel",)),
        name="diff_merge",
    )(o_maps, lam_rows, norm_g)


def _merge_kernel(od_ref, of_ref, gd_ref, gf_ref, wd_ref, wf_ref, o_ref):
    ud = jnp.dot(od_ref[...], wd_ref[...], preferred_element_type=F32)
    uf = jnp.dot(of_ref[...], wf_ref[...], preferred_element_type=F32)
    o_ref[...] = (gd_ref[...].astype(F32) * ud + gf_ref[...].astype(F32) * uf).astype(o_ref.dtype)


def _merge(o_diff, o_fox, gates, wd, wf):
    s, d = o_diff.shape[0], wd.shape[1]
    tm = min(MERGE_TM, s)
    row = lambda c: pl.BlockSpec((tm, c), lambda i: (i, 0))
    full = lambda a: pl.BlockSpec(a.shape, lambda i: (0, 0))
    return pl.pallas_call(
        _merge_kernel,
        out_shape=jax.ShapeDtypeStruct((s, d), BF16),
        grid=(s // tm,),
        in_specs=[row(o_diff.shape[1]), row(o_fox.shape[1]),
                  pl.BlockSpec((tm, d), lambda i: (i, 0)), pl.BlockSpec((tm, d), lambda i: (i, 1)),
                  full(wd), full(wf)],
        out_specs=row(d),
        compiler_params=_params(("parallel",)),
        name="merge_branches",
    )(o_diff, o_fox, gates, gates, wd, wf)


def _layer_norm(y, g, b):
    mu = jnp.mean(y, axis=1, keepdims=True)
    yc = y - mu
    var = jnp.mean(yc * yc, axis=1, keepdims=True)
    return yc * lax.rsqrt(var + LN_EPS) * g + b


def _route(lg):
    lane = lax.broadcasted_iota(I32, lg.shape, 1)
    far = jnp.int32(4 * LANES)
    is_g = lane < N_GROUPS
    gl = jnp.where(is_g, lg, NEG)
    gmax = jnp.max(gl, axis=1, keepdims=True)
    gidx = jnp.min(jnp.where(gl == gmax, lane, far), axis=1, keepdims=True)
    top_gp = 1.0 / jnp.sum(jnp.where(is_g, jnp.exp(gl - gmax), 0.0), axis=1, keepdims=True)
    lo = N_GROUPS + gidx * EXPERTS_PER_GROUP
    el = jnp.where((lane >= lo) & (lane < lo + EXPERTS_PER_GROUP), lg, NEG)
    m1 = jnp.max(el, axis=1, keepdims=True)
    i1 = jnp.min(jnp.where(el == m1, lane, far), axis=1, keepdims=True)
    el2 = jnp.where(lane == i1, 2.0 * NEG, el)
    m2 = jnp.max(el2, axis=1, keepdims=True)
    i2 = jnp.min(jnp.where(el2 == m2, lane, far), axis=1, keepdims=True)
    d = jnp.exp(m2 - m1)
    w1 = top_gp / (1.0 + d)
    w2 = w1 * d
    ids = jnp.where(lane == 0, i1 - N_GROUPS, jnp.where(lane == 1, i2 - N_GROUPS, 0))
    wts = jnp.where(lane == 0, w1, jnp.where(lane == 1, w2, 0.0))
    return ids, wts


def _outln_kernel(mg_ref, x_ref, wo_ref, g_ref, b_ref, wr_ref, br_ref, x1_ref, ids_ref, wts_ref, *, alpha):
    mix = jnp.dot(mg_ref[...], wo_ref[...], preferred_element_type=F32)
    x1 = _layer_norm(alpha * x_ref[...] + mix, g_ref[...], b_ref[...])
    x1_ref[...] = x1
    xh = x1.astype(BF16)
    xl = (x1 - xh.astype(F32)).astype(BF16)
    wr = wr_ref[...]
    a = jnp.dot(xh, wr, preferred_element_type=F32)
    b = jnp.dot(xl, wr[:, :LANES], preferred_element_type=F32)
    ids, wts = _route(a[:, :LANES] + a[:, LANES:] + b + br_ref[...])
    ids_ref[...] = ids
    wts_ref[...] = wts


def _out_ln_route(merged, x, w_out, g, b, wr2, br, alpha):
    s, d = x.shape
    tm = min(OUTLN_TM, s)
    row = lambda c: pl.BlockSpec((tm, c), lambda i: (i, 0))
    full = lambda a: pl.BlockSpec(a.shape, lambda i: (0, 0))
    return pl.pallas_call(
        functools.partial(_outln_kernel, alpha=alpha),
        out_shape=(jax.ShapeDtypeStruct((s, d), F32), jax.ShapeDtypeStruct((s, LANES), I32),
                   jax.ShapeDtypeStruct((s, LANES), F32)),
        grid=(s // tm,),
        in_specs=[row(d), row(d), full(w_out), full(g), full(b), full(wr2), full(br)],
        out_specs=(row(d), row(LANES), row(LANES)),
        compiler_params=_params(("parallel",)),
        name="out_proj_ln_route",
    )(merged, x, w_out, g, b, wr2, br)


def _rank_kernel(ids_ref, dest_ref, bexp_ref, meta_ref, cnt_ref, carry_ref, start_ref, *, block, n_blocks):
    ph, i = pl.program_id(0), pl.program_id(1)
    ids = ids_ref[...]
    tb = ids.shape[0]
    lane = lax.broadcasted_iota(I32, (tb, LANES), 1)
    oh1 = (lane == ids[:, 0:1]).astype(F32)
    oh2 = (lane == ids[:, 1:2]).astype(F32)
    oh = oh1 + oh2
    colsum = jnp.sum(oh, axis=0, keepdims=True)

    @pl.when((ph == 0) & (i == 0))
    def _():
        cnt_ref[...] = jnp.zeros_like(cnt_ref)

    @pl.when(ph == 0)
    def _():
        cnt_ref[...] += jnp.broadcast_to(colsum, cnt_ref.shape)

    @pl.when((ph == 1) & (i == 0))
    def _():
        cnt = cnt_ref[0:1, :]
        nblk = jnp.floor((cnt + (block - 1)) * (1.0 / block))
        r = lax.broadcasted_iota(I32, (LANES, LANES), 0)
        c = lax.broadcasted_iota(I32, (LANES, LANES), 1)
        upper = (r < c).astype(BF16)
        nb8 = jnp.broadcast_to(nblk, (8, LANES)).astype(BF16)
        bstart = jnp.dot(nb8, upper, preferred_element_type=F32)
        start_ref[...] = bstart * block
        carry_ref[...] = jnp.zeros_like(carry_ref)
        lane1 = lax.broadcasted_iota(I32, (1, LANES), 1)
        bend = jnp.where(lane1 < N_EXPERTS, bstart[0:1] + nblk, 4.0 * n_blocks)
        bidx = lax.broadcasted_iota(I32, (n_blocks, LANES), 0).astype(F32)
        be = jnp.sum((jnp.broadcast_to(bend, (n_blocks, LANES)) <= bidx).astype(F32), axis=1, keepdims=True)
        be = jnp.minimum(be, N_EXPERTS - 1.0)
        lane_b = lax.broadcasted_iota(I32, (n_blocks, LANES), 1).astype(F32)
        owns = jnp.broadcast_to((nblk > 0.0) & (lane1 < N_EXPERTS), (n_blocks, LANES))
        nxt = jnp.min(jnp.where(owns & (lane_b > be), lane_b, 1.0 * LANES), axis=1, keepdims=True)
        nxt = jnp.where(nxt >= 1.0 * LANES, -1.0, nxt)
        before = jnp.sum((owns & (lane_b < be)).astype(F32), axis=1, keepdims=True)
        parity = before - 2.0 * jnp.floor(before * 0.5)
        bexp_ref[...] = jnp.where(lane_b == 0.0, be, jnp.where(lane_b == 1.0, nxt, parity)).astype(I32)
        total = jnp.sum(jnp.where(lane1 < N_EXPERTS, nblk, 0.0), axis=1, keepdims=True)
        row = lax.broadcasted_iota(I32, (8, LANES), 0)
        first_pad = start_ref[0:1, :] + cnt
        meta = jnp.where(row == 0, jnp.broadcast_to(total, (8, LANES)),
                         jnp.where(row == 1, jnp.broadcast_to(first_pad, (8, LANES)),
                                   jnp.broadcast_to(bstart[0:1] * block + nblk * block, (8, LANES))))
        meta_ref[...] = meta.astype(I32)

    @pl.when(ph == 1)
    def _():
        rr = lax.broadcasted_iota(I32, (tb, tb), 0)
        cc = lax.broadcasted_iota(I32, (tb, tb), 1)
        lower = (cc < rr).astype(BF16)
        prefix = jnp.dot(lower, oh.astype(BF16), preferred_element_type=F32)
        pos = prefix + carry_ref[0:1, :] + start_ref[0:1, :]
        d1 = jnp.sum(pos * oh1, axis=1, keepdims=True)
        d2 = jnp.sum(pos * oh2, axis=1, keepdims=True)
        dest_ref[...] = jnp.where(lane == 0, d1, jnp.where(lane == 1, d2, 0.0)).astype(I32)
        carry_ref[...] += jnp.broadcast_to(colsum, carry_ref.shape)


def _rank(ids, block, n_blocks):
    t = ids.shape[0]
    tb = min(RANK_TB, t)
    return pl.pallas_call(
        functools.partial(_rank_kernel, block=block, n_blocks=n_blocks),
        out_shape=(jax.ShapeDtypeStruct((t, LANES), I32), jax.ShapeDtypeStruct((n_blocks, LANES), I32),
                   jax.ShapeDtypeStruct((8, LANES), I32)),
        grid=(2, t // tb),
        in_specs=[pl.BlockSpec((tb, LANES), lambda p, i: (i, 0))],
        out_specs=(pl.BlockSpec((tb, LANES), lambda p, i: (i * p, 0)),
                   pl.BlockSpec((n_blocks, LANES), lambda p, i: (0, 0)),
                   pl.BlockSpec((8, LANES), lambda p, i: (0, 0))),
        scratch_shapes=[pltpu.VMEM((8, LANES), F32)] * 3,
        compiler_params=_params(("arbitrary", "arbitrary")),
        name="moe_rank",
    )(ids)


def _invert_kernel(dest_ref, zeros_hbm, inv_hbm, inv_ref, sem):
    i = pl.program_id(0)
    n = dest_ref.shape[0]

    @pl.when(i == 0)
    def _():
        fill = pltpu.make_async_copy(zeros_hbm, inv_ref, sem)
        fill.start()
        fill.wait()

    def body(j, c):
        inv_ref[dest_ref[j]] = i * n + j
        return c

    lax.fori_loop(0, n, body, 0, unroll=8)

    @pl.when(i == pl.num_programs(0) - 1)
    def _():
        flush = pltpu.make_async_copy(inv_ref, inv_hbm, sem)
        flush.start()
        flush.wait()


def _invert(dest_flat, rows):
    a = dest_flat.shape[0]
    chunk = min(INVERT_CHUNK, a)
    return pl.pallas_call(
        _invert_kernel,
        out_shape=jax.ShapeDtypeStruct((rows,), I32),
        grid=(a // chunk,),
        in_specs=[pl.BlockSpec((chunk,), lambda i: (i,), memory_space=pltpu.SMEM),
                  pl.BlockSpec(memory_space=pl.ANY)],
        out_specs=pl.BlockSpec(memory_space=pl.ANY),
        scratch_shapes=[pltpu.SMEM((rows,), I32), pltpu.SemaphoreType.DMA],
        compiler_params=_params(("arbitrary",)),
        name="moe_invert",
    )(dest_flat, jnp.zeros((rows,), I32))


def _expert_kernel(tab_ref, nact_ref, inv0_ref, inv1_ref, inv2_ref, x1_ref, wg_hbm, wu_hbm, wd_hbm, ys_ref,
                   xbuf, wg_f, wu_f, wd_f, wg_b, wu_b, wd_b, gsem, wsem):
    b = pl.program_id(0)
    n_active = nact_ref[0]
    active = b < n_active
    block = xbuf.shape[1]
    expert, next_expert, wslot = tab_ref[0, b], tab_ref[1, b], tab_ref[2, b]
    first_of_expert = (b == 0) | (tab_ref[0, jnp.maximum(b - 1, 0)] != expert)

    def row_copy(inv_ref, blk, r):
        token = lax.shift_right_logical(inv_ref[0, 0, r], 1)
        slot = lax.rem(blk, GATHER_DEPTH)
        return pltpu.make_async_copy(x1_ref.at[pl.ds(token, 1)], xbuf.at[slot, pl.ds(r, 1)], gsem.at[slot])

    def gather(inv_ref, blk):
        for r in range(block):
            row_copy(inv_ref, blk, r).start()

    def weight_copies(e, s):
        return (pltpu.make_async_copy(wg_hbm.at[e], wg_f.at[s], wsem.at[s, 0]),
                pltpu.make_async_copy(wu_hbm.at[e], wu_f.at[s], wsem.at[s, 1]),
                pltpu.make_async_copy(wd_hbm.at[e], wd_f.at[s], wsem.at[s, 2]))

    @pl.when(b == 0)
    def _():
        for c in weight_copies(expert, wslot):
            c.start()
        gather(inv0_ref, b)

    @pl.when((b == 0) & (n_active > 1))
    def _():
        gather(inv1_ref, b + 1)

    @pl.when(b + 2 < n_active)
    def _():
        gather(inv2_ref, b + 2)

    @pl.when(active & first_of_expert)
    def _():
        for c in weight_copies(expert, wslot):
            c.wait()

        @pl.when(next_expert >= 0)
        def _():
            for c in weight_copies(next_expert, 1 - wslot):
                c.start()

        wg_b[...] = wg_f[wslot].astype(BF16)
        wu_b[...] = wu_f[wslot].astype(BF16)
        wd_b[...] = wd_f[wslot].astype(BF16)

    @pl.when(active)
    def _():
        for r in range(block):
            row_copy(inv0_ref, b, r).wait()
        x = xbuf[lax.rem(b, GATHER_DEPTH)].astype(BF16)
        g = jnp.dot(x, wg_b[...], preferred_element_type=F32)
        u = jnp.dot(x, wu_b[...], preferred_element_type=F32)
        h = (g / (1.0 + jnp.exp(-g)) * u).astype(BF16)
        ys_ref[...] = jnp.dot(h, wd_b[...], preferred_element_type=F32)

    @pl.when(jnp.logical_not(active))
    def _():
        ys_ref[...] = jnp.zeros_like(ys_ref)


def _experts(x1, inv, tab, nact, w_gate, w_up, w_down, block):
    n_blocks = tab.shape[1]
    d, ff = w_gate.shape[1], w_gate.shape[2]
    inv3 = inv.reshape(n_blocks, 1, block)
    ahead = lambda k: pl.BlockSpec((1, 1, block), lambda b, tb, na: (jnp.minimum(b + k, n_blocks - 1), 0, 0),
                                   memory_space=pltpu.SMEM)
    hbm = pl.BlockSpec(memory_space=pl.ANY)
    return pl.pallas_call(
        _expert_kernel,
        out_shape=jax.ShapeDtypeStruct((n_blocks * block, d), F32),
        grid_spec=pltpu.PrefetchScalarGridSpec(
            num_scalar_prefetch=2, grid=(n_blocks,),
            in_specs=[ahead(0), ahead(1), ahead(2), hbm, hbm, hbm, hbm],
            out_specs=pl.BlockSpec((block, d), lambda b, tb, na: (b, 0)),
            scratch_shapes=[pltpu.VMEM((GATHER_DEPTH, block, d), F32),
                            pltpu.VMEM((2, d, ff), F32), pltpu.VMEM((2, d, ff), F32), pltpu.VMEM((2, ff, d), F32),
                            pltpu.VMEM((d, ff), BF16), pltpu.VMEM((d, ff), BF16), pltpu.VMEM((ff, d), BF16),
                            pltpu.SemaphoreType.DMA((GATHER_DEPTH,)), pltpu.SemaphoreType.DMA((2, 3))]),
        compiler_params=_params(("arbitrary",)),
        name="moe_experts",
    )(tab, nact, inv3, inv3, inv3, x1, w_gate, w_up, w_down)


def _combine_kernel(dest_ref, next_ref, x1_ref, wts_ref, g_ref, b_ref, ys_ref, o_ref, buf, sem, *, alpha):
    i = pl.program_id(0)
    tm = x1_ref.shape[0]
    slot = lax.rem(i, 2)

    def row_copy(idx_ref, s, r, k):
        return pltpu.make_async_copy(ys_ref.at[pl.ds(idx_ref[0, 0, TOP_K * r + k], 1)],
                                     buf.at[s, k, pl.ds(r, 1)], sem.at[s])

    def gather(idx_ref, s):
        def start(r, c):
            for k in range(TOP_K):
                row_copy(idx_ref, s, r, k).start()
            return c

        lax.fori_loop(0, tm, start, 0, unroll=8)

    @pl.when(i == 0)
    def _():
        gather(dest_ref, slot)

    @pl.when(i + 1 < pl.num_programs(0))
    def _():
        gather(next_ref, 1 - slot)

    for r in range(tm):
        for k in range(TOP_K):
            row_copy(dest_ref, slot, r, k).wait()
    w = wts_ref[...]
    ffn = w[:, 0:1] * buf[slot, 0] + w[:, 1:2] * buf[slot, 1]
    o_ref[...] = _layer_norm(alpha * x1_ref[...] + ffn, g_ref[...], b_ref[...])


def _combine(ys, dest3, x1, wts, g, b, alpha):
    t, d = x1.shape
    tm = dest3.shape[2] // TOP_K
    row = lambda c: pl.BlockSpec((tm, c), lambda i: (i, 0))
    full = lambda a: pl.BlockSpec(a.shape, lambda i: (0, 0))
    return pl.pallas_call(
        functools.partial(_combine_kernel, alpha=alpha),
        out_shape=jax.ShapeDtypeStruct((t, d), F32),
        grid=(t // tm,),
        in_specs=[pl.BlockSpec((1, 1, TOP_K * tm), lambda i: (i, 0, 0), memory_space=pltpu.SMEM),
                  pl.BlockSpec((1, 1, TOP_K * tm), lambda i: (jnp.minimum(i + 1, t // tm - 1), 0, 0),
                               memory_space=pltpu.SMEM),
                  row(d), row(LANES), full(g), full(b), pl.BlockSpec(memory_space=pl.ANY)],
        out_specs=row(d),
        scratch_shapes=[pltpu.VMEM((2, TOP_K, tm, d), F32), pltpu.SemaphoreType.DMA((2,))],
        compiler_params=_params(("arbitrary",)),
        name="moe_combine_ln",
    )(dest3, dest3, x1, wts, g, b, ys)


def _rope_tables(s):
    inv_freq = ROPE_THETA ** (-jnp.arange(0, HEAD_DIM, 2, dtype=F32) / HEAD_DIM)
    ang = jnp.arange(s, dtype=F32)[:, None] * inv_freq[None, :]
    cos, sin = jnp.cos(ang), jnp.sin(ang)
    return jnp.concatenate([cos, cos], axis=1), jnp.concatenate([-sin, sin], axis=1)


def _split_bf16_pair(w, width):
    hi = w.astype(BF16)
    lo = (w - hi.astype(F32)).astype(BF16)
    pad = lambda a: jnp.pad(a, ((0, 0), (0, width - a.shape[1])))
    return jnp.concatenate([pad(hi), pad(lo)], axis=1)


def _layer(x2, layer, depth, w_in, b_forget, lam_q1, lam_k1, lam_q2, lam_k2, diff_norm_g, w_proj_diff,
           w_proj_fox, w_out, ln1_g, ln1_b, w_rg, b_rg, w_re, b_re, w_gate, w_up, w_down, ln2_g, ln2_b):
    s, d = x2.shape
    alpha = (2 * depth) ** 0.25
    lam_init = 0.8 - 0.6 * math.exp(-0.3 * layer)
    n_diff, n_fox = diff_norm_g.shape[0], b_forget.shape[0]
    qk_cols = n_diff * 2 * HEAD_DIM
    lin_cols = n_diff * DIFF_V_DIM + 3 * n_fox * HEAD_DIM
    q_scale = HEAD_DIM ** -0.5 * LOG2E

    f0 = 2 * qk_cols + lin_cols
    w_t = w_in.T
    fcum, x_bf = _forget_cumsum(x2, w_t[f0:f0 + n_fox].T, b_forget)

    cos_t, sin_t = _rope_tables(s)
    tm = min(PROJ_TM, s)
    tab = pl.BlockSpec((tm, HEAD_DIM), lambda j, i: (i, 0))
    assert qk_cols == PROJ_TN, "column block 0 of the rotary call must be exactly the queries"
    qk = _proj(x_bf, w_t, 0, 2 * qk_cols, "rope", (cos_t, sin_t), (tab, tab), q_scale)
    col_scale = jnp.ones((1, lin_cols), F32).at[:, n_diff * DIFF_V_DIM:n_diff * DIFF_V_DIM + n_fox * HEAD_DIM].set(q_scale)
    lin = _proj(x_bf, w_t, 2 * qk_cols, lin_cols, "scale", (col_scale,),
                (pl.BlockSpec((1, min(PROJ_TN, lin_cols)), lambda j, i: (0, j)),))
    gates = _proj(x_bf, w_t, f0 + n_fox, w_in.shape[1] - f0 - n_fox, "sigmoid")

    lam_rows = jnp.stack([lam_q1, lam_k1, lam_q2, lam_k2]).astype(F32)
    n_maps = 2 * n_diff
    o_maps = _attention(qk, qk, lin, 0, n_maps, lambda h: h // 2, n_maps, DIFF_V_DIM)
    o_diff = _diff_merge(o_maps, lam_rows, diff_norm_g.astype(F32), lam_init)
    fq_col = n_diff * DIFF_V_DIM // HEAD_DIM
    o_fox = _attention(lin, lin, lin, fq_col, fq_col + n_fox, lambda h: fq_col + 2 * n_fox + h, n_fox, HEAD_DIM, fcum)

    merged = _merge(o_diff, o_fox, gates, w_proj_diff.astype(BF16), w_proj_fox.astype(BF16))
    w_router = jnp.concatenate([w_rg, jnp.moveaxis(w_re, 0, 1).reshape(d, N_EXPERTS)], axis=1)
    b_router = jnp.pad(jnp.concatenate([b_rg, b_re.reshape(N_EXPERTS)]).astype(F32),
                       (0, LANES - N_GROUPS - N_EXPERTS)).reshape(1, LANES)
    x1, ids, wts = _out_ln_route(merged, x2, w_out.astype(BF16), ln1_g.reshape(1, d), ln1_b.reshape(1, d),
                                 _split_bf16_pair(w_router, LANES), b_router, alpha)

    block = MOE_BLOCK
    n_blocks = -(-(s * TOP_K) // block) + N_EXPERTS
    dest, bexp, meta = _rank(ids, block, n_blocks)
    tmr = min(ROW_TM, s)
    dest2 = dest[:, :TOP_K]
    inv = _invert(dest2.reshape(s * TOP_K), n_blocks * block)
    ys = _experts(x1, inv, bexp[:, :3].T, meta[0, :1], w_gate, w_up, w_down, block)
    dest3 = dest2.reshape(s // tmr, 1, TOP_K * tmr)
    return _combine(ys, dest3, x1, wts, ln2_g.reshape(1, d), ln2_b.reshape(1, d), alpha)


def kernel(x, w_in, b_forget, lam_q1, lam_k1, lam_q2, lam_k2, diff_norm_g, w_proj_diff, w_proj_fox, w_out,
           ln1_g, ln1_b, w_router_group, b_router_group, w_router_expert, b_router_expert, w_gate, w_up,
           w_down, ln2_g, ln2_b):
    batch, s, d = x.shape
    depth = w_in.shape[0]
    params = (w_in, b_forget, lam_q1, lam_k1, lam_q2, lam_k2, diff_norm_g, w_proj_diff, w_proj_fox, w_out,
              ln1_g, ln1_b, w_router_group, b_router_group, w_router_expert, b_router_expert, w_gate, w_up,
              w_down, ln2_g, ln2_b)
    outs = []
    for bi in range(batch):
        h = x[bi]
        for layer in range(depth):
            h = _layer(h, layer, depth, *(p[layer] for p in params))
        outs.append(h)
    return jnp.stack(outs)
```

```python
import functools
import math

import jax
import jax.numpy as jnp
from jax import lax
from jax.experimental import pallas as pl
from jax.experimental.pallas import tpu as pltpu

F32, BF16, I32 = jnp.float32, jnp.bfloat16, jnp.int32

HEAD_DIM = 128
DIFF_V_DIM = 2 * HEAD_DIM
ROPE_THETA = 10000.0
N_GROUPS = 4
EXPERTS_PER_GROUP = 8
N_EXPERTS = N_GROUPS * EXPERTS_PER_GROUP
TOP_K = 2
LN_EPS = 1e-5
LOG2E = 1.4426950408889634
NEG = -1e30
LANES = 128
SUBLANES = 8
VMEM_LIMIT = 56 * 1024 * 1024

MOE_BLOCK = 128
PROJ_TM, PROJ_TN = 1024, 1024
PROJ_CAST_ROWS = 256
ATT_TQ = 512
ONES_ROWS = 16
PIPE_UNROLL = 4
FORGET_TS = 512
MERGE_TM = 512
OUTLN_TM = 512
RANK_TB = 512
ROW_TM = 256
INVERT_CHUNK = 1024
GATHER_DEPTH = 3


def _params(sem, vmem=VMEM_LIMIT):
    return pltpu.CompilerParams(dimension_semantics=sem, vmem_limit_bytes=vmem)


def _proj_kernel(x_ref, wt_hbm, *rest, epilogue, row0, q_scale):
    *extra, o_ref, w_f32, w_bf, sem = rest
    j, i = pl.program_id(0), pl.program_id(1)
    tn = w_bf.shape[0]
    slot = lax.rem(j, 2)

    def fetch(jj, s):
        rows = pl.ds(pl.multiple_of(row0 + jj * tn, SUBLANES), tn)
        return pltpu.make_async_copy(wt_hbm.at[rows, :], w_f32.at[s], sem.at[s])

    @pl.when(i == 0)
    def _():
        @pl.when(j == 0)
        def _():
            fetch(j, slot).start()

        fetch(j, slot).wait()

        @pl.when(j + 1 < pl.num_programs(0))
        def _():
            fetch(j + 1, 1 - slot).start()

        rows = min(PROJ_CAST_ROWS, tn)

        def chunk(c, carry):
            r = pl.ds(pl.multiple_of(c * rows, rows), rows)
            w_bf[r, :] = w_f32[slot, r, :].astype(BF16)
            return carry

        lax.fori_loop(0, tn // rows, chunk, 0)

    acc = lax.dot_general(x_ref[...], w_bf[...], (((1,), (1,)), ((), ())), preferred_element_type=F32)
    if epilogue == "rope":
        cos_ref, sin_ref = extra
        scale = jnp.where(j == 0, q_scale, 1.0)
        cosf, sinf = cos_ref[...] * scale, sin_ref[...] * scale
        for h in range(tn // HEAD_DIM):
            t = acc[:, h * HEAD_DIM:(h + 1) * HEAD_DIM]
            o_ref[:, h * HEAD_DIM:(h + 1) * HEAD_DIM] = (
                t * cosf + pltpu.roll(t, HEAD_DIM // 2, 1) * sinf).astype(o_ref.dtype)
    elif epilogue == "scale":
        o_ref[...] = (acc * extra[0][...]).astype(o_ref.dtype)
    else:
        o_ref[...] = (1.0 / (1.0 + jnp.exp(-acc))).astype(o_ref.dtype)


def _proj(x_bf, w_t, col0, n, epilogue, extra=(), extra_specs=(), q_scale=1.0):
    m, k = x_bf.shape
    tm, tn = min(PROJ_TM, m), min(PROJ_TN, n)
    assert col0 % SUBLANES == 0 and n % tn == 0
    return pl.pallas_call(
        functools.partial(_proj_kernel, epilogue=epilogue, row0=col0, q_scale=q_scale),
        out_shape=jax.ShapeDtypeStruct((m, n), BF16),
        grid=(n // tn, m // tm),
        in_specs=[pl.BlockSpec((tm, k), lambda j, i: (i, 0)), pl.BlockSpec(memory_space=pl.ANY), *extra_specs],
        out_specs=pl.BlockSpec((tm, tn), lambda j, i: (i, j)),
        scratch_shapes=[pltpu.VMEM((2, tn, k), F32), pltpu.VMEM((tn, k), BF16), pltpu.SemaphoreType.DMA((2,))],
        compiler_params=_params(("arbitrary", "arbitrary")),
        name=f"in_proj_{epilogue}",
    )(x_bf, w_t, *extra)


def _split3(v):
    h = v.astype(BF16)
    r = v - h.astype(F32)
    m = r.astype(BF16)
    return h, m, (r - m.astype(F32)).astype(BF16)


def _forget_kernel(x_ref, w_ref, b_ref, o_ref, xbf_ref, carry_ref):
    i = pl.program_id(0)

    @pl.when(i == 0)
    def _():
        carry_ref[...] = jnp.zeros_like(carry_ref)

    x = x_ref[...]
    xh = x.astype(BF16)
    xbf_ref[...] = xh
    xl = (x - xh.astype(F32)).astype(BF16)
    w = w_ref[...]
    a = jnp.dot(xh, w, preferred_element_type=F32)
    b = jnp.dot(xl, w[:, :LANES], preferred_element_type=F32)
    z = a[:, :LANES] + a[:, LANES:] + b + b_ref[...]
    logf = jnp.minimum(z, 0.0) - jnp.log1p(jnp.exp(-jnp.abs(z)))
    ts = logf.shape[0]
    tri = (lax.broadcasted_iota(I32, (ts, ts), 1) <= lax.broadcasted_iota(I32, (ts, ts), 0)).astype(BF16)
    c = carry_ref[0:1, :]
    for piece in _split3(logf):
        c = c + jnp.dot(tri, piece, preferred_element_type=F32)
    o_ref[...] = c * LOG2E
    carry_ref[...] = jnp.broadcast_to(c[ts - 1:ts, :], carry_ref.shape)


def _forget_cumsum(x, w_f, b_f):
    s, d = x.shape
    nh = w_f.shape[1]
    wh = w_f.astype(BF16)
    wl = (w_f - wh.astype(F32)).astype(BF16)
    pad = lambda a: jnp.pad(a, ((0, 0), (0, LANES - nh)))
    w2 = jnp.concatenate([pad(wh), pad(wl)], axis=1)
    b2 = jnp.pad(b_f.astype(F32), (0, LANES - nh)).reshape(1, LANES)
    ts = min(FORGET_TS, s)
    return pl.pallas_call(
        _forget_kernel,
        out_shape=(jax.ShapeDtypeStruct((s, LANES), F32), jax.ShapeDtypeStruct((s, d), BF16)),
        grid=(s // ts,),
        in_specs=[pl.BlockSpec((ts, d), lambda i: (i, 0)),
                  pl.BlockSpec((d, 2 * LANES), lambda i: (0, 0)),
                  pl.BlockSpec((1, LANES), lambda i: (0, 0))],
        out_specs=(pl.BlockSpec((ts, LANES), lambda i: (i, 0)), pl.BlockSpec((ts, d), lambda i: (i, 0))),
        scratch_shapes=[pltpu.VMEM((8, LANES), F32)],
        compiler_params=_params(("arbitrary",)),
        name="forget_cumsum",
    )(x, w2, b2)


def _pipeline3(n, stage_a, stage_b, stage_c, carry):
    def step(i, slot, state, do_a=True):
        aux, post, carry = state
        stage_c(i - 1, 1 - slot, post)
        aux_next = stage_a(i + 1, 1 - slot) if do_a else aux
        carry, post = stage_b(i, slot, aux, carry)
        return aux_next, post, carry

    aux = stage_a(0, 0)
    if n == 1:
        carry, post = stage_b(0, 0, aux, carry)
        stage_c(0, 0, post)
        return carry
    aux_next = stage_a(1, 1)
    carry, post = stage_b(0, 0, aux, carry)
    state = (aux_next, post, carry)
    mid = n - 2

    def unrolled(h, s):
        for u in range(PIPE_UNROLL):
            s = step(PIPE_UNROLL * h + 1 + u, (1 + u) % 2, s)
        return s

    state = lax.fori_loop(0, mid // PIPE_UNROLL, unrolled, state)
    for i in range(mid - mid % PIPE_UNROLL + 1, n - 1):
        state = step(i, i % 2, state)
    _, post, carry = step(n - 1, (n - 1) % 2, state, do_a=False)
    stage_c(n - 1, (n - 1) % 2, post)
    return carry


def _attn_kernel(qi_tab, kb_tab, first_tab, q_ref, k_ref, v_ref, *rest, t, has_bias):
    if has_bias:
        c_ref, o_ref, kaug, qt, vt, st0, st1, p0, p1, acc_all, m_all = rest
    else:
        o_ref, qt, vt, st0, st1, p0, p1, acc_all, m_all = rest
        kaug = k_ref
    st, p = (st0, st1), (p0, p1)
    seq, dv = v_ref.shape
    nq = seq // t
    dq = q_ref.shape[1]
    blk = lambda b: pl.ds(pl.multiple_of(b * t, t), t)
    head = pl.program_id(0)

    def prep(c, carry):
        rows = blk(c)
        lane = lax.broadcasted_iota(I32, (t, LANES), 1)
        if has_bias:
            neg_c = -jnp.sum(jnp.where(lane == head, c_ref[rows, :], 0.0), axis=1, keepdims=True)
            hi, mid, lo = (x.astype(F32) for x in _split3(neg_c))
            aug = jnp.where(lane == 0, hi, jnp.where(lane == 1, mid, jnp.where(lane == 2, lo, 0.0)))
            kaug[rows, :dq] = k_ref[rows, :]
            kaug[rows, dq:] = aug.astype(BF16)
            ones = (lax.broadcasted_iota(I32, (LANES, t), 0) < 3).astype(F32)
            qt[:, rows] = jnp.concatenate([q_ref[rows, :].astype(F32).T, ones], axis=0).astype(BF16)
        else:
            qt[:, rows] = q_ref[rows, :].astype(F32).T.astype(BF16)
        one_row = (lax.broadcasted_iota(I32, (ONES_ROWS, t), 0) == 0).astype(F32)
        vt[:, rows] = jnp.concatenate([v_ref[rows, :].astype(F32).T, one_row], axis=0).astype(BF16)
        return carry

    lax.fori_loop(0, nq, prep, 0)

    def scores(qi, kb):
        return jnp.dot(kaug[blk(kb), :], qt[:, blk(qi)], preferred_element_type=F32)

    def values(kb, slot):
        return jnp.dot(vt[:, blk(kb)], p[slot][...], preferred_element_type=F32)

    def save_max(qi, m):
        m_all[qi] = jnp.broadcast_to(m, m_all.shape[1:])

    def diag_scores(i, slot):
        mask = lax.broadcasted_iota(I32, (t, t), 1) >= lax.broadcasted_iota(I32, (t, t), 0)
        s = jnp.where(mask, scores(i, i), NEG)
        st[slot][...] = s
        return jnp.max(s, axis=0, keepdims=True)

    def diag_softmax(i, slot, cmax, carry):
        p[slot][...] = jnp.exp2(st[slot][...] - cmax).astype(BF16)
        save_max(i, cmax)
        return carry, None

    def diag_values(i, slot, _):
        acc_all[i] = values(i, slot)

    _pipeline3(nq, diag_scores, diag_softmax, diag_values, 0)

    n_low = nq * (nq - 1) // 2
    if n_low:
        last = n_low - 1
        tab = lambda ref, f: ref[jnp.minimum(f, last)]

        def low_scores(f, slot):
            s = scores(tab(qi_tab, f), tab(kb_tab, f))
            st[slot][...] = s
            return jnp.max(s, axis=0, keepdims=True)

        def low_softmax(f, slot, cmax, m):
            qi = tab(qi_tab, f)
            m_prev = jnp.where(tab(first_tab, f) == 1, m_all[qi][0:1], m)
            m_new = jnp.maximum(m_prev, cmax)
            p[slot][...] = jnp.exp2(st[slot][...] - m_new).astype(BF16)
            return m_new, jnp.exp2(m_prev - m_new)

        def low_values(f, slot, a):
            qi = tab(qi_tab, f)
            acc_all[qi] = a * acc_all[qi] + values(tab(kb_tab, f), slot)

        _pipeline3(n_low, low_scores, low_softmax, low_values, jnp.full((1, t), NEG, F32))

    def finish(qi, carry):
        acc = acc_all[qi]
        o_ref[blk(qi), :] = (acc[:dv] / acc[dv:dv + 1]).T.astype(o_ref.dtype)
        return carry

    lax.fori_loop(0, nq, finish, 0)


def _attention(q_src, k_src, v_src, q_col, k_col, v_col, n_heads, dv, fcum=None):
    s = q_src.shape[0]
    t = min(ATT_TQ, s)
    nq = s // t
    pairs = [(qi, kb) for qi in range(nq) for kb in range(qi)] or [(0, 0)]
    qi_tab = jnp.array([a for a, _ in pairs], I32)
    kb_tab = jnp.array([b for _, b in pairs], I32)
    first_tab = jnp.array([int(b == 0) for _, b in pairs], I32)
    has_bias = fcum is not None
    dk = 2 * HEAD_DIM if has_bias else HEAD_DIM
    in_specs = [pl.BlockSpec((s, HEAD_DIM), lambda h, *_: (0, q_col + h)),
                pl.BlockSpec((s, HEAD_DIM), lambda h, *_: (0, k_col + h)),
                pl.BlockSpec((s, dv), lambda h, *_: (0, v_col(h)))]
    args = [q_src, k_src, v_src]
    scratch = []
    if has_bias:
        in_specs.append(pl.BlockSpec((s, LANES), lambda h, *_: (0, 0)))
        args.append(fcum)
        scratch.append(pltpu.VMEM((s, dk), BF16))
    scratch += [pltpu.VMEM((dk, s), BF16), pltpu.VMEM((dv + ONES_ROWS, s), BF16),
                pltpu.VMEM((t, t), F32), pltpu.VMEM((t, t), F32), pltpu.VMEM((t, t), BF16), pltpu.VMEM((t, t), BF16),
                pltpu.VMEM((nq, dv + ONES_ROWS, t), F32), pltpu.VMEM((nq, SUBLANES, t), F32)]
    return pl.pallas_call(
        functools.partial(_attn_kernel, t=t, has_bias=has_bias),
        out_shape=jax.ShapeDtypeStruct((s, n_heads * dv), BF16),
        grid_spec=pltpu.PrefetchScalarGridSpec(
            num_scalar_prefetch=3, grid=(n_heads,), in_specs=in_specs,
            out_specs=pl.BlockSpec((s, dv), lambda h, *_: (0, h)), scratch_shapes=scratch),
        compiler_params=_params(("arbitrary",)),
        name="fox_attention" if has_bias else "diff_attention",
    )(qi_tab, kb_tab, first_tab, *args)


def _merge_kernel(om_ref, of_ref, gd_ref, gf_ref, lam_ref, g_ref, wd_ref, wf_ref, o_ref, *, lam_init):
    lam_v = lam_ref[...]
    lam = (jnp.exp(jnp.sum(lam_v[0:1] * lam_v[1:2], axis=1, keepdims=True))
           - jnp.exp(jnp.sum(lam_v[2:3] * lam_v[3:4], axis=1, keepdims=True)) + lam_init)
    heads = []
    for h in range(g_ref.shape[0]):
        o1 = om_ref[:, (2 * h) * DIFF_V_DIM:(2 * h + 1) * DIFF_V_DIM].astype(F32)
        o2 = om_ref[:, (2 * h + 1) * DIFF_V_DIM:(2 * h + 2) * DIFF_V_DIM].astype(F32)
        o = o1 - lam * o2
        o = o * lax.rsqrt(jnp.mean(o * o, axis=1, keepdims=True) + LN_EPS) * g_ref[h:h + 1, :] * (1.0 - lam_init)
        heads.append(o.astype(BF16))
    ud = jnp.dot(jnp.concatenate(heads, axis=1), wd_ref[...], preferred_element_type=F32)
    uf = jnp.dot(of_ref[...], wf_ref[...], preferred_element_type=F32)
    o_ref[...] = (gd_ref[...].astype(F32) * ud + gf_ref[...].astype(F32) * uf).astype(o_ref.dtype)


def _merge(o_maps, o_fox, gates, lam_rows, norm_g, wd, wf, lam_init):
    s, d = o_maps.shape[0], wd.shape[1]
    tm = min(MERGE_TM, s)
    row = lambda c: pl.BlockSpec((tm, c), lambda i: (i, 0))
    full = lambda a: pl.BlockSpec(a.shape, lambda i: (0, 0))
    return pl.pallas_call(
        functools.partial(_merge_kernel, lam_init=lam_init),
        out_shape=jax.ShapeDtypeStruct((s, d), BF16),
        grid=(s // tm,),
        in_specs=[row(o_maps.shape[1]), row(o_fox.shape[1]),
                  pl.BlockSpec((tm, d), lambda i: (i, 0)), pl.BlockSpec((tm, d), lambda i: (i, 1)),
                  full(lam_rows), full(norm_g), full(wd), full(wf)],
        out_specs=row(d),
        compiler_params=_params(("parallel",)),
        name="merge_branches",
    )(o_maps, o_fox, gates, gates, lam_rows, norm_g, wd, wf)


def _layer_norm(y, g, b):
    mu = jnp.mean(y, axis=1, keepdims=True)
    yc = y - mu
    var = jnp.mean(yc * yc, axis=1, keepdims=True)
    return yc * lax.rsqrt(var + LN_EPS) * g + b


def _route(lg):
    lane = lax.broadcasted_iota(I32, lg.shape, 1)
    far = jnp.int32(4 * LANES)
    is_g = lane < N_GROUPS
    gl = jnp.where(is_g, lg, NEG)
    gmax = jnp.max(gl, axis=1, keepdims=True)
    gidx = jnp.min(jnp.where(gl == gmax, lane, far), axis=1, keepdims=True)
    top_gp = 1.0 / jnp.sum(jnp.where(is_g, jnp.exp(gl - gmax), 0.0), axis=1, keepdims=True)
    lo = N_GROUPS + gidx * EXPERTS_PER_GROUP
    el = jnp.where((lane >= lo) & (lane < lo + EXPERTS_PER_GROUP), lg, NEG)
    m1 = jnp.max(el, axis=1, keepdims=True)
    i1 = jnp.min(jnp.where(el == m1, lane, far), axis=1, keepdims=True)
    el2 = jnp.where(lane == i1, 2.0 * NEG, el)
    m2 = jnp.max(el2, axis=1, keepdims=True)
    i2 = jnp.min(jnp.where(el2 == m2, lane, far), axis=1, keepdims=True)
    d = jnp.exp(m2 - m1)
    w1 = top_gp / (1.0 + d)
    w2 = w1 * d
    ids = jnp.where(lane == 0, i1 - N_GROUPS, jnp.where(lane == 1, i2 - N_GROUPS, 0))
    wts = jnp.where(lane == 0, w1, jnp.where(lane == 1, w2, 0.0))
    return ids, wts


def _outln_kernel(mg_ref, x_ref, wo_ref, g_ref, b_ref, wr_ref, br_ref, x1_ref, ids_ref, wts_ref, *, alpha):
    mix = jnp.dot(mg_ref[...], wo_ref[...], preferred_element_type=F32)
    x1 = _layer_norm(alpha * x_ref[...] + mix, g_ref[...], b_ref[...])
    x1_ref[...] = x1
    xh = x1.astype(BF16)
    xl = (x1 - xh.astype(F32)).astype(BF16)
    wr = wr_ref[...]
    a = jnp.dot(xh, wr, preferred_element_type=F32)
    b = jnp.dot(xl, wr[:, :LANES], preferred_element_type=F32)
    ids, wts = _route(a[:, :LANES] + a[:, LANES:] + b + br_ref[...])
    ids_ref[...] = ids
    wts_ref[...] = wts


def _out_ln_route(merged, x, w_out, g, b, wr2, br, alpha):
    s, d = x.shape
    tm = min(OUTLN_TM, s)
    row = lambda c: pl.BlockSpec((tm, c), lambda i: (i, 0))
    full = lambda a: pl.BlockSpec(a.shape, lambda i: (0, 0))
    return pl.pallas_call(
        functools.partial(_outln_kernel, alpha=alpha),
        out_shape=(jax.ShapeDtypeStruct((s, d), F32), jax.ShapeDtypeStruct((s, LANES), I32),
                   jax.ShapeDtypeStruct((s, LANES), F32)),
        grid=(s // tm,),
        in_specs=[row(d), row(d), full(w_out), full(g), full(b), full(wr2), full(br)],
        out_specs=(row(d), row(LANES), row(LANES)),
        compiler_params=_params(("parallel",)),
        name="out_proj_ln_route",
    )(merged, x, w_out, g, b, wr2, br)


def _rank_kernel(ids_ref, dest_ref, bexp_ref, meta_ref, cnt_ref, carry_ref, start_ref, *, block, n_blocks):
    ph, i = pl.program_id(0), pl.program_id(1)
    ids = ids_ref[...]
    tb = ids.shape[0]
    lane = lax.broadcasted_iota(I32, (tb, LANES), 1)
    oh1 = (lane == ids[:, 0:1]).astype(F32)
    oh2 = (lane == ids[:, 1:2]).astype(F32)
    oh = oh1 + oh2
    colsum = jnp.sum(oh, axis=0, keepdims=True)

    @pl.when((ph == 0) & (i == 0))
    def _():
        cnt_ref[...] = jnp.zeros_like(cnt_ref)

    @pl.when(ph == 0)
    def _():
        cnt_ref[...] += jnp.broadcast_to(colsum, cnt_ref.shape)

    @pl.when((ph == 1) & (i == 0))
    def _():
        cnt = cnt_ref[0:1, :]
        nblk = jnp.floor((cnt + (block - 1)) * (1.0 / block))
        r = lax.broadcasted_iota(I32, (LANES, LANES), 0)
        c = lax.broadcasted_iota(I32, (LANES, LANES), 1)
        upper = (r < c).astype(BF16)
        nb8 = jnp.broadcast_to(nblk, (8, LANES)).astype(BF16)
        bstart = jnp.dot(nb8, upper, preferred_element_type=F32)
        start_ref[...] = bstart * block
        carry_ref[...] = jnp.zeros_like(carry_ref)
        lane1 = lax.broadcasted_iota(I32, (1, LANES), 1)
        bend = jnp.where(lane1 < N_EXPERTS, bstart[0:1] + nblk, 4.0 * n_blocks)
        bidx = lax.broadcasted_iota(I32, (n_blocks, LANES), 0).astype(F32)
        be = jnp.sum((jnp.broadcast_to(bend, (n_blocks, LANES)) <= bidx).astype(F32), axis=1, keepdims=True)
        be = jnp.minimum(be, N_EXPERTS - 1.0)
        lane_b = lax.broadcasted_iota(I32, (n_blocks, LANES), 1).astype(F32)
        owns = jnp.broadcast_to((nblk > 0.0) & (lane1 < N_EXPERTS), (n_blocks, LANES))
        nxt = jnp.min(jnp.where(owns & (lane_b > be), lane_b, 1.0 * LANES), axis=1, keepdims=True)
        nxt = jnp.where(nxt >= 1.0 * LANES, -1.0, nxt)
        before = jnp.sum((owns & (lane_b < be)).astype(F32), axis=1, keepdims=True)
        parity = before - 2.0 * jnp.floor(before * 0.5)
        bexp_ref[...] = jnp.where(lane_b == 0.0, be, jnp.where(lane_b == 1.0, nxt, parity)).astype(I32)
        total = jnp.sum(jnp.where(lane1 < N_EXPERTS, nblk, 0.0), axis=1, keepdims=True)
        row = lax.broadcasted_iota(I32, (8, LANES), 0)
        first_pad = start_ref[0:1, :] + cnt
        meta = jnp.where(row == 0, jnp.broadcast_to(total, (8, LANES)),
                         jnp.where(row == 1, jnp.broadcast_to(first_pad, (8, LANES)),
                                   jnp.broadcast_to(bstart[0:1] * block + nblk * block, (8, LANES))))
        meta_ref[...] = meta.astype(I32)

    @pl.when(ph == 1)
    def _():
        rr = lax.broadcasted_iota(I32, (tb, tb), 0)
        cc = lax.broadcasted_iota(I32, (tb, tb), 1)
        lower = (cc < rr).astype(BF16)
        prefix = jnp.dot(lower, oh.astype(BF16), preferred_element_type=F32)
        pos = prefix + carry_ref[0:1, :] + start_ref[0:1, :]
        d1 = jnp.sum(pos * oh1, axis=1, keepdims=True)
        d2 = jnp.sum(pos * oh2, axis=1, keepdims=True)
        dest_ref[...] = jnp.where(lane == 0, d1, jnp.where(lane == 1, d2, 0.0)).astype(I32)
        carry_ref[...] += jnp.broadcast_to(colsum, carry_ref.shape)


def _rank(ids, block, n_blocks):
    t = ids.shape[0]
    tb = min(RANK_TB, t)
    return pl.pallas_call(
        functools.partial(_rank_kernel, block=block, n_blocks=n_blocks),
        out_shape=(jax.ShapeDtypeStruct((t, LANES), I32), jax.ShapeDtypeStruct((n_blocks, LANES), I32),
                   jax.ShapeDtypeStruct((8, LANES), I32)),
        grid=(2, t // tb),
        in_specs=[pl.BlockSpec((tb, LANES), lambda p, i: (i, 0))],
        out_specs=(pl.BlockSpec((tb, LANES), lambda p, i: (i * p, 0)),
                   pl.BlockSpec((n_blocks, LANES), lambda p, i: (0, 0)),
                   pl.BlockSpec((8, LANES), lambda p, i: (0, 0))),
        scratch_shapes=[pltpu.VMEM((8, LANES), F32)] * 3,
        compiler_params=_params(("arbitrary", "arbitrary")),
        name="moe_rank",
    )(ids)


def _invert_kernel(dest_ref, zeros_hbm, inv_hbm, inv_ref, sem):
    i = pl.program_id(0)
    n = dest_ref.shape[0]

    @pl.when(i == 0)
    def _():
        fill = pltpu.make_async_copy(zeros_hbm, inv_ref, sem)
        fill.start()
        fill.wait()

    def body(j, c):
        inv_ref[dest_ref[j]] = i * n + j
        return c

    lax.fori_loop(0, n, body, 0, unroll=8)

    @pl.when(i == pl.num_programs(0) - 1)
    def _():
        flush = pltpu.make_async_copy(inv_ref, inv_hbm, sem)
        flush.start()
        flush.wait()


def _invert(dest_flat, rows):
    a = dest_flat.shape[0]
    chunk = min(INVERT_CHUNK, a)
    return pl.pallas_call(
        _invert_kernel,
        out_shape=jax.ShapeDtypeStruct((rows,), I32),
        grid=(a // chunk,),
        in_specs=[pl.BlockSpec((chunk,), lambda i: (i,), memory_space=pltpu.SMEM),
                  pl.BlockSpec(memory_space=pl.ANY)],
        out_specs=pl.BlockSpec(memory_space=pl.ANY),
        scratch_shapes=[pltpu.SMEM((rows,), I32), pltpu.SemaphoreType.DMA],
        compiler_params=_params(("arbitrary",)),
        name="moe_invert",
    )(dest_flat, jnp.zeros((rows,), I32))


def _expert_kernel(tab_ref, nact_ref, inv0_ref, inv1_ref, inv2_ref, x1_ref, wg_hbm, wu_hbm, wd_hbm, ys_ref,
                   xbuf, wg_f, wu_f, wd_f, wg_b, wu_b, wd_b, gsem, wsem):
    b = pl.program_id(0)
    n_active = nact_ref[0]
    active = b < n_active
    block = xbuf.shape[1]
    expert, next_expert, wslot = tab_ref[0, b], tab_ref[1, b], tab_ref[2, b]
    first_of_expert = (b == 0) | (tab_ref[0, jnp.maximum(b - 1, 0)] != expert)

    def row_copy(inv_ref, blk, r):
        token = lax.shift_right_logical(inv_ref[0, 0, r], 1)
        slot = lax.rem(blk, GATHER_DEPTH)
        return pltpu.make_async_copy(x1_ref.at[pl.ds(token, 1)], xbuf.at[slot, pl.ds(r, 1)], gsem.at[slot])

    def gather(inv_ref, blk):
        for r in range(block):
            row_copy(inv_ref, blk, r).start()

    def weight_copies(e, s):
        return (pltpu.make_async_copy(wg_hbm.at[e], wg_f.at[s], wsem.at[s, 0]),
                pltpu.make_async_copy(wu_hbm.at[e], wu_f.at[s], wsem.at[s, 1]),
                pltpu.make_async_copy(wd_hbm.at[e], wd_f.at[s], wsem.at[s, 2]))

    @pl.when(b == 0)
    def _():
        for c in weight_copies(expert, wslot):
            c.start(priority=1)
        gather(inv0_ref, b)

    @pl.when((b == 0) & (n_active > 1))
    def _():
        gather(inv1_ref, b + 1)

    @pl.when(b + 2 < n_active)
    def _():
        gather(inv2_ref, b + 2)

    @pl.when(active & first_of_expert)
    def _():
        for c in weight_copies(expert, wslot):
            c.wait()

        @pl.when(next_expert >= 0)
        def _():
            for c in weight_copies(next_expert, 1 - wslot):
                c.start(priority=1)

        wg_b[...] = wg_f[wslot].astype(BF16)
        wu_b[...] = wu_f[wslot].astype(BF16)
        wd_b[...] = wd_f[wslot].astype(BF16)

    @pl.when(active)
    def _():
        for r in range(block):
            row_copy(inv0_ref, b, r).wait()
        x = xbuf[lax.rem(b, GATHER_DEPTH)].astype(BF16)
        g = jnp.dot(x, wg_b[...], preferred_element_type=F32)
        u = jnp.dot(x, wu_b[...], preferred_element_type=F32)
        h = (g / (1.0 + jnp.exp(-g)) * u).astype(BF16)
        ys_ref[...] = jnp.dot(h, wd_b[...], preferred_element_type=F32)

    @pl.when(jnp.logical_not(active))
    def _():
        ys_ref[...] = jnp.zeros_like(ys_ref)


def _experts(x1, inv, tab, nact, w_gate, w_up, w_down, block):
    n_blocks = tab.shape[1]
    d, ff = w_gate.shape[1], w_gate.shape[2]
    inv3 = inv.reshape(n_blocks, 1, block)
    ahead = lambda k: pl.BlockSpec((1, 1, block), lambda b, tb, na: (jnp.minimum(b + k, n_blocks - 1), 0, 0),
                                   memory_space=pltpu.SMEM)
    hbm = pl.BlockSpec(memory_space=pl.ANY)
    return pl.pallas_call(
        _expert_kernel,
        out_shape=jax.ShapeDtypeStruct((n_blocks * block, d), F32),
        grid_spec=pltpu.PrefetchScalarGridSpec(
            num_scalar_prefetch=2, grid=(n_blocks,),
            in_specs=[ahead(0), ahead(1), ahead(2), hbm, hbm, hbm, hbm],
            out_specs=pl.BlockSpec((block, d), lambda b, tb, na: (b, 0)),
            scratch_shapes=[pltpu.VMEM((GATHER_DEPTH, block, d), F32),
                            pltpu.VMEM((2, d, ff), F32), pltpu.VMEM((2, d, ff), F32), pltpu.VMEM((2, ff, d), F32),
                            pltpu.VMEM((d, ff), BF16), pltpu.VMEM((d, ff), BF16), pltpu.VMEM((ff, d), BF16),
                            pltpu.SemaphoreType.DMA((GATHER_DEPTH,)), pltpu.SemaphoreType.DMA((2, 3))]),
        compiler_params=_params(("arbitrary",)),
        name="moe_experts",
    )(tab, nact, inv3, inv3, inv3, x1, w_gate, w_up, w_down)


def _combine_kernel(dest_ref, next_ref, x1_ref, wts_ref, g_ref, b_ref, ys_ref, o_ref, buf, sem, *, alpha):
    i = pl.program_id(0)
    tm = x1_ref.shape[0]
    slot = lax.rem(i, 2)

    def row_copy(idx_ref, s, r, k):
        return pltpu.make_async_copy(ys_ref.at[pl.ds(idx_ref[0, 0, TOP_K * r + k], 1)],
                                     buf.at[s, k, pl.ds(r, 1)], sem.at[s])

    def gather(idx_ref, s):
        def start(r, c):
            for k in range(TOP_K):
                row_copy(idx_ref, s, r, k).start(priority=k % 2)
            return c

        lax.fori_loop(0, tm, start, 0, unroll=8)

    @pl.when(i == 0)
    def _():
        gather(dest_ref, slot)

    @pl.when(i + 1 < pl.num_programs(0))
    def _():
        gather(next_ref, 1 - slot)

    for r in range(tm):
        for k in range(TOP_K):
            row_copy(dest_ref, slot, r, k).wait()
    w = wts_ref[...]
    ffn = w[:, 0:1] * buf[slot, 0] + w[:, 1:2] * buf[slot, 1]
    o_ref[...] = _layer_norm(alpha * x1_ref[...] + ffn, g_ref[...], b_ref[...])


def _combine(ys, dest3, x1, wts, g, b, alpha):
    t, d = x1.shape
    tm = dest3.shape[2] // TOP_K
    row = lambda c: pl.BlockSpec((tm, c), lambda i: (i, 0))
    full = lambda a: pl.BlockSpec(a.shape, lambda i: (0, 0))
    return pl.pallas_call(
        functools.partial(_combine_kernel, alpha=alpha),
        out_shape=jax.ShapeDtypeStruct((t, d), F32),
        grid=(t // tm,),
        in_specs=[pl.BlockSpec((1, 1, TOP_K * tm), lambda i: (i, 0, 0), memory_space=pltpu.SMEM),
                  pl.BlockSpec((1, 1, TOP_K * tm), lambda i: (jnp.minimum(i + 1, t // tm - 1), 0, 0),
                               memory_space=pltpu.SMEM),
                  row(d), row(LANES), full(g), full(b), pl.BlockSpec(memory_space=pl.ANY)],
        out_specs=row(d),
        scratch_shapes=[pltpu.VMEM((2, TOP_K, tm, d), F32), pltpu.SemaphoreType.DMA((2,))],
        compiler_params=_params(("arbitrary",)),
        name="moe_combine_ln",
    )(dest3, dest3, x1, wts, g, b, ys)


def _rope_tables(s):
    inv_freq = ROPE_THETA ** (-jnp.arange(0, HEAD_DIM, 2, dtype=F32) / HEAD_DIM)
    ang = jnp.arange(s, dtype=F32)[:, None] * inv_freq[None, :]
    cos, sin = jnp.cos(ang), jnp.sin(ang)
    return jnp.concatenate([cos, cos], axis=1), jnp.concatenate([-sin, sin], axis=1)


def _split_bf16_pair(w, width):
    hi = w.astype(BF16)
    lo = (w - hi.astype(F32)).astype(BF16)
    pad = lambda a: jnp.pad(a, ((0, 0), (0, width - a.shape[1])))
    return jnp.concatenate([pad(hi), pad(lo)], axis=1)


def _layer(x2, layer, depth, w_in, b_forget, lam_q1, lam_k1, lam_q2, lam_k2, diff_norm_g, w_proj_diff,
           w_proj_fox, w_out, ln1_g, ln1_b, w_rg, b_rg, w_re, b_re, w_gate, w_up, w_down, ln2_g, ln2_b):
    s, d = x2.shape
    alpha = (2 * depth) ** 0.25
    lam_init = 0.8 - 0.6 * math.exp(-0.3 * layer)
    n_diff, n_fox = diff_norm_g.shape[0], b_forget.shape[0]
    qk_cols = n_diff * 2 * HEAD_DIM
    lin_cols = n_diff * DIFF_V_DIM + 3 * n_fox * HEAD_DIM
    q_scale = HEAD_DIM ** -0.5 * LOG2E

    f0 = 2 * qk_cols + lin_cols
    w_t = w_in.T
    fcum, x_bf = _forget_cumsum(x2, w_t[f0:f0 + n_fox].T, b_forget)

    cos_t, sin_t = _rope_tables(s)
    tm = min(PROJ_TM, s)
    tab = pl.BlockSpec((tm, HEAD_DIM), lambda j, i: (i, 0))
    assert qk_cols == PROJ_TN, "column block 0 of the rotary call must be exactly the queries"
    qk = _proj(x_bf, w_t, 0, 2 * qk_cols, "rope", (cos_t, sin_t), (tab, tab), q_scale)
    col_scale = jnp.ones((1, lin_cols), F32).at[:, n_diff * DIFF_V_DIM:n_diff * DIFF_V_DIM + n_fox * HEAD_DIM].set(q_scale)
    lin = _proj(x_bf, w_t, 2 * qk_cols, lin_cols, "scale", (col_scale,),
                (pl.BlockSpec((1, min(PROJ_TN, lin_cols)), lambda j, i: (0, j)),))
    gates = _proj(x_bf, w_t, f0 + n_fox, w_in.shape[1] - f0 - n_fox, "sigmoid")

    lam_rows = jnp.stack([lam_q1, lam_k1, lam_q2, lam_k2]).astype(F32)
    n_maps = 2 * n_diff
    o_maps = _attention(qk, qk, lin, 0, n_maps, lambda h: h // 2, n_maps, DIFF_V_DIM)
    fq_col = n_diff * DIFF_V_DIM // HEAD_DIM
    o_fox = _attention(lin, lin, lin, fq_col, fq_col + n_fox, lambda h: fq_col + 2 * n_fox + h, n_fox, HEAD_DIM, fcum)

    merged = _merge(o_maps, o_fox, gates, lam_rows, diff_norm_g.astype(F32), w_proj_diff.astype(BF16),
                    w_proj_fox.astype(BF16), lam_init)
    w_router = jnp.concatenate([w_rg, jnp.moveaxis(w_re, 0, 1).reshape(d, N_EXPERTS)], axis=1)
    b_router = jnp.pad(jnp.concatenate([b_rg, b_re.reshape(N_EXPERTS)]).astype(F32),
                       (0, LANES - N_GROUPS - N_EXPERTS)).reshape(1, LANES)
    x1, ids, wts = _out_ln_route(merged, x2, w_out.astype(BF16), ln1_g.reshape(1, d), ln1_b.reshape(1, d),
                                 _split_bf16_pair(w_router, LANES), b_router, alpha)

    block = MOE_BLOCK
    n_blocks = -(-(s * TOP_K) // block) + N_EXPERTS
    dest, bexp, meta = _rank(ids, block, n_blocks)
    tmr = min(ROW_TM, s)
    dest2 = dest[:, :TOP_K]
    inv = _invert(dest2.reshape(s * TOP_K), n_blocks * block)
    ys = _experts(x1, inv, bexp[:, :3].T, meta[0, :1], w_gate, w_up, w_down, block)
    dest3 = dest2.reshape(s // tmr, 1, TOP_K * tmr)
    return _combine(ys, dest3, x1, wts, ln2_g.reshape(1, d), ln2_b.reshape(1, d), alpha)


def kernel(x, w_in, b_forget, lam_q1, lam_k1, lam_q2, lam_k2, diff_norm_g, w_proj_diff, w_proj_fox, w_out,
           ln1_g, ln1_b, w_router_group, b_router_group, w_router_expert, b_router_expert, w_gate, w_up,
           w_down, ln2_g, ln2_b):
    batch, s, d = x.shape
    depth = w_in.shape[0]
    params = (w_in, b_forget, lam_q1, lam_k1, lam_q2, lam_k2, diff_norm_g, w_proj_diff, w_proj_fox, w_out,
              ln1_g, ln1_b, w_router_group, b_router_group, w_router_expert, b_router_expert, w_gate, w_up,
              w_down, ln2_g, ln2_b)
    outs = []
    for bi in range(batch):
        h = x[bi]
        for layer in range(depth):
            h = _layer(h, layer, depth, *(p[layer] for p in params))
        outs.append(h)
    return jnp.stack(outs)
```

```python
import functools
import math

import jax
import jax.numpy as jnp
from jax import lax
from jax.experimental import pallas as pl
from jax.experimental.pallas import tpu as pltpu

F32, BF16, I32 = jnp.float32, jnp.bfloat16, jnp.int32

HEAD_DIM = 128
DIFF_V_DIM = 2 * HEAD_DIM
ROPE_THETA = 10000.0
N_GROUPS = 4
EXPERTS_PER_GROUP = 8
N_EXPERTS = N_GROUPS * EXPERTS_PER_GROUP
TOP_K = 2
LN_EPS = 1e-5
LOG2E = 1.4426950408889634
NEG = -1e30
LANES = 128
SUBLANES = 8
VMEM_LIMIT = 56 * 1024 * 1024

MOE_BLOCK = 128
PROJ_TM, PROJ_TN = 1024, 1024
PROJ_CAST_ROWS = 256
ATT_TQ = 512
ONES_ROWS = 16
PIPE_UNROLL = 4
FORGET_TS = 512
MERGE_TM = 512
OUTLN_TM = 512
RANK_TB = 512
ROW_TM = 256
INVERT_CHUNK = 1024
GATHER_DEPTH = 3
DMA_GROUP = 16


def _params(sem, vmem=VMEM_LIMIT):
    return pltpu.CompilerParams(dimension_semantics=sem, vmem_limit_bytes=vmem)


def _proj_kernel(x_ref, wt_hbm, *rest, epilogue, row0, q_scale):
    *extra, o_ref, w_f32, w_bf, sem = rest
    j, i = pl.program_id(0), pl.program_id(1)
    tn = w_bf.shape[0]
    slot = lax.rem(j, 2)

    def fetch(jj, s):
        rows = pl.ds(pl.multiple_of(row0 + jj * tn, SUBLANES), tn)
        return pltpu.make_async_copy(wt_hbm.at[rows, :], w_f32.at[s], sem.at[s])

    @pl.when(i == 0)
    def _():
        @pl.when(j == 0)
        def _():
            fetch(j, slot).start()

        fetch(j, slot).wait()

        @pl.when(j + 1 < pl.num_programs(0))
        def _():
            fetch(j + 1, 1 - slot).start()

        rows = min(PROJ_CAST_ROWS, tn)

        def chunk(c, carry):
            r = pl.ds(pl.multiple_of(c * rows, rows), rows)
            w_bf[r, :] = w_f32[slot, r, :].astype(BF16)
            return carry

        lax.fori_loop(0, tn // rows, chunk, 0)

    acc = lax.dot_general(x_ref[...], w_bf[...], (((1,), (1,)), ((), ())), preferred_element_type=F32)
    if epilogue == "rope":
        cos_ref, sin_ref = extra
        scale = jnp.where(j == 0, q_scale, 1.0)
        cosf, sinf = cos_ref[...] * scale, sin_ref[...] * scale
        for h in range(tn // HEAD_DIM):
            t = acc[:, h * HEAD_DIM:(h + 1) * HEAD_DIM]
            o_ref[:, h * HEAD_DIM:(h + 1) * HEAD_DIM] = (
                t * cosf + pltpu.roll(t, HEAD_DIM // 2, 1) * sinf).astype(o_ref.dtype)
    elif epilogue == "scale":
        o_ref[...] = (acc * extra[0][...]).astype(o_ref.dtype)
    else:
        o_ref[...] = (1.0 / (1.0 + jnp.exp(-acc))).astype(o_ref.dtype)


def _proj(x_bf, w_t, col0, n, epilogue, extra=(), extra_specs=(), q_scale=1.0):
    m, k = x_bf.shape
    tm, tn = min(PROJ_TM, m), min(PROJ_TN, n)
    assert col0 % SUBLANES == 0 and n % tn == 0
    return pl.pallas_call(
        functools.partial(_proj_kernel, epilogue=epilogue, row0=col0, q_scale=q_scale),
        out_shape=jax.ShapeDtypeStruct((m, n), BF16),
        grid=(n // tn, m // tm),
        in_specs=[pl.BlockSpec((tm, k), lambda j, i: (i, 0)), pl.BlockSpec(memory_space=pl.ANY), *extra_specs],
        out_specs=pl.BlockSpec((tm, tn), lambda j, i: (i, j)),
        scratch_shapes=[pltpu.VMEM((2, tn, k), F32), pltpu.VMEM((tn, k), BF16), pltpu.SemaphoreType.DMA((2,))],
        compiler_params=_params(("arbitrary", "arbitrary")),
        name=f"in_proj_{epilogue}",
    )(x_bf, w_t, *extra)


def _split3(v):
    h = v.astype(BF16)
    r = v - h.astype(F32)
    m = r.astype(BF16)
    return h, m, (r - m.astype(F32)).astype(BF16)


def _forget_kernel(x_ref, w_ref, b_ref, o_ref, xbf_ref, carry_ref):
    i = pl.program_id(0)

    @pl.when(i == 0)
    def _():
        carry_ref[...] = jnp.zeros_like(carry_ref)

    x = x_ref[...]
    xh = x.astype(BF16)
    xbf_ref[...] = xh
    xl = (x - xh.astype(F32)).astype(BF16)
    w = w_ref[...]
    a = jnp.dot(xh, w, preferred_element_type=F32)
    b = jnp.dot(xl, w[:, :LANES], preferred_element_type=F32)
    z = a[:, :LANES] + a[:, LANES:] + b + b_ref[...]
    logf = jnp.minimum(z, 0.0) - jnp.log1p(jnp.exp(-jnp.abs(z)))
    ts = logf.shape[0]
    tri = (lax.broadcasted_iota(I32, (ts, ts), 1) <= lax.broadcasted_iota(I32, (ts, ts), 0)).astype(BF16)
    c = carry_ref[0:1, :]
    for piece in _split3(logf):
        c = c + jnp.dot(tri, piece, preferred_element_type=F32)
    o_ref[...] = c * LOG2E
    carry_ref[...] = jnp.broadcast_to(c[ts - 1:ts, :], carry_ref.shape)


def _forget_cumsum(x, w_f, b_f):
    s, d = x.shape
    nh = w_f.shape[1]
    wh = w_f.astype(BF16)
    wl = (w_f - wh.astype(F32)).astype(BF16)
    pad = lambda a: jnp.pad(a, ((0, 0), (0, LANES - nh)))
    w2 = jnp.concatenate([pad(wh), pad(wl)], axis=1)
    b2 = jnp.pad(b_f.astype(F32), (0, LANES - nh)).reshape(1, LANES)
    ts = min(FORGET_TS, s)
    return pl.pallas_call(
        _forget_kernel,
        out_shape=(jax.ShapeDtypeStruct((s, LANES), F32), jax.ShapeDtypeStruct((s, d), BF16)),
        grid=(s // ts,),
        in_specs=[pl.BlockSpec((ts, d), lambda i: (i, 0)),
                  pl.BlockSpec((d, 2 * LANES), lambda i: (0, 0)),
                  pl.BlockSpec((1, LANES), lambda i: (0, 0))],
        out_specs=(pl.BlockSpec((ts, LANES), lambda i: (i, 0)), pl.BlockSpec((ts, d), lambda i: (i, 0))),
        scratch_shapes=[pltpu.VMEM((8, LANES), F32)],
        compiler_params=_params(("arbitrary",)),
        name="forget_cumsum",
    )(x, w2, b2)


def _pipeline3(n, stage_a, stage_b, stage_c, carry):
    if n < 3:
        for i in range(n):
            carry, post = stage_b(i, i % 2, i % PIPE_UNROLL, stage_a(i, i % 2), carry)
            stage_c(i, i % PIPE_UNROLL, post)
        return carry

    def step(i, m, state, do_a=True):
        aux, post_old, post_new, carry = state
        stage_c(i - 2, (m - 2) % PIPE_UNROLL, post_old)
        aux_next = stage_a(i + 1, (m + 1) % 2) if do_a else aux
        carry, post = stage_b(i, m % 2, m, aux, carry)
        return aux_next, post_new, post, carry

    aux0 = stage_a(0, 0)
    aux1 = stage_a(1, 1)
    carry, post0 = stage_b(0, 0, 0, aux0, carry)
    aux2 = stage_a(2, 0)
    carry, post1 = stage_b(1, 1, 1, aux1, carry)
    state = (aux2, post0, post1, carry)
    mid = n - 3

    def unrolled(h, s):
        for u in range(PIPE_UNROLL):
            s = step(PIPE_UNROLL * h + 2 + u, (2 + u) % PIPE_UNROLL, s)
        return s

    state = lax.fori_loop(0, mid // PIPE_UNROLL, unrolled, state)
    for i in range(mid - mid % PIPE_UNROLL + 2, n - 1):
        state = step(i, i % PIPE_UNROLL, state)
    _, post_old, post_new, carry = step(n - 1, (n - 1) % PIPE_UNROLL, state, do_a=False)
    stage_c(n - 2, (n - 2) % PIPE_UNROLL, post_old)
    stage_c(n - 1, (n - 1) % PIPE_UNROLL, post_new)
    return carry


def _attn_kernel(qi_tab, kb_tab, first_tab, q_ref, k_ref, v_ref, *rest, t, has_bias):
    if has_bias:
        c_ref, o_ref, kaug, qt, vt, st0, st1, *p, acc_all, m_all = rest
    else:
        o_ref, qt, vt, st0, st1, *p, acc_all, m_all = rest
        kaug = k_ref
    st = (st0, st1)
    seq, dv = v_ref.shape
    nq = seq // t
    dq = q_ref.shape[1]
    blk = lambda b: pl.ds(pl.multiple_of(b * t, t), t)
    head = pl.program_id(0)

    def prep(c, carry):
        rows = blk(c)
        lane = lax.broadcasted_iota(I32, (t, LANES), 1)
        if has_bias:
            neg_c = -jnp.sum(jnp.where(lane == head, c_ref[rows, :], 0.0), axis=1, keepdims=True)
            hi, mid, lo = (x.astype(F32) for x in _split3(neg_c))
            aug = jnp.where(lane == 0, hi, jnp.where(lane == 1, mid, jnp.where(lane == 2, lo, 0.0)))
            kaug[rows, :dq] = k_ref[rows, :]
            kaug[rows, dq:] = aug.astype(BF16)
            ones = (lax.broadcasted_iota(I32, (LANES, t), 0) < 3).astype(F32)
            qt[:, rows] = jnp.concatenate([q_ref[rows, :].astype(F32).T, ones], axis=0).astype(BF16)
        else:
            qt[:, rows] = q_ref[rows, :].astype(F32).T.astype(BF16)
        one_row = (lax.broadcasted_iota(I32, (ONES_ROWS, t), 0) == 0).astype(F32)
        vt[:, rows] = jnp.concatenate([v_ref[rows, :].astype(F32).T, one_row], axis=0).astype(BF16)
        return carry

    lax.fori_loop(0, nq, prep, 0)

    def scores(qi, kb):
        return jnp.dot(kaug[blk(kb), :], qt[:, blk(qi)], preferred_element_type=F32)

    def values(kb, slot):
        return jnp.dot(vt[:, blk(kb)], p[slot][...], preferred_element_type=F32)

    def save_max(qi, m):
        m_all[qi] = jnp.broadcast_to(m, m_all.shape[1:])

    def diag_scores(i, slot):
        mask = lax.broadcasted_iota(I32, (t, t), 1) >= lax.broadcasted_iota(I32, (t, t), 0)
        s = jnp.where(mask, scores(i, i), NEG)
        st[slot][...] = s
        return jnp.max(s, axis=0, keepdims=True)

    def diag_softmax(i, slot, pslot, cmax, carry):
        p[pslot][...] = jnp.exp2(st[slot][...] - cmax).astype(BF16)
        save_max(i, cmax)
        return carry, None

    def diag_values(i, slot, _):
        acc_all[i] = values(i, slot)

    _pipeline3(nq, diag_scores, diag_softmax, diag_values, 0)

    n_low = nq * (nq - 1) // 2
    if n_low:
        last = n_low - 1
        tab = lambda ref, f: ref[jnp.minimum(f, last)]

        def low_scores(f, slot):
            s = scores(tab(qi_tab, f), tab(kb_tab, f))
            st[slot][...] = s
            return jnp.max(s, axis=0, keepdims=True)

        def low_softmax(f, slot, pslot, cmax, m):
            qi = tab(qi_tab, f)
            m_prev = jnp.where(tab(first_tab, f) == 1, m_all[qi][0:1], m)
            m_new = jnp.maximum(m_prev, cmax)
            p[pslot][...] = jnp.exp2(st[slot][...] - m_new).astype(BF16)
            return m_new, jnp.exp2(m_prev - m_new)

        def low_values(f, slot, a):
            qi = tab(qi_tab, f)
            acc_all[qi] = a * acc_all[qi] + values(tab(kb_tab, f), slot)

        _pipeline3(n_low, low_scores, low_softmax, low_values, jnp.full((1, t), NEG, F32))

    def finish(qi, carry):
        acc = acc_all[qi]
        o_ref[blk(qi), :] = (acc[:dv] / acc[dv:dv + 1]).T.astype(o_ref.dtype)
        return carry

    lax.fori_loop(0, nq, finish, 0)


def _attention(q_src, k_src, v_src, q_col, k_col, v_col, n_heads, dv, fcum=None):
    s = q_src.shape[0]
    t = min(ATT_TQ, s)
    nq = s // t
    pairs = [(qi, kb) for qi in range(nq) for kb in range(qi)] or [(0, 0)]
    qi_tab = jnp.array([a for a, _ in pairs], I32)
    kb_tab = jnp.array([b for _, b in pairs], I32)
    first_tab = jnp.array([int(b == 0) for _, b in pairs], I32)
    has_bias = fcum is not None
    dk = 2 * HEAD_DIM if has_bias else HEAD_DIM
    in_specs = [pl.BlockSpec((s, HEAD_DIM), lambda h, *_: (0, q_col + h)),
                pl.BlockSpec((s, HEAD_DIM), lambda h, *_: (0, k_col + h)),
                pl.BlockSpec((s, dv), lambda h, *_: (0, v_col(h)))]
    args = [q_src, k_src, v_src]
    scratch = []
    if has_bias:
        in_specs.append(pl.BlockSpec((s, LANES), lambda h, *_: (0, 0)))
        args.append(fcum)
        scratch.append(pltpu.VMEM((s, dk), BF16))
    scratch += [pltpu.VMEM((dk, s), BF16), pltpu.VMEM((dv + ONES_ROWS, s), BF16),
                pltpu.VMEM((t, t), F32), pltpu.VMEM((t, t), F32), *[pltpu.VMEM((t, t), BF16)] * PIPE_UNROLL,
                pltpu.VMEM((nq, dv + ONES_ROWS, t), F32), pltpu.VMEM((nq, SUBLANES, t), F32)]
    return pl.pallas_call(
        functools.partial(_attn_kernel, t=t, has_bias=has_bias),
        out_shape=jax.ShapeDtypeStruct((s, n_heads * dv), BF16),
        grid_spec=pltpu.PrefetchScalarGridSpec(
            num_scalar_prefetch=3, grid=(n_heads,), in_specs=in_specs,
            out_specs=pl.BlockSpec((s, dv), lambda h, *_: (0, h)), scratch_shapes=scratch),
        compiler_params=_params(("arbitrary",)),
        name="fox_attention" if has_bias else "diff_attention",
    )(qi_tab, kb_tab, first_tab, *args)


def _merge_kernel(om_ref, of_ref, gd_ref, gf_ref, lam_ref, g_ref, wd_ref, wf_ref, o_ref, *, lam_init):
    lam_v = lam_ref[...]
    lam = (jnp.exp(jnp.sum(lam_v[0:1] * lam_v[1:2], axis=1, keepdims=True))
           - jnp.exp(jnp.sum(lam_v[2:3] * lam_v[3:4], axis=1, keepdims=True)) + lam_init)
    heads = []
    for h in range(g_ref.shape[0]):
        o1 = om_ref[:, (2 * h) * DIFF_V_DIM:(2 * h + 1) * DIFF_V_DIM].astype(F32)
        o2 = om_ref[:, (2 * h + 1) * DIFF_V_DIM:(2 * h + 2) * DIFF_V_DIM].astype(F32)
        o = o1 - lam * o2
        o = o * lax.rsqrt(jnp.mean(o * o, axis=1, keepdims=True) + LN_EPS) * g_ref[h:h + 1, :] * (1.0 - lam_init)
        heads.append(o.astype(BF16))
    ud = jnp.dot(jnp.concatenate(heads, axis=1), wd_ref[...], preferred_element_type=F32)
    uf = jnp.dot(of_ref[...], wf_ref[...], preferred_element_type=F32)
    o_ref[...] = (gd_ref[...].astype(F32) * ud + gf_ref[...].astype(F32) * uf).astype(o_ref.dtype)


def _merge(o_maps, o_fox, gates, lam_rows, norm_g, wd, wf, lam_init):
    s, d = o_maps.shape[0], wd.shape[1]
    tm = min(MERGE_TM, s)
    row = lambda c: pl.BlockSpec((tm, c), lambda i: (i, 0))
    full = lambda a: pl.BlockSpec(a.shape, lambda i: (0, 0))
    return pl.pallas_call(
        functools.partial(_merge_kernel, lam_init=lam_init),
        out_shape=jax.ShapeDtypeStruct((s, d), BF16),
        grid=(s // tm,),
        in_specs=[row(o_maps.shape[1]), row(o_fox.shape[1]),
                  pl.BlockSpec((tm, d), lambda i: (i, 0)), pl.BlockSpec((tm, d), lambda i: (i, 1)),
                  full(lam_rows), full(norm_g), full(wd), full(wf)],
        out_specs=row(d),
        compiler_params=_params(("parallel",)),
        name="merge_branches",
    )(o_maps, o_fox, gates, gates, lam_rows, norm_g, wd, wf)


def _layer_norm(y, g, b):
    mu = jnp.mean(y, axis=1, keepdims=True)
    yc = y - mu
    var = jnp.mean(yc * yc, axis=1, keepdims=True)
    return yc * lax.rsqrt(var + LN_EPS) * g + b


def _route(lg):
    lane = lax.broadcasted_iota(I32, lg.shape, 1)
    far = jnp.int32(4 * LANES)
    is_g = lane < N_GROUPS
    gl = jnp.where(is_g, lg, NEG)
    gmax = jnp.max(gl, axis=1, keepdims=True)
    gidx = jnp.min(jnp.where(gl == gmax, lane, far), axis=1, keepdims=True)
    top_gp = 1.0 / jnp.sum(jnp.where(is_g, jnp.exp(gl - gmax), 0.0), axis=1, keepdims=True)
    lo = N_GROUPS + gidx * EXPERTS_PER_GROUP
    el = jnp.where((lane >= lo) & (lane < lo + EXPERTS_PER_GROUP), lg, NEG)
    m1 = jnp.max(el, axis=1, keepdims=True)
    i1 = jnp.min(jnp.where(el == m1, lane, far), axis=1, keepdims=True)
    el2 = jnp.where(lane == i1, 2.0 * NEG, el)
    m2 = jnp.max(el2, axis=1, keepdims=True)
    i2 = jnp.min(jnp.where(el2 == m2, lane, far), axis=1, keepdims=True)
    d = jnp.exp(m2 - m1)
    w1 = top_gp / (1.0 + d)
    w2 = w1 * d
    ids = jnp.where(lane == 0, i1 - N_GROUPS, jnp.where(lane == 1, i2 - N_GROUPS, 0))
    wts = jnp.where(lane == 0, w1, jnp.where(lane == 1, w2, 0.0))
    return ids, wts


def _outln_kernel(mg_ref, x_ref, wo_ref, g_ref, b_ref, wr_ref, br_ref, x1_ref, ids_ref, wts_ref, *, alpha):
    mix = jnp.dot(mg_ref[...], wo_ref[...], preferred_element_type=F32)
    x1 = _layer_norm(alpha * x_ref[...] + mix, g_ref[...], b_ref[...])
    x1_ref[...] = x1
    xh = x1.astype(BF16)
    xl = (x1 - xh.astype(F32)).astype(BF16)
    wr = wr_ref[...]
    a = jnp.dot(xh, wr, preferred_element_type=F32)
    b = jnp.dot(xl, wr[:, :LANES], preferred_element_type=F32)
    ids, wts = _route(a[:, :LANES] + a[:, LANES:] + b + br_ref[...])
    ids_ref[...] = ids
    wts_ref[...] = wts


def _out_ln_route(merged, x, w_out, g, b, wr2, br, alpha):
    s, d = x.shape
    tm = min(OUTLN_TM, s)
    row = lambda c: pl.BlockSpec((tm, c), lambda i: (i, 0))
    full = lambda a: pl.BlockSpec(a.shape, lambda i: (0, 0))
    return pl.pallas_call(
        functools.partial(_outln_kernel, alpha=alpha),
        out_shape=(jax.ShapeDtypeStruct((s, d), F32), jax.ShapeDtypeStruct((s, LANES), I32),
                   jax.ShapeDtypeStruct((s, LANES), F32)),
        grid=(s // tm,),
        in_specs=[row(d), row(d), full(w_out), full(g), full(b), full(wr2), full(br)],
        out_specs=(row(d), row(LANES), row(LANES)),
        compiler_params=_params(("parallel",)),
        name="out_proj_ln_route",
    )(merged, x, w_out, g, b, wr2, br)


def _rank_kernel(ids_ref, dest_ref, bexp_ref, meta_ref, cnt_ref, carry_ref, start_ref, *, block, n_blocks):
    ph, i = pl.program_id(0), pl.program_id(1)
    ids = ids_ref[...]
    tb = ids.shape[0]
    lane = lax.broadcasted_iota(I32, (tb, LANES), 1)
    oh1 = (lane == ids[:, 0:1]).astype(F32)
    oh2 = (lane == ids[:, 1:2]).astype(F32)
    oh = oh1 + oh2
    colsum = jnp.sum(oh, axis=0, keepdims=True)

    @pl.when((ph == 0) & (i == 0))
    def _():
        cnt_ref[...] = jnp.zeros_like(cnt_ref)

    @pl.when(ph == 0)
    def _():
        cnt_ref[...] += jnp.broadcast_to(colsum, cnt_ref.shape)

    @pl.when((ph == 1) & (i == 0))
    def _():
        cnt = cnt_ref[0:1, :]
        nblk = jnp.floor((cnt + (block - 1)) * (1.0 / block))
        r = lax.broadcasted_iota(I32, (LANES, LANES), 0)
        c = lax.broadcasted_iota(I32, (LANES, LANES), 1)
        upper = (r < c).astype(BF16)
        nb8 = jnp.broadcast_to(nblk, (8, LANES)).astype(BF16)
        bstart = jnp.dot(nb8, upper, preferred_element_type=F32)
        start_ref[...] = bstart * block
        carry_ref[...] = jnp.zeros_like(carry_ref)
        lane1 = lax.broadcasted_iota(I32, (1, LANES), 1)
        bend = jnp.where(lane1 < N_EXPERTS, bstart[0:1] + nblk, 4.0 * n_blocks)
        bidx = lax.broadcasted_iota(I32, (n_blocks, LANES), 0).astype(F32)
        be = jnp.sum((jnp.broadcast_to(bend, (n_blocks, LANES)) <= bidx).astype(F32), axis=1, keepdims=True)
        be = jnp.minimum(be, N_EXPERTS - 1.0)
        lane_b = lax.broadcasted_iota(I32, (n_blocks, LANES), 1).astype(F32)
        owns = jnp.broadcast_to((nblk > 0.0) & (lane1 < N_EXPERTS), (n_blocks, LANES))
        nxt = jnp.min(jnp.where(owns & (lane_b > be), lane_b, 1.0 * LANES), axis=1, keepdims=True)
        nxt = jnp.where(nxt >= 1.0 * LANES, -1.0, nxt)
        before = jnp.sum((owns & (lane_b < be)).astype(F32), axis=1, keepdims=True)
        parity = before - 2.0 * jnp.floor(before * 0.5)
        bexp_ref[...] = jnp.where(lane_b == 0.0, be, jnp.where(lane_b == 1.0, nxt, parity)).astype(I32)
        total = jnp.sum(jnp.where(lane1 < N_EXPERTS, nblk, 0.0), axis=1, keepdims=True)
        row = lax.broadcasted_iota(I32, (8, LANES), 0)
        first_pad = start_ref[0:1, :] + cnt
        meta = jnp.where(row == 0, jnp.broadcast_to(total, (8, LANES)),
                         jnp.where(row == 1, jnp.broadcast_to(first_pad, (8, LANES)),
                                   jnp.broadcast_to(bstart[0:1] * block + nblk * block, (8, LANES))))
        meta_ref[...] = meta.astype(I32)

    @pl.when(ph == 1)
    def _():
        rr = lax.broadcasted_iota(I32, (tb, tb), 0)
        cc = lax.broadcasted_iota(I32, (tb, tb), 1)
        lower = (cc < rr).astype(BF16)
        prefix = jnp.dot(lower, oh.astype(BF16), preferred_element_type=F32)
        pos = prefix + carry_ref[0:1, :] + start_ref[0:1, :]
        d1 = jnp.sum(pos * oh1, axis=1, keepdims=True)
        d2 = jnp.sum(pos * oh2, axis=1, keepdims=True)
        dest_ref[...] = jnp.where(lane == 0, d1, jnp.where(lane == 1, d2, 0.0)).astype(I32)
        carry_ref[...] += jnp.broadcast_to(colsum, carry_ref.shape)


def _rank(ids, block, n_blocks):
    t = ids.shape[0]
    tb = min(RANK_TB, t)
    return pl.pallas_call(
        functools.partial(_rank_kernel, block=block, n_blocks=n_blocks),
        out_shape=(jax.ShapeDtypeStruct((t, LANES), I32), jax.ShapeDtypeStruct((n_blocks, LANES), I32),
                   jax.ShapeDtypeStruct((8, LANES), I32)),
        grid=(2, t // tb),
        in_specs=[pl.BlockSpec((tb, LANES), lambda p, i: (i, 0))],
        out_specs=(pl.BlockSpec((tb, LANES), lambda p, i: (i * p, 0)),
                   pl.BlockSpec((n_blocks, LANES), lambda p, i: (0, 0)),
                   pl.BlockSpec((8, LANES), lambda p, i: (0, 0))),
        scratch_shapes=[pltpu.VMEM((8, LANES), F32)] * 3,
        compiler_params=_params(("arbitrary", "arbitrary")),
        name="moe_rank",
    )(ids)


def _invert_kernel(dest_ref, zeros_hbm, inv_hbm, inv_ref, sem):
    i = pl.program_id(0)
    n = dest_ref.shape[0]

    @pl.when(i == 0)
    def _():
        fill = pltpu.make_async_copy(zeros_hbm, inv_ref, sem)
        fill.start()
        fill.wait()

    def body(j, c):
        inv_ref[dest_ref[j]] = i * n + j
        return c

    lax.fori_loop(0, n, body, 0, unroll=8)

    @pl.when(i == pl.num_programs(0) - 1)
    def _():
        flush = pltpu.make_async_copy(inv_ref, inv_hbm, sem)
        flush.start()
        flush.wait()


def _invert(dest_flat, rows):
    a = dest_flat.shape[0]
    chunk = min(INVERT_CHUNK, a)
    return pl.pallas_call(
        _invert_kernel,
        out_shape=jax.ShapeDtypeStruct((rows,), I32),
        grid=(a // chunk,),
        in_specs=[pl.BlockSpec((chunk,), lambda i: (i,), memory_space=pltpu.SMEM),
                  pl.BlockSpec(memory_space=pl.ANY)],
        out_specs=pl.BlockSpec(memory_space=pl.ANY),
        scratch_shapes=[pltpu.SMEM((rows,), I32), pltpu.SemaphoreType.DMA],
        compiler_params=_params(("arbitrary",)),
        name="moe_invert",
    )(dest_flat, jnp.zeros((rows,), I32))


def _expert_kernel(tab_ref, nact_ref, inv0_ref, inv1_ref, inv2_ref, x1_ref, wg_hbm, wu_hbm, wd_hbm, ys_ref,
                   xbuf, wg_f, wu_f, wd_f, wg_b, wu_b, wd_b, gsem, wsem):
    b = pl.program_id(0)
    n_active = nact_ref[0]
    active = b < n_active
    block = xbuf.shape[1]
    expert, next_expert, wslot = tab_ref[0, b], tab_ref[1, b], tab_ref[2, b]
    first_of_expert = (b == 0) | (tab_ref[0, jnp.maximum(b - 1, 0)] != expert)

    def row_copy(token, slot, r):
        return pltpu.make_async_copy(x1_ref.at[pl.ds(token, 1)], xbuf.at[slot, pl.ds(r, 1)], gsem.at[slot])

    def gather(inv_ref, blk):
        for slot in range(GATHER_DEPTH):
            @pl.when(lax.rem(blk, GATHER_DEPTH) == slot)
            def _():
                for r0 in range(0, block, DMA_GROUP):
                    tokens = [lax.shift_right_logical(inv_ref[0, 0, r], 1) for r in range(r0, r0 + DMA_GROUP)]
                    for j, token in enumerate(tokens):
                        row_copy(token, slot, r0 + j).start()

    def weight_copies(e, s):
        return (pltpu.make_async_copy(wg_hbm.at[e], wg_f.at[s], wsem.at[s, 0]),
                pltpu.make_async_copy(wu_hbm.at[e], wu_f.at[s], wsem.at[s, 1]),
                pltpu.make_async_copy(wd_hbm.at[e], wd_f.at[s], wsem.at[s, 2]))

    @pl.when(b == 0)
    def _():
        for c in weight_copies(expert, wslot):
            c.start(priority=1)
        gather(inv0_ref, b)

    @pl.when((b == 0) & (n_active > 1))
    def _():
        gather(inv1_ref, b + 1)

    @pl.when(b + 2 < n_active)
    def _():
        gather(inv2_ref, b + 2)

    @pl.when(active & first_of_expert)
    def _():
        for c in weight_copies(expert, wslot):
            c.wait()

        @pl.when(next_expert >= 0)
        def _():
            for c in weight_copies(next_expert, 1 - wslot):
                c.start(priority=1)

        wg_b[...] = wg_f[wslot].astype(BF16)
        wu_b[...] = wu_f[wslot].astype(BF16)
        wd_b[...] = wd_f[wslot].astype(BF16)

    @pl.when(active)
    def _():
        for r in range(block):
            row_copy(0, lax.rem(b, GATHER_DEPTH), r).wait()
        x = xbuf[lax.rem(b, GATHER_DEPTH)].astype(BF16)
        g = jnp.dot(x, wg_b[...], preferred_element_type=F32)
        u = jnp.dot(x, wu_b[...], preferred_element_type=F32)
        h = (g / (1.0 + jnp.exp(-g)) * u).astype(BF16)
        ys_ref[...] = jnp.dot(h, wd_b[...], preferred_element_type=F32)

    @pl.when(jnp.logical_not(active))
    def _():
        ys_ref[...] = jnp.zeros_like(ys_ref)


def _experts(x1, inv, tab, nact, w_gate, w_up, w_down, block):
    n_blocks = tab.shape[1]
    d, ff = w_gate.shape[1], w_gate.shape[2]
    inv3 = inv.reshape(n_blocks, 1, block)
    ahead = lambda k: pl.BlockSpec((1, 1, block), lambda b, tb, na: (jnp.minimum(b + k, n_blocks - 1), 0, 0),
                                   memory_space=pltpu.SMEM)
    hbm = pl.BlockSpec(memory_space=pl.ANY)
    return pl.pallas_call(
        _expert_kernel,
        out_shape=jax.ShapeDtypeStruct((n_blocks * block, d), F32),
        grid_spec=pltpu.PrefetchScalarGridSpec(
            num_scalar_prefetch=2, grid=(n_blocks,),
            in_specs=[ahead(0), ahead(1), ahead(2), hbm, hbm, hbm, hbm],
            out_specs=pl.BlockSpec((block, d), lambda b, tb, na: (b, 0)),
            scratch_shapes=[pltpu.VMEM((GATHER_DEPTH, block, d), F32),
                            pltpu.VMEM((2, d, ff), F32), pltpu.VMEM((2, d, ff), F32), pltpu.VMEM((2, ff, d), F32),
                            pltpu.VMEM((d, ff), BF16), pltpu.VMEM((d, ff), BF16), pltpu.VMEM((ff, d), BF16),
                            pltpu.SemaphoreType.DMA((GATHER_DEPTH,)), pltpu.SemaphoreType.DMA((2, 3))]),
        compiler_params=_params(("arbitrary",)),
        name="moe_experts",
    )(tab, nact, inv3, inv3, inv3, x1, w_gate, w_up, w_down)


def _combine_kernel(dest_ref, next_ref, x1_ref, wts_ref, g_ref, b_ref, ys_ref, o_ref, buf, sem, *, alpha):
    i = pl.program_id(0)
    tm = x1_ref.shape[0]
    slot = lax.rem(i, 2)

    def row_copy(row, s, r, k):
        return pltpu.make_async_copy(ys_ref.at[pl.ds(row, 1)], buf.at[s, k, pl.ds(r, 1)], sem.at[s])

    def gather(idx_ref, dyn_slot):
        for s in range(2):
            @pl.when(dyn_slot == s)
            def _():
                for a0 in range(0, TOP_K * tm, DMA_GROUP):
                    rows = [idx_ref[0, 0, a] for a in range(a0, a0 + DMA_GROUP)]
                    for j, row in enumerate(rows):
                        r, k = divmod(a0 + j, TOP_K)
                        row_copy(row, s, r, k).start(priority=k % 2)

    @pl.when(i == 0)
    def _():
        gather(dest_ref, slot)

    @pl.when(i + 1 < pl.num_programs(0))
    def _():
        gather(next_ref, 1 - slot)

    for r in range(tm):
        for k in range(TOP_K):
            row_copy(0, slot, r, k).wait()
    w = wts_ref[...]
    ffn = w[:, 0:1] * buf[slot, 0] + w[:, 1:2] * buf[slot, 1]
    o_ref[...] = _layer_norm(alpha * x1_ref[...] + ffn, g_ref[...], b_ref[...])


def _combine(ys, dest3, x1, wts, g, b, alpha):
    t, d = x1.shape
    tm = dest3.shape[2] // TOP_K
    row = lambda c: pl.BlockSpec((tm, c), lambda i: (i, 0))
    full = lambda a: pl.BlockSpec(a.shape, lambda i: (0, 0))
    return pl.pallas_call(
        functools.partial(_combine_kernel, alpha=alpha),
        out_shape=jax.ShapeDtypeStruct((t, d), F32),
        grid=(t // tm,),
        in_specs=[pl.BlockSpec((1, 1, TOP_K * tm), lambda i: (i, 0, 0), memory_space=pltpu.SMEM),
                  pl.BlockSpec((1, 1, TOP_K * tm), lambda i: (jnp.minimum(i + 1, t // tm - 1), 0, 0),
                               memory_space=pltpu.SMEM),
                  row(d), row(LANES), full(g), full(b), pl.BlockSpec(memory_space=pl.ANY)],
        out_specs=row(d),
        scratch_shapes=[pltpu.VMEM((2, TOP_K, tm, d), F32), pltpu.SemaphoreType.DMA((2,))],
        compiler_params=_params(("arbitrary",)),
        name="moe_combine_ln",
    )(dest3, dest3, x1, wts, g, b, ys)


def _rope_tables(s):
    inv_freq = ROPE_THETA ** (-jnp.arange(0, HEAD_DIM, 2, dtype=F32) / HEAD_DIM)
    ang = jnp.arange(s, dtype=F32)[:, None] * inv_freq[None, :]
    cos, sin = jnp.cos(ang), jnp.sin(ang)
    return jnp.concatenate([cos, cos], axis=1), jnp.concatenate([-sin, sin], axis=1)


def _split_bf16_pair(w, width):
    hi = w.astype(BF16)
    lo = (w - hi.astype(F32)).astype(BF16)
    pad = lambda a: jnp.pad(a, ((0, 0), (0, width - a.shape[1])))
    return jnp.concatenate([pad(hi), pad(lo)], axis=1)


def _layer(x2, layer, depth, w_in, b_forget, lam_q1, lam_k1, lam_q2, lam_k2, diff_norm_g, w_proj_diff,
           w_proj_fox, w_out, ln1_g, ln1_b, w_rg, b_rg, w_re, b_re, w_gate, w_up, w_down, ln2_g, ln2_b):
    s, d = x2.shape
    alpha = (2 * depth) ** 0.25
    lam_init = 0.8 - 0.6 * math.exp(-0.3 * layer)
    n_diff, n_fox = diff_norm_g.shape[0], b_forget.shape[0]
    qk_cols = n_diff * 2 * HEAD_DIM
    lin_cols = n_diff * DIFF_V_DIM + 3 * n_fox * HEAD_DIM
    q_scale = HEAD_DIM ** -0.5 * LOG2E

    f0 = 2 * qk_cols + lin_cols
    w_t = w_in.T
    fcum, x_bf = _forget_cumsum(x2, w_t[f0:f0 + n_fox].T, b_forget)

    cos_t, sin_t = _rope_tables(s)
    tm = min(PROJ_TM, s)
    tab = pl.BlockSpec((tm, HEAD_DIM), lambda j, i: (i, 0))
    assert qk_cols == PROJ_TN, "column block 0 of the rotary call must be exactly the queries"
    qk = _proj(x_bf, w_t, 0, 2 * qk_cols, "rope", (cos_t, sin_t), (tab, tab), q_scale)
    col_scale = jnp.ones((1, lin_cols), F32).at[:, n_diff * DIFF_V_DIM:n_diff * DIFF_V_DIM + n_fox * HEAD_DIM].set(q_scale)
    lin = _proj(x_bf, w_t, 2 * qk_cols, lin_cols, "scale", (col_scale,),
                (pl.BlockSpec((1, min(PROJ_TN, lin_cols)), lambda j, i: (0, j)),))
    gates = _proj(x_bf, w_t, f0 + n_fox, w_in.shape[1] - f0 - n_fox, "sigmoid")

    lam_rows = jnp.stack([lam_q1, lam_k1, lam_q2, lam_k2]).astype(F32)
    n_maps = 2 * n_diff
    o_maps = _attention(qk, qk, lin, 0, n_maps, lambda h: h // 2, n_maps, DIFF_V_DIM)
    fq_col = n_diff * DIFF_V_DIM // HEAD_DIM
    o_fox = _attention(lin, lin, lin, fq_col, fq_col + n_fox, lambda h: fq_col + 2 * n_fox + h, n_fox, HEAD_DIM, fcum)

    merged = _merge(o_maps, o_fox, gates, lam_rows, diff_norm_g.astype(F32), w_proj_diff.astype(BF16),
                    w_proj_fox.astype(BF16), lam_init)
    w_router = jnp.concatenate([w_rg, jnp.moveaxis(w_re, 0, 1).reshape(d, N_EXPERTS)], axis=1)
    b_router = jnp.pad(jnp.concatenate([b_rg, b_re.reshape(N_EXPERTS)]).astype(F32),
                       (0, LANES - N_GROUPS - N_EXPERTS)).reshape(1, LANES)
    x1, ids, wts = _out_ln_route(merged, x2, w_out.astype(BF16), ln1_g.reshape(1, d), ln1_b.reshape(1, d),
                                 _split_bf16_pair(w_router, LANES), b_router, alpha)

    block = MOE_BLOCK
    n_blocks = -(-(s * TOP_K) // block) + N_EXPERTS
    dest, bexp, meta = _rank(ids, block, n_blocks)
    tmr = min(ROW_TM, s)
    dest2 = dest[:, :TOP_K]
    inv = _invert(dest2.reshape(s * TOP_K), n_blocks * block)
    ys = _experts(x1, inv, bexp[:, :3].T, meta[0, :1], w_gate, w_up, w_down, block)
    dest3 = dest2.reshape(s // tmr, 1, TOP_K * tmr)
    return _combine(ys, dest3, x1, wts, ln2_g.reshape(1, d), ln2_b.reshape(1, d), alpha)


def kernel(x, w_in, b_forget, lam_q1, lam_k1, lam_q2, lam_k2, diff_norm_g, w_proj_diff, w_proj_fox, w_out,
           ln1_g, ln1_b, w_router_group, b_router_group, w_router_expert, b_router_expert, w_gate, w_up,
           w_down, ln2_g, ln2_b):
    batch, s, d = x.shape
    depth = w_in.shape[0]
    params = (w_in, b_forget, lam_q1, lam_k1, lam_q2, lam_k2, diff_norm_g, w_proj_diff, w_proj_fox, w_out,
              ln1_g, ln1_b, w_router_group, b_router_group, w_router_expert, b_router_expert, w_gate, w_up,
              w_down, ln2_g, ln2_b)
    outs = []
    for bi in range(batch):
        h = x[bi]
        for layer in range(depth):
            h = _layer(h, layer, depth, *(p[layer] for p in params))
        outs.append(h)
    return jnp.stack(outs)
```

```python
import functools
import math

import jax
import jax.numpy as jnp
from jax import lax
from jax.experimental import pallas as pl
from jax.experimental.pallas import tpu as pltpu

F32, BF16, I32 = jnp.float32, jnp.bfloat16, jnp.int32

HEAD_DIM = 128
DIFF_V_DIM = 2 * HEAD_DIM
ROPE_THETA = 10000.0
N_GROUPS = 4
EXPERTS_PER_GROUP = 8
N_EXPERTS = N_GROUPS * EXPERTS_PER_GROUP
TOP_K = 2
LN_EPS = 1e-5
LOG2E = 1.4426950408889634
NEG = -1e30
LANES = 128
SUBLANES = 8
VMEM_LIMIT = 56 * 1024 * 1024

MOE_BLOCK = 128
PROJ_TM, PROJ_TN = 1024, 1024
PROJ_CAST_ROWS = 256
ATT_TQ = 512
ONES_ROWS = 16
FOX_UNROLL, DIFF_UNROLL = 8, 4
FORGET_TS = 512
MERGE_TM = 512
OUTLN_TM = 512
RANK_TB = 512
ROW_TM = 256
INVERT_CHUNK = 1024
GATHER_DEPTH = 3
DMA_GROUP = 16


def _params(sem, vmem=VMEM_LIMIT):
    return pltpu.CompilerParams(dimension_semantics=sem, vmem_limit_bytes=vmem)


def _proj_kernel(x_ref, wt_hbm, *rest, epilogue, row0, q_scale):
    *extra, o_ref, w_f32, w_bf, sem = rest
    j, i = pl.program_id(0), pl.program_id(1)
    tn = w_bf.shape[0]
    slot = lax.rem(j, 2)

    def fetch(jj, s):
        rows = pl.ds(pl.multiple_of(row0 + jj * tn, SUBLANES), tn)
        return pltpu.make_async_copy(wt_hbm.at[rows, :], w_f32.at[s], sem.at[s])

    @pl.when(i == 0)
    def _():
        @pl.when(j == 0)
        def _():
            fetch(j, slot).start()

        fetch(j, slot).wait()

        @pl.when(j + 1 < pl.num_programs(0))
        def _():
            fetch(j + 1, 1 - slot).start()

        rows = min(PROJ_CAST_ROWS, tn)

        def chunk(c, carry):
            r = pl.ds(pl.multiple_of(c * rows, rows), rows)
            w_bf[r, :] = w_f32[slot, r, :].astype(BF16)
            return carry

        lax.fori_loop(0, tn // rows, chunk, 0)

    acc = lax.dot_general(x_ref[...], w_bf[...], (((1,), (1,)), ((), ())), preferred_element_type=F32)
    if epilogue == "rope":
        cos_ref, sin_ref = extra
        scale = jnp.where(j == 0, q_scale, 1.0)
        cosf, sinf = cos_ref[...] * scale, sin_ref[...] * scale
        for h in range(tn // HEAD_DIM):
            t = acc[:, h * HEAD_DIM:(h + 1) * HEAD_DIM]
            o_ref[:, h * HEAD_DIM:(h + 1) * HEAD_DIM] = (
                t * cosf + pltpu.roll(t, HEAD_DIM // 2, 1) * sinf).astype(o_ref.dtype)
    elif epilogue == "scale":
        o_ref[...] = (acc * extra[0][...]).astype(o_ref.dtype)
    else:
        o_ref[...] = (1.0 / (1.0 + jnp.exp(-acc))).astype(o_ref.dtype)


def _proj(x_bf, w_t, col0, n, epilogue, extra=(), extra_specs=(), q_scale=1.0):
    m, k = x_bf.shape
    tm, tn = min(PROJ_TM, m), min(PROJ_TN, n)
    assert col0 % SUBLANES == 0 and n % tn == 0
    return pl.pallas_call(
        functools.partial(_proj_kernel, epilogue=epilogue, row0=col0, q_scale=q_scale),
        out_shape=jax.ShapeDtypeStruct((m, n), BF16),
        grid=(n // tn, m // tm),
        in_specs=[pl.BlockSpec((tm, k), lambda j, i: (i, 0)), pl.BlockSpec(memory_space=pl.ANY), *extra_specs],
        out_specs=pl.BlockSpec((tm, tn), lambda j, i: (i, j)),
        scratch_shapes=[pltpu.VMEM((2, tn, k), F32), pltpu.VMEM((tn, k), BF16), pltpu.SemaphoreType.DMA((2,))],
        compiler_params=_params(("arbitrary", "arbitrary")),
        name=f"in_proj_{epilogue}",
    )(x_bf, w_t, *extra)


def _split3(v):
    h = v.astype(BF16)
    r = v - h.astype(F32)
    m = r.astype(BF16)
    return h, m, (r - m.astype(F32)).astype(BF16)


def _forget_kernel(x_ref, w_ref, b_ref, o_ref, xbf_ref, carry_ref):
    i = pl.program_id(0)

    @pl.when(i == 0)
    def _():
        carry_ref[...] = jnp.zeros_like(carry_ref)

    x = x_ref[...]
    xh = x.astype(BF16)
    xbf_ref[...] = xh
    xl = (x - xh.astype(F32)).astype(BF16)
    w = w_ref[...]
    a = jnp.dot(xh, w, preferred_element_type=F32)
    b = jnp.dot(xl, w[:, :LANES], preferred_element_type=F32)
    z = a[:, :LANES] + a[:, LANES:] + b + b_ref[...]
    logf = jnp.minimum(z, 0.0) - jnp.log1p(jnp.exp(-jnp.abs(z)))
    ts = logf.shape[0]
    tri = (lax.broadcasted_iota(I32, (ts, ts), 1) <= lax.broadcasted_iota(I32, (ts, ts), 0)).astype(BF16)
    c = carry_ref[0:1, :]
    for piece in _split3(logf):
        c = c + jnp.dot(tri, piece, preferred_element_type=F32)
    o_ref[...] = c * LOG2E
    carry_ref[...] = jnp.broadcast_to(c[ts - 1:ts, :], carry_ref.shape)


def _forget_cumsum(x, w_f, b_f):
    s, d = x.shape
    nh = w_f.shape[1]
    wh = w_f.astype(BF16)
    wl = (w_f - wh.astype(F32)).astype(BF16)
    pad = lambda a: jnp.pad(a, ((0, 0), (0, LANES - nh)))
    w2 = jnp.concatenate([pad(wh), pad(wl)], axis=1)
    b2 = jnp.pad(b_f.astype(F32), (0, LANES - nh)).reshape(1, LANES)
    ts = min(FORGET_TS, s)
    return pl.pallas_call(
        _forget_kernel,
        out_shape=(jax.ShapeDtypeStruct((s, LANES), F32), jax.ShapeDtypeStruct((s, d), BF16)),
        grid=(s // ts,),
        in_specs=[pl.BlockSpec((ts, d), lambda i: (i, 0)),
                  pl.BlockSpec((d, 2 * LANES), lambda i: (0, 0)),
                  pl.BlockSpec((1, LANES), lambda i: (0, 0))],
        out_specs=(pl.BlockSpec((ts, LANES), lambda i: (i, 0)), pl.BlockSpec((ts, d), lambda i: (i, 0))),
        scratch_shapes=[pltpu.VMEM((8, LANES), F32)],
        compiler_params=_params(("arbitrary",)),
        name="forget_cumsum",
    )(x, w2, b2)


def _pipeline3(n, unroll, stage_a, stage_b, stage_c, carry):
    if n < 3:
        for i in range(n):
            carry, post = stage_b(i, i % 2, i % unroll, stage_a(i, i % 2), carry)
            stage_c(i, i % unroll, post)
        return carry

    def step(i, m, state, do_a=True):
        aux, post_old, post_new, carry = state
        stage_c(i - 2, (m - 2) % unroll, post_old)
        aux_next = stage_a(i + 1, (m + 1) % 2) if do_a else aux
        carry, post = stage_b(i, m % 2, m, aux, carry)
        return aux_next, post_new, post, carry

    aux0 = stage_a(0, 0)
    aux1 = stage_a(1, 1)
    carry, post0 = stage_b(0, 0, 0, aux0, carry)
    aux2 = stage_a(2, 0)
    carry, post1 = stage_b(1, 1, 1, aux1, carry)
    state = (aux2, post0, post1, carry)
    mid = n - 3

    def unrolled(h, s):
        for u in range(unroll):
            s = step(unroll * h + 2 + u, (2 + u) % unroll, s)
        return s

    state = lax.fori_loop(0, mid // unroll, unrolled, state)
    for i in range(mid - mid % unroll + 2, n - 1):
        state = step(i, i % unroll, state)
    _, post_old, post_new, carry = step(n - 1, (n - 1) % unroll, state, do_a=False)
    stage_c(n - 2, (n - 2) % unroll, post_old)
    stage_c(n - 1, (n - 1) % unroll, post_new)
    return carry


def _attn_kernel(qi_tab, kb_tab, first_tab, q_ref, k_ref, v_ref, *rest, t, has_bias, unroll):
    if has_bias:
        c_ref, o_ref, kaug, qt, vt, st0, st1, *p, acc_all, m_all = rest
    else:
        o_ref, qt, vt, st0, st1, *p, acc_all, m_all = rest
        kaug = k_ref
    st = (st0, st1)
    seq, dv = v_ref.shape
    nq = seq // t
    dq = q_ref.shape[1]
    blk = lambda b: pl.ds(pl.multiple_of(b * t, t), t)
    head = pl.program_id(0)

    def prep(c, carry):
        rows = blk(c)
        lane = lax.broadcasted_iota(I32, (t, LANES), 1)
        if has_bias:
            neg_c = -jnp.sum(jnp.where(lane == head, c_ref[rows, :], 0.0), axis=1, keepdims=True)
            hi, mid, lo = (x.astype(F32) for x in _split3(neg_c))
            aug = jnp.where(lane == 0, hi, jnp.where(lane == 1, mid, jnp.where(lane == 2, lo, 0.0)))
            kaug[rows, :dq] = k_ref[rows, :]
            kaug[rows, dq:] = aug.astype(BF16)
            ones = (lax.broadcasted_iota(I32, (LANES, t), 0) < 3).astype(F32)
            qt[:, rows] = jnp.concatenate([q_ref[rows, :].astype(F32).T, ones], axis=0).astype(BF16)
        else:
            qt[:, rows] = q_ref[rows, :].astype(F32).T.astype(BF16)
        one_row = (lax.broadcasted_iota(I32, (ONES_ROWS, t), 0) == 0).astype(F32)
        vt[:, rows] = jnp.concatenate([v_ref[rows, :].astype(F32).T, one_row], axis=0).astype(BF16)
        return carry

    lax.fori_loop(0, nq, prep, 0)

    def scores(qi, kb):
        return jnp.dot(kaug[blk(kb), :], qt[:, blk(qi)], preferred_element_type=F32)

    def values(kb, slot):
        return jnp.dot(vt[:, blk(kb)], p[slot][...], preferred_element_type=F32)

    def save_max(qi, m):
        m_all[qi] = jnp.broadcast_to(m, m_all.shape[1:])

    def diag_scores(i, slot):
        mask = lax.broadcasted_iota(I32, (t, t), 1) >= lax.broadcasted_iota(I32, (t, t), 0)
        s = jnp.where(mask, scores(i, i), NEG)
        st[slot][...] = s
        return jnp.max(s, axis=0, keepdims=True)

    def diag_softmax(i, slot, pslot, cmax, carry):
        p[pslot][...] = jnp.exp2(st[slot][...] - cmax).astype(BF16)
        save_max(i, cmax)
        return carry, None

    def diag_values(i, slot, _):
        acc_all[i] = values(i, slot)

    _pipeline3(nq, unroll, diag_scores, diag_softmax, diag_values, 0)

    n_low = nq * (nq - 1) // 2
    if n_low:
        last = n_low - 1
        tab = lambda ref, f: ref[jnp.minimum(f, last)]

        def low_scores(f, slot):
            s = scores(tab(qi_tab, f), tab(kb_tab, f))
            st[slot][...] = s
            return jnp.max(s, axis=0, keepdims=True)

        def low_softmax(f, slot, pslot, cmax, m):
            qi = tab(qi_tab, f)
            m_prev = jnp.where(tab(first_tab, f) == 1, m_all[qi][0:1], m)
            m_new = jnp.maximum(m_prev, cmax)
            p[pslot][...] = jnp.exp2(st[slot][...] - m_new).astype(BF16)
            return m_new, jnp.exp2(m_prev - m_new)

        def low_values(f, slot, a):
            qi = tab(qi_tab, f)
            acc_all[qi] = a * acc_all[qi] + values(tab(kb_tab, f), slot)

        _pipeline3(n_low, unroll, low_scores, low_softmax, low_values, jnp.full((1, t), NEG, F32))

    def finish(qi, carry):
        acc = acc_all[qi]
        o_ref[blk(qi), :] = (acc[:dv] / acc[dv:dv + 1]).T.astype(o_ref.dtype)
        return carry

    lax.fori_loop(0, nq, finish, 0)


def _attention(q_src, k_src, v_src, q_col, k_col, v_col, n_heads, dv, fcum=None):
    s = q_src.shape[0]
    t = min(ATT_TQ, s)
    nq = s // t
    pairs = [(qi, kb) for qi in range(nq) for kb in range(qi)] or [(0, 0)]
    qi_tab = jnp.array([a for a, _ in pairs], I32)
    kb_tab = jnp.array([b for _, b in pairs], I32)
    first_tab = jnp.array([int(b == 0) for _, b in pairs], I32)
    has_bias = fcum is not None
    dk = 2 * HEAD_DIM if has_bias else HEAD_DIM
    unroll = FOX_UNROLL if has_bias else DIFF_UNROLL
    in_specs = [pl.BlockSpec((s, HEAD_DIM), lambda h, *_: (0, q_col + h)),
                pl.BlockSpec((s, HEAD_DIM), lambda h, *_: (0, k_col + h)),
                pl.BlockSpec((s, dv), lambda h, *_: (0, v_col(h)))]
    args = [q_src, k_src, v_src]
    scratch = []
    if has_bias:
        in_specs.append(pl.BlockSpec((s, LANES), lambda h, *_: (0, 0)))
        args.append(fcum)
        scratch.append(pltpu.VMEM((s, dk), BF16))
    scratch += [pltpu.VMEM((dk, s), BF16), pltpu.VMEM((dv + ONES_ROWS, s), BF16),
                pltpu.VMEM((t, t), F32), pltpu.VMEM((t, t), F32), *[pltpu.VMEM((t, t), BF16)] * unroll,
                pltpu.VMEM((nq, dv + ONES_ROWS, t), F32), pltpu.VMEM((nq, SUBLANES, t), F32)]
    return pl.pallas_call(
        functools.partial(_attn_kernel, t=t, has_bias=has_bias, unroll=unroll),
        out_shape=jax.ShapeDtypeStruct((s, n_heads * dv), BF16),
        grid_spec=pltpu.PrefetchScalarGridSpec(
            num_scalar_prefetch=3, grid=(n_heads,), in_specs=in_specs,
            out_specs=pl.BlockSpec((s, dv), lambda h, *_: (0, h)), scratch_shapes=scratch),
        compiler_params=_params(("arbitrary",)),
        name="fox_attention" if has_bias else "diff_attention",
    )(qi_tab, kb_tab, first_tab, *args)


def _merge_kernel(om_ref, of_ref, gd_ref, gf_ref, lam_ref, g_ref, wd_ref, wf_ref, o_ref, *, lam_init):
    lam_v = lam_ref[...]
    lam = (jnp.exp(jnp.sum(lam_v[0:1] * lam_v[1:2], axis=1, keepdims=True))
           - jnp.exp(jnp.sum(lam_v[2:3] * lam_v[3:4], axis=1, keepdims=True)) + lam_init)
    heads = []
    for h in range(g_ref.shape[0]):
        o1 = om_ref[:, (2 * h) * DIFF_V_DIM:(2 * h + 1) * DIFF_V_DIM].astype(F32)
        o2 = om_ref[:, (2 * h + 1) * DIFF_V_DIM:(2 * h + 2) * DIFF_V_DIM].astype(F32)
        o = o1 - lam * o2
        o = o * lax.rsqrt(jnp.mean(o * o, axis=1, keepdims=True) + LN_EPS) * g_ref[h:h + 1, :] * (1.0 - lam_init)
        heads.append(o.astype(BF16))
    ud = jnp.dot(jnp.concatenate(heads, axis=1), wd_ref[...], preferred_element_type=F32)
    uf = jnp.dot(of_ref[...], wf_ref[...], preferred_element_type=F32)
    o_ref[...] = (gd_ref[...].astype(F32) * ud + gf_ref[...].astype(F32) * uf).astype(o_ref.dtype)


def _merge(o_maps, o_fox, gates, lam_rows, norm_g, wd, wf, lam_init):
    s, d = o_maps.shape[0], wd.shape[1]
    tm = min(MERGE_TM, s)
    row = lambda c: pl.BlockSpec((tm, c), lambda i: (i, 0))
    full = lambda a: pl.BlockSpec(a.shape, lambda i: (0, 0))
    return pl.pallas_call(
        functools.partial(_merge_kernel, lam_init=lam_init),
        out_shape=jax.ShapeDtypeStruct((s, d), BF16),
        grid=(s // tm,),
        in_specs=[row(o_maps.shape[1]), row(o_fox.shape[1]),
                  pl.BlockSpec((tm, d), lambda i: (i, 0)), pl.BlockSpec((tm, d), lambda i: (i, 1)),
                  full(lam_rows), full(norm_g), full(wd), full(wf)],
        out_specs=row(d),
        compiler_params=_params(("parallel",)),
        name="merge_branches",
    )(o_maps, o_fox, gates, gates, lam_rows, norm_g, wd, wf)


def _layer_norm(y, g, b):
    mu = jnp.mean(y, axis=1, keepdims=True)
    yc = y - mu
    var = jnp.mean(yc * yc, axis=1, keepdims=True)
    return yc * lax.rsqrt(var + LN_EPS) * g + b


def _route(lg):
    lane = lax.broadcasted_iota(I32, lg.shape, 1)
    far = jnp.int32(4 * LANES)
    is_g = lane < N_GROUPS
    gl = jnp.where(is_g, lg, NEG)
    gmax = jnp.max(gl, axis=1, keepdims=True)
    gidx = jnp.min(jnp.where(gl == gmax, lane, far), axis=1, keepdims=True)
    top_gp = 1.0 / jnp.sum(jnp.where(is_g, jnp.exp(gl - gmax), 0.0), axis=1, keepdims=True)
    lo = N_GROUPS + gidx * EXPERTS_PER_GROUP
    el = jnp.where((lane >= lo) & (lane < lo + EXPERTS_PER_GROUP), lg, NEG)
    m1 = jnp.max(el, axis=1, keepdims=True)
    i1 = jnp.min(jnp.where(el == m1, lane, far), axis=1, keepdims=True)
    el2 = jnp.where(lane == i1, 2.0 * NEG, el)
    m2 = jnp.max(el2, axis=1, keepdims=True)
    i2 = jnp.min(jnp.where(el2 == m2, lane, far), axis=1, keepdims=True)
    d = jnp.exp(m2 - m1)
    w1 = top_gp / (1.0 + d)
    w2 = w1 * d
    ids = jnp.where(lane == 0, i1 - N_GROUPS, jnp.where(lane == 1, i2 - N_GROUPS, 0))
    wts = jnp.where(lane == 0, w1, jnp.where(lane == 1, w2, 0.0))
    return ids, wts


def _outln_kernel(mg_ref, x_ref, wo_ref, g_ref, b_ref, wr_ref, br_ref, x1_ref, ids_ref, wts_ref, *, alpha):
    mix = jnp.dot(mg_ref[...], wo_ref[...], preferred_element_type=F32)
    x1 = _layer_norm(alpha * x_ref[...] + mix, g_ref[...], b_ref[...])
    x1_ref[...] = x1
    xh = x1.astype(BF16)
    xl = (x1 - xh.astype(F32)).astype(BF16)
    wr = wr_ref[...]
    a = jnp.dot(xh, wr, preferred_element_type=F32)
    b = jnp.dot(xl, wr[:, :LANES], preferred_element_type=F32)
    ids, wts = _route(a[:, :LANES] + a[:, LANES:] + b + br_ref[...])
    ids_ref[...] = ids
    wts_ref[...] = wts


def _out_ln_route(merged, x, w_out, g, b, wr2, br, alpha):
    s, d = x.shape
    tm = min(OUTLN_TM, s)
    row = lambda c: pl.BlockSpec((tm, c), lambda i: (i, 0))
    full = lambda a: pl.BlockSpec(a.shape, lambda i: (0, 0))
    return pl.pallas_call(
        functools.partial(_outln_kernel, alpha=alpha),
        out_shape=(jax.ShapeDtypeStruct((s, d), F32), jax.ShapeDtypeStruct((s, LANES), I32),
                   jax.ShapeDtypeStruct((s, LANES), F32)),
        grid=(s // tm,),
        in_specs=[row(d), row(d), full(w_out), full(g), full(b), full(wr2), full(br)],
        out_specs=(row(d), row(LANES), row(LANES)),
        compiler_params=_params(("parallel",)),
        name="out_proj_ln_route",
    )(merged, x, w_out, g, b, wr2, br)


def _rank_kernel(ids_ref, dest_ref, bexp_ref, meta_ref, cnt_ref, carry_ref, start_ref, *, block, n_blocks):
    ph, i = pl.program_id(0), pl.program_id(1)
    ids = ids_ref[...]
    tb = ids.shape[0]
    lane = lax.broadcasted_iota(I32, (tb, LANES), 1)
    oh1 = (lane == ids[:, 0:1]).astype(F32)
    oh2 = (lane == ids[:, 1:2]).astype(F32)
    oh = oh1 + oh2
    colsum = jnp.sum(oh, axis=0, keepdims=True)

    @pl.when((ph == 0) & (i == 0))
    def _():
        cnt_ref[...] = jnp.zeros_like(cnt_ref)

    @pl.when(ph == 0)
    def _():
        cnt_ref[...] += jnp.broadcast_to(colsum, cnt_ref.shape)

    @pl.when((ph == 1) & (i == 0))
    def _():
        cnt = cnt_ref[0:1, :]
        nblk = jnp.floor((cnt + (block - 1)) * (1.0 / block))
        r = lax.broadcasted_iota(I32, (LANES, LANES), 0)
        c = lax.broadcasted_iota(I32, (LANES, LANES), 1)
        upper = (r < c).astype(BF16)
        nb8 = jnp.broadcast_to(nblk, (8, LANES)).astype(BF16)
        bstart = jnp.dot(nb8, upper, preferred_element_type=F32)
        start_ref[...] = bstart * block
        carry_ref[...] = jnp.zeros_like(carry_ref)
        lane1 = lax.broadcasted_iota(I32, (1, LANES), 1)
        bend = jnp.where(lane1 < N_EXPERTS, bstart[0:1] + nblk, 4.0 * n_blocks)
        bidx = lax.broadcasted_iota(I32, (n_blocks, LANES), 0).astype(F32)
        be = jnp.sum((jnp.broadcast_to(bend, (n_blocks, LANES)) <= bidx).astype(F32), axis=1, keepdims=True)
        be = jnp.minimum(be, N_EXPERTS - 1.0)
        lane_b = lax.broadcasted_iota(I32, (n_blocks, LANES), 1).astype(F32)
        owns = jnp.broadcast_to((nblk > 0.0) & (lane1 < N_EXPERTS), (n_blocks, LANES))
        nxt = jnp.min(jnp.where(owns & (lane_b > be), lane_b, 1.0 * LANES), axis=1, keepdims=True)
        nxt = jnp.where(nxt >= 1.0 * LANES, -1.0, nxt)
        before = jnp.sum((owns & (lane_b < be)).astype(F32), axis=1, keepdims=True)
        parity = before - 2.0 * jnp.floor(before * 0.5)
        bexp_ref[...] = jnp.where(lane_b == 0.0, be, jnp.where(lane_b == 1.0, nxt, parity)).astype(I32)
        total = jnp.sum(jnp.where(lane1 < N_EXPERTS, nblk, 0.0), axis=1, keepdims=True)
        row = lax.broadcasted_iota(I32, (8, LANES), 0)
        first_pad = start_ref[0:1, :] + cnt
        meta = jnp.where(row == 0, jnp.broadcast_to(total, (8, LANES)),
                         jnp.where(row == 1, jnp.broadcast_to(first_pad, (8, LANES)),
                                   jnp.broadcast_to(bstart[0:1] * block + nblk * block, (8, LANES))))
        meta_ref[...] = meta.astype(I32)

    @pl.when(ph == 1)
    def _():
        rr = lax.broadcasted_iota(I32, (tb, tb), 0)
        cc = lax.broadcasted_iota(I32, (tb, tb), 1)
        lower = (cc < rr).astype(BF16)
        prefix = jnp.dot(lower, oh.astype(BF16), preferred_element_type=F32)
        pos = prefix + carry_ref[0:1, :] + start_ref[0:1, :]
        d1 = jnp.sum(pos * oh1, axis=1, keepdims=True)
        d2 = jnp.sum(pos * oh2, axis=1, keepdims=True)
        dest_ref[...] = jnp.where(lane == 0, d1, jnp.where(lane == 1, d2, 0.0)).astype(I32)
        carry_ref[...] += jnp.broadcast_to(colsum, carry_ref.shape)


def _rank(ids, block, n_blocks):
    t = ids.shape[0]
    tb = min(RANK_TB, t)
    return pl.pallas_call(
        functools.partial(_rank_kernel, block=block, n_blocks=n_blocks),
        out_shape=(jax.ShapeDtypeStruct((t, LANES), I32), jax.ShapeDtypeStruct((n_blocks, LANES), I32),
                   jax.ShapeDtypeStruct((8, LANES), I32)),
        grid=(2, t // tb),
        in_specs=[pl.BlockSpec((tb, LANES), lambda p, i: (i, 0))],
        out_specs=(pl.BlockSpec((tb, LANES), lambda p, i: (i * p, 0)),
                   pl.BlockSpec((n_blocks, LANES), lambda p, i: (0, 0)),
                   pl.BlockSpec((8, LANES), lambda p, i: (0, 0))),
        scratch_shapes=[pltpu.VMEM((8, LANES), F32)] * 3,
        compiler_params=_params(("arbitrary", "arbitrary")),
        name="moe_rank",
    )(ids)


def _invert_kernel(dest_ref, zeros_hbm, inv_hbm, inv_ref, sem):
    i = pl.program_id(0)
    n = dest_ref.shape[0]

    @pl.when(i == 0)
    def _():
        fill = pltpu.make_async_copy(zeros_hbm, inv_ref, sem)
        fill.start()
        fill.wait()

    def body(j, c):
        inv_ref[dest_ref[j]] = i * n + j
        return c

    lax.fori_loop(0, n, body, 0, unroll=8)

    @pl.when(i == pl.num_programs(0) - 1)
    def _():
        flush = pltpu.make_async_copy(inv_ref, inv_hbm, sem)
        flush.start()
        flush.wait()


def _invert(dest_flat, rows):
    a = dest_flat.shape[0]
    chunk = min(INVERT_CHUNK, a)
    return pl.pallas_call(
        _invert_kernel,
        out_shape=jax.ShapeDtypeStruct((rows,), I32),
        grid=(a // chunk,),
        in_specs=[pl.BlockSpec((chunk,), lambda i: (i,), memory_space=pltpu.SMEM),
                  pl.BlockSpec(memory_space=pl.ANY)],
        out_specs=pl.BlockSpec(memory_space=pl.ANY),
        scratch_shapes=[pltpu.SMEM((rows,), I32), pltpu.SemaphoreType.DMA],
        compiler_params=_params(("arbitrary",)),
        name="moe_invert",
    )(dest_flat, jnp.zeros((rows,), I32))


def _expert_kernel(tab_ref, nact_ref, inv0_ref, inv1_ref, inv2_ref, x1_ref, wg_hbm, wu_hbm, wd_hbm, ys_ref,
                   xbuf, wg_f, wu_f, wd_f, wg_b, wu_b, wd_b, gsem, wsem):
    b = pl.program_id(0)
    n_active = nact_ref[0]
    active = b < n_active
    block = xbuf.shape[1]
    expert, next_expert, wslot = tab_ref[0, b], tab_ref[1, b], tab_ref[2, b]
    first_of_expert = (b == 0) | (tab_ref[0, jnp.maximum(b - 1, 0)] != expert)

    def row_copy(token, slot, r):
        return pltpu.make_async_copy(x1_ref.at[pl.ds(token, 1)], xbuf.at[slot, pl.ds(r, 1)], gsem.at[slot])

    def gather(inv_ref, blk):
        for slot in range(GATHER_DEPTH):
            @pl.when(lax.rem(blk, GATHER_DEPTH) == slot)
            def _():
                for r0 in range(0, block, DMA_GROUP):
                    tokens = [lax.shift_right_logical(inv_ref[0, 0, r], 1) for r in range(r0, r0 + DMA_GROUP)]
                    for j, token in enumerate(tokens):
                        row_copy(token, slot, r0 + j).start()

    def weight_copies(e, s):
        return (pltpu.make_async_copy(wg_hbm.at[e], wg_f.at[s], wsem.at[s, 0]),
                pltpu.make_async_copy(wu_hbm.at[e], wu_f.at[s], wsem.at[s, 1]),
                pltpu.make_async_copy(wd_hbm.at[e], wd_f.at[s], wsem.at[s, 2]))

    @pl.when(b == 0)
    def _():
        for c in weight_copies(expert, wslot):
            c.start(priority=1)
        gather(inv0_ref, b)

    @pl.when((b == 0) & (n_active > 1))
    def _():
        gather(inv1_ref, b + 1)

    @pl.when(b + 2 < n_active)
    def _():
        gather(inv2_ref, b + 2)

    @pl.when(active & first_of_expert)
    def _():
        for c in weight_copies(expert, wslot):
            c.wait()

        @pl.when(next_expert >= 0)
        def _():
            for c in weight_copies(next_expert, 1 - wslot):
                c.start(priority=1)

        wg_b[...] = wg_f[wslot].astype(BF16)
        wu_b[...] = wu_f[wslot].astype(BF16)
        wd_b[...] = wd_f[wslot].astype(BF16)

    @pl.when(active)
    def _():
        for r in range(block):
            row_copy(0, lax.rem(b, GATHER_DEPTH), r).wait()
        x = xbuf[lax.rem(b, GATHER_DEPTH)].astype(BF16)
        g = jnp.dot(x, wg_b[...], preferred_element_type=F32)
        u = jnp.dot(x, wu_b[...], preferred_element_type=F32)
        h = (g / (1.0 + jnp.exp(-g)) * u).astype(BF16)
        ys_ref[...] = jnp.dot(h, wd_b[...], preferred_element_type=F32)

    @pl.when(jnp.logical_not(active))
    def _():
        ys_ref[...] = jnp.zeros_like(ys_ref)


def _experts(x1, inv, tab, nact, w_gate, w_up, w_down, block):
    n_blocks = tab.shape[1]
    d, ff = w_gate.shape[1], w_gate.shape[2]
    inv3 = inv.reshape(n_blocks, 1, block)
    ahead = lambda k: pl.BlockSpec((1, 1, block), lambda b, tb, na: (jnp.minimum(b + k, n_blocks - 1), 0, 0),
                                   memory_space=pltpu.SMEM)
    hbm = pl.BlockSpec(memory_space=pl.ANY)
    return pl.pallas_call(
        _expert_kernel,
        out_shape=jax.ShapeDtypeStruct((n_blocks * block, d), F32),
        grid_spec=pltpu.PrefetchScalarGridSpec(
            num_scalar_prefetch=2, grid=(n_blocks,),
            in_specs=[ahead(0), ahead(1), ahead(2), hbm, hbm, hbm, hbm],
            out_specs=pl.BlockSpec((block, d), lambda b, tb, na: (b, 0)),
            scratch_shapes=[pltpu.VMEM((GATHER_DEPTH, block, d), F32),
                            pltpu.VMEM((2, d, ff), F32), pltpu.VMEM((2, d, ff), F32), pltpu.VMEM((2, ff, d), F32),
                            pltpu.VMEM((d, ff), BF16), pltpu.VMEM((d, ff), BF16), pltpu.VMEM((ff, d), BF16),
                            pltpu.SemaphoreType.DMA((GATHER_DEPTH,)), pltpu.SemaphoreType.DMA((2, 3))]),
        compiler_params=_params(("arbitrary",)),
        name="moe_experts",
    )(tab, nact, inv3, inv3, inv3, x1, w_gate, w_up, w_down)


def _combine_kernel(dest_ref, next_ref, x1_ref, wts_ref, g_ref, b_ref, ys_ref, o_ref, buf, sem, *, alpha):
    i = pl.program_id(0)
    tm = x1_ref.shape[0]
    slot = lax.rem(i, 2)

    def row_copy(row, s, r, k):
        return pltpu.make_async_copy(ys_ref.at[pl.ds(row, 1)], buf.at[s, k, pl.ds(r, 1)], sem.at[s])

    def gather(idx_ref, dyn_slot):
        for s in range(2):
            @pl.when(dyn_slot == s)
            def _():
                for a0 in range(0, TOP_K * tm, DMA_GROUP):
                    rows = [idx_ref[0, 0, a] for a in range(a0, a0 + DMA_GROUP)]
                    for j, row in enumerate(rows):
                        r, k = divmod(a0 + j, TOP_K)
                        row_copy(row, s, r, k).start(priority=k % 2)

    @pl.when(i == 0)
    def _():
        gather(dest_ref, slot)

    @pl.when(i + 1 < pl.num_programs(0))
    def _():
        gather(next_ref, 1 - slot)

    for r in range(tm):
        for k in range(TOP_K):
            row_copy(0, slot, r, k).wait()
    w = wts_ref[...]
    ffn = w[:, 0:1] * buf[slot, 0] + w[:, 1:2] * buf[slot, 1]
    o_ref[...] = _layer_norm(alpha * x1_ref[...] + ffn, g_ref[...], b_ref[...])


def _combine(ys, dest3, x1, wts, g, b, alpha):
    t, d = x1.shape
    tm = dest3.shape[2] // TOP_K
    row = lambda c: pl.BlockSpec((tm, c), lambda i: (i, 0))
    full = lambda a: pl.BlockSpec(a.shape, lambda i: (0, 0))
    return pl.pallas_call(
        functools.partial(_combine_kernel, alpha=alpha),
        out_shape=jax.ShapeDtypeStruct((t, d), F32),
        grid=(t // tm,),
        in_specs=[pl.BlockSpec((1, 1, TOP_K * tm), lambda i: (i, 0, 0), memory_space=pltpu.SMEM),
                  pl.BlockSpec((1, 1, TOP_K * tm), lambda i: (jnp.minimum(i + 1, t // tm - 1), 0, 0),
                               memory_space=pltpu.SMEM),
                  row(d), row(LANES), full(g), full(b), pl.BlockSpec(memory_space=pl.ANY)],
        out_specs=row(d),
        scratch_shapes=[pltpu.VMEM((2, TOP_K, tm, d), F32), pltpu.SemaphoreType.DMA((2,))],
        compiler_params=_params(("arbitrary",)),
        name="moe_combine_ln",
    )(dest3, dest3, x1, wts, g, b, ys)


def _rope_tables(s):
    inv_freq = ROPE_THETA ** (-jnp.arange(0, HEAD_DIM, 2, dtype=F32) / HEAD_DIM)
    ang = jnp.arange(s, dtype=F32)[:, None] * inv_freq[None, :]
    cos, sin = jnp.cos(ang), jnp.sin(ang)
    return jnp.concatenate([cos, cos], axis=1), jnp.concatenate([-sin, sin], axis=1)


def _split_bf16_pair(w, width):
    hi = w.astype(BF16)
    lo = (w - hi.astype(F32)).astype(BF16)
    pad = lambda a: jnp.pad(a, ((0, 0), (0, width - a.shape[1])))
    return jnp.concatenate([pad(hi), pad(lo)], axis=1)


def _layer(x2, layer, depth, w_in, b_forget, lam_q1, lam_k1, lam_q2, lam_k2, diff_norm_g, w_proj_diff,
           w_proj_fox, w_out, ln1_g, ln1_b, w_rg, b_rg, w_re, b_re, w_gate, w_up, w_down, ln2_g, ln2_b):
    s, d = x2.shape
    alpha = (2 * depth) ** 0.25
    lam_init = 0.8 - 0.6 * math.exp(-0.3 * layer)
    n_diff, n_fox = diff_norm_g.shape[0], b_forget.shape[0]
    qk_cols = n_diff * 2 * HEAD_DIM
    lin_cols = n_diff * DIFF_V_DIM + 3 * n_fox * HEAD_DIM
    q_scale = HEAD_DIM ** -0.5 * LOG2E

    f0 = 2 * qk_cols + lin_cols
    w_t = w_in.T
    fcum, x_bf = _forget_cumsum(x2, w_t[f0:f0 + n_fox].T, b_forget)

    cos_t, sin_t = _rope_tables(s)
    tm = min(PROJ_TM, s)
    tab = pl.BlockSpec((tm, HEAD_DIM), lambda j, i: (i, 0))
    assert qk_cols == PROJ_TN, "column block 0 of the rotary call must be exactly the queries"
    qk = _proj(x_bf, w_t, 0, 2 * qk_cols, "rope", (cos_t, sin_t), (tab, tab), q_scale)
    col_scale = jnp.ones((1, lin_cols), F32).at[:, n_diff * DIFF_V_DIM:n_diff * DIFF_V_DIM + n_fox * HEAD_DIM].set(q_scale)
    lin = _proj(x_bf, w_t, 2 * qk_cols, lin_cols, "scale", (col_scale,),
                (pl.BlockSpec((1, min(PROJ_TN, lin_cols)), lambda j, i: (0, j)),))
    gates = _proj(x_bf, w_t, f0 + n_fox, w_in.shape[1] - f0 - n_fox, "sigmoid")

    lam_rows = jnp.stack([lam_q1, lam_k1, lam_q2, lam_k2]).astype(F32)
    n_maps = 2 * n_diff
    o_maps = _attention(qk, qk, lin, 0, n_maps, lambda h: h // 2, n_maps, DIFF_V_DIM)
    fq_col = n_diff * DIFF_V_DIM // HEAD_DIM
    o_fox = _attention(lin, lin, lin, fq_col, fq_col + n_fox, lambda h: fq_col + 2 * n_fox + h, n_fox, HEAD_DIM, fcum)

    merged = _merge(o_maps, o_fox, gates, lam_rows, diff_norm_g.astype(F32), w_proj_diff.astype(BF16),
                    w_proj_fox.astype(BF16), lam_init)
    w_router = jnp.concatenate([w_rg, jnp.moveaxis(w_re, 0, 1).reshape(d, N_EXPERTS)], axis=1)
    b_router = jnp.pad(jnp.concatenate([b_rg, b_re.reshape(N_EXPERTS)]).astype(F32),
                       (0, LANES - N_GROUPS - N_EXPERTS)).reshape(1, LANES)
    x1, ids, wts = _out_ln_route(merged, x2, w_out.astype(BF16), ln1_g.reshape(1, d), ln1_b.reshape(1, d),
                                 _split_bf16_pair(w_router, LANES), b_router, alpha)

    block = MOE_BLOCK
    n_blocks = -(-(s * TOP_K) // block) + N_EXPERTS
    dest, bexp, meta = _rank(ids, block, n_blocks)
    tmr = min(ROW_TM, s)
    dest2 = dest[:, :TOP_K]
    inv = _invert(dest2.reshape(s * TOP_K), n_blocks * block)
    ys = _experts(x1, inv, bexp[:, :3].T, meta[0, :1], w_gate, w_up, w_down, block)
    dest3 = dest2.reshape(s // tmr, 1, TOP_K * tmr)
    return _combine(ys, dest3, x1, wts, ln2_g.reshape(1, d), ln2_b.reshape(1, d), alpha)


def kernel(x, w_in, b_forget, lam_q1, lam_k1, lam_q2, lam_k2, diff_norm_g, w_proj_diff, w_proj_fox, w_out,
           ln1_g, ln1_b, w_router_group, b_router_group, w_router_expert, b_router_expert, w_gate, w_up,
           w_down, ln2_g, ln2_b):
    batch, s, d = x.shape
    depth = w_in.shape[0]
    params = (w_in, b_forget, lam_q1, lam_k1, lam_q2, lam_k2, diff_norm_g, w_proj_diff, w_proj_fox, w_out,
              ln1_g, ln1_b, w_router_group, b_router_group, w_router_expert, b_router_expert, w_gate, w_up,
              w_down, ln2_g, ln2_b)
    outs = []
    for bi in range(batch):
        h = x[bi]
        for layer in range(depth):
            h = _layer(h, layer, depth, *(p[layer] for p in params))
        outs.append(h)
    return jnp.stack(outs)
```

```python
import functools
import math

import jax
import jax.numpy as jnp
from jax import lax
from jax.experimental import pallas as pl
from jax.experimental.pallas import tpu as pltpu

F32, BF16, I32 = jnp.float32, jnp.bfloat16, jnp.int32

HEAD_DIM = 128
DIFF_V_DIM = 2 * HEAD_DIM
ROPE_THETA = 10000.0
N_GROUPS = 4
EXPERTS_PER_GROUP = 8
N_EXPERTS = N_GROUPS * EXPERTS_PER_GROUP
TOP_K = 2
LN_EPS = 1e-5
LOG2E = 1.4426950408889634
NEG = -1e30
LANES = 128
SUBLANES = 8
VMEM_LIMIT = 56 * 1024 * 1024

MOE_BLOCK = 128
PROJ_TM, PROJ_TN = 1024, 1024
PROJ_CAST_ROWS = 256
ATT_TQ = 512
ONES_ROWS = 16
FOX_UNROLL, DIFF_UNROLL = 16, 8
FORGET_TS = 512
MERGE_TM = 512
OUTLN_TM = 512
RANK_TB = 512
ROW_TM = 256
INVERT_CHUNK = 1024
GATHER_DEPTH = 3
DMA_GROUP = 16


def _params(sem, vmem=VMEM_LIMIT):
    return pltpu.CompilerParams(dimension_semantics=sem, vmem_limit_bytes=vmem)


def _proj_kernel(x_ref, wt_hbm, *rest, epilogue, row0, q_scale):
    *extra, o_ref, w_f32, w_bf, sem = rest
    j, i = pl.program_id(0), pl.program_id(1)
    tn = w_bf.shape[0]
    slot = lax.rem(j, 2)

    def fetch(jj, s):
        rows = pl.ds(pl.multiple_of(row0 + jj * tn, SUBLANES), tn)
        return pltpu.make_async_copy(wt_hbm.at[rows, :], w_f32.at[s], sem.at[s])

    @pl.when(i == 0)
    def _():
        @pl.when(j == 0)
        def _():
            fetch(j, slot).start()

        fetch(j, slot).wait()

        @pl.when(j + 1 < pl.num_programs(0))
        def _():
            fetch(j + 1, 1 - slot).start()

        rows = min(PROJ_CAST_ROWS, tn)

        def chunk(c, carry):
            r = pl.ds(pl.multiple_of(c * rows, rows), rows)
            w_bf[r, :] = w_f32[slot, r, :].astype(BF16)
            return carry

        lax.fori_loop(0, tn // rows, chunk, 0)

    acc = lax.dot_general(x_ref[...], w_bf[...], (((1,), (1,)), ((), ())), preferred_element_type=F32)
    if epilogue == "rope":
        cos_ref, sin_ref = extra
        scale = jnp.where(j == 0, q_scale, 1.0)
        cosf, sinf = cos_ref[...] * scale, sin_ref[...] * scale
        for h in range(tn // HEAD_DIM):
            t = acc[:, h * HEAD_DIM:(h + 1) * HEAD_DIM]
            o_ref[:, h * HEAD_DIM:(h + 1) * HEAD_DIM] = (
                t * cosf + pltpu.roll(t, HEAD_DIM // 2, 1) * sinf).astype(o_ref.dtype)
    elif epilogue == "scale":
        o_ref[...] = (acc * extra[0][...]).astype(o_ref.dtype)
    else:
        o_ref[...] = (1.0 / (1.0 + jnp.exp(-acc))).astype(o_ref.dtype)


def _proj(x_bf, w_t, col0, n, epilogue, extra=(), extra_specs=(), q_scale=1.0):
    m, k = x_bf.shape
    tm, tn = min(PROJ_TM, m), min(PROJ_TN, n)
    assert col0 % SUBLANES == 0 and n % tn == 0
    return pl.pallas_call(
        functools.partial(_proj_kernel, epilogue=epilogue, row0=col0, q_scale=q_scale),
        out_shape=jax.ShapeDtypeStruct((m, n), BF16),
        grid=(n // tn, m // tm),
        in_specs=[pl.BlockSpec((tm, k), lambda j, i: (i, 0)), pl.BlockSpec(memory_space=pl.ANY), *extra_specs],
        out_specs=pl.BlockSpec((tm, tn), lambda j, i: (i, j)),
        scratch_shapes=[pltpu.VMEM((2, tn, k), F32), pltpu.VMEM((tn, k), BF16), pltpu.SemaphoreType.DMA((2,))],
        compiler_params=_params(("arbitrary", "arbitrary")),
        name=f"in_proj_{epilogue}",
    )(x_bf, w_t, *extra)


def _split3(v):
    h = v.astype(BF16)
    r = v - h.astype(F32)
    m = r.astype(BF16)
    return h, m, (r - m.astype(F32)).astype(BF16)


def _forget_kernel(x_ref, w_ref, b_ref, o_ref, xbf_ref, carry_ref):
    i = pl.program_id(0)

    @pl.when(i == 0)
    def _():
        carry_ref[...] = jnp.zeros_like(carry_ref)

    x = x_ref[...]
    xh = x.astype(BF16)
    xbf_ref[...] = xh
    xl = (x - xh.astype(F32)).astype(BF16)
    w = w_ref[...]
    a = jnp.dot(xh, w, preferred_element_type=F32)
    b = jnp.dot(xl, w[:, :LANES], preferred_element_type=F32)
    z = a[:, :LANES] + a[:, LANES:] + b + b_ref[...]
    logf = jnp.minimum(z, 0.0) - jnp.log1p(jnp.exp(-jnp.abs(z)))
    ts = logf.shape[0]
    tri = (lax.broadcasted_iota(I32, (ts, ts), 1) <= lax.broadcasted_iota(I32, (ts, ts), 0)).astype(BF16)
    c = carry_ref[0:1, :]
    for piece in _split3(logf):
        c = c + jnp.dot(tri, piece, preferred_element_type=F32)
    o_ref[...] = c * LOG2E
    carry_ref[...] = jnp.broadcast_to(c[ts - 1:ts, :], carry_ref.shape)


def _forget_cumsum(x, w_f, b_f):
    s, d = x.shape
    nh = w_f.shape[1]
    wh = w_f.astype(BF16)
    wl = (w_f - wh.astype(F32)).astype(BF16)
    pad = lambda a: jnp.pad(a, ((0, 0), (0, LANES - nh)))
    w2 = jnp.concatenate([pad(wh), pad(wl)], axis=1)
    b2 = jnp.pad(b_f.astype(F32), (0, LANES - nh)).reshape(1, LANES)
    ts = min(FORGET_TS, s)
    return pl.pallas_call(
        _forget_kernel,
        out_shape=(jax.ShapeDtypeStruct((s, LANES), F32), jax.ShapeDtypeStruct((s, d), BF16)),
        grid=(s // ts,),
        in_specs=[pl.BlockSpec((ts, d), lambda i: (i, 0)),
                  pl.BlockSpec((d, 2 * LANES), lambda i: (0, 0)),
                  pl.BlockSpec((1, LANES), lambda i: (0, 0))],
        out_specs=(pl.BlockSpec((ts, LANES), lambda i: (i, 0)), pl.BlockSpec((ts, d), lambda i: (i, 0))),
        scratch_shapes=[pltpu.VMEM((8, LANES), F32)],
        compiler_params=_params(("arbitrary",)),
        name="forget_cumsum",
    )(x, w2, b2)


def _pipeline3(n, unroll, stage_a, stage_b, stage_c, carry):
    if n < 3:
        for i in range(n):
            carry, post = stage_b(i, i % 2, i % unroll, stage_a(i, i % 2), carry)
            stage_c(i, i % unroll, post)
        return carry

    def step(i, m, state, do_a=True):
        aux, post_old, post_new, carry = state
        stage_c(i - 2, (m - 2) % unroll, post_old)
        aux_next = stage_a(i + 1, (m + 1) % 2) if do_a else aux
        carry, post = stage_b(i, m % 2, m, aux, carry)
        return aux_next, post_new, post, carry

    aux0 = stage_a(0, 0)
    aux1 = stage_a(1, 1)
    carry, post0 = stage_b(0, 0, 0, aux0, carry)
    aux2 = stage_a(2, 0)
    carry, post1 = stage_b(1, 1, 1, aux1, carry)
    state = (aux2, post0, post1, carry)
    mid = n - 3

    def unrolled(h, s):
        for u in range(unroll):
            s = step(unroll * h + 2 + u, (2 + u) % unroll, s)
        return s

    state = lax.fori_loop(0, mid // unroll, unrolled, state)
    for i in range(mid - mid % unroll + 2, n - 1):
        state = step(i, i % unroll, state)
    _, post_old, post_new, carry = step(n - 1, (n - 1) % unroll, state, do_a=False)
    stage_c(n - 2, (n - 2) % unroll, post_old)
    stage_c(n - 1, (n - 1) % unroll, post_new)
    return carry


def _attn_kernel(qi_tab, kb_tab, first_tab, q_ref, k_ref, v_ref, *rest, t, has_bias, unroll):
    if has_bias:
        c_ref, o_ref, kaug, qt, vt, st0, st1, *p, acc_all, m_all = rest
    else:
        o_ref, qt, vt, st0, st1, *p, acc_all, m_all = rest
        kaug = k_ref
    st = (st0, st1)
    seq, dv = v_ref.shape
    nq = seq // t
    dq = q_ref.shape[1]
    blk = lambda b: pl.ds(pl.multiple_of(b * t, t), t)
    head = pl.program_id(0)

    def prep(c, carry):
        rows = blk(c)
        lane = lax.broadcasted_iota(I32, (t, LANES), 1)
        if has_bias:
            neg_c = -jnp.sum(jnp.where(lane == head, c_ref[rows, :], 0.0), axis=1, keepdims=True)
            hi, mid, lo = (x.astype(F32) for x in _split3(neg_c))
            aug = jnp.where(lane == 0, hi, jnp.where(lane == 1, mid, jnp.where(lane == 2, lo, 0.0)))
            kaug[rows, :dq] = k_ref[rows, :]
            kaug[rows, dq:] = aug.astype(BF16)
            ones = (lax.broadcasted_iota(I32, (LANES, t), 0) < 3).astype(F32)
            qt[:, rows] = jnp.concatenate([q_ref[rows, :].astype(F32).T, ones], axis=0).astype(BF16)
        else:
            qt[:, rows] = q_ref[rows, :].astype(F32).T.astype(BF16)
        one_row = (lax.broadcasted_iota(I32, (ONES_ROWS, t), 0) == 0).astype(F32)
        vt[:, rows] = jnp.concatenate([v_ref[rows, :].astype(F32).T, one_row], axis=0).astype(BF16)
        return carry

    lax.fori_loop(0, nq, prep, 0)

    def scores(qi, kb):
        return jnp.dot(kaug[blk(kb), :], qt[:, blk(qi)], preferred_element_type=F32)

    def values(kb, slot):
        return jnp.dot(vt[:, blk(kb)], p[slot][...], preferred_element_type=F32)

    def save_max(qi, m):
        m_all[qi] = jnp.broadcast_to(m, m_all.shape[1:])

    def diag_scores(i, slot):
        mask = lax.broadcasted_iota(I32, (t, t), 1) >= lax.broadcasted_iota(I32, (t, t), 0)
        s = jnp.where(mask, scores(i, i), NEG)
        st[slot][...] = s
        return jnp.max(s, axis=0, keepdims=True)

    def diag_softmax(i, slot, pslot, cmax, carry):
        p[pslot][...] = jnp.exp2(st[slot][...] - cmax).astype(BF16)
        save_max(i, cmax)
        return carry, None

    def diag_values(i, slot, _):
        acc_all[i] = values(i, slot)

    _pipeline3(nq, unroll, diag_scores, diag_softmax, diag_values, 0)

    n_low = nq * (nq - 1) // 2
    if n_low:
        last = n_low - 1
        tab = lambda ref, f: ref[jnp.minimum(f, last)]

        def low_scores(f, slot):
            s = scores(tab(qi_tab, f), tab(kb_tab, f))
            st[slot][...] = s
            return jnp.max(s, axis=0, keepdims=True)

        def low_softmax(f, slot, pslot, cmax, m):
            qi = tab(qi_tab, f)
            m_prev = jnp.where(tab(first_tab, f) == 1, m_all[qi][0:1], m)
            m_new = jnp.maximum(m_prev, cmax)
            p[pslot][...] = jnp.exp2(st[slot][...] - m_new).astype(BF16)
            return m_new, jnp.exp2(m_prev - m_new)

        def low_values(f, slot, a):
            qi = tab(qi_tab, f)
            acc_all[qi] = a * acc_all[qi] + values(tab(kb_tab, f), slot)

        _pipeline3(n_low, unroll, low_scores, low_softmax, low_values, jnp.full((1, t), NEG, F32))

    def finish(qi, carry):
        acc = acc_all[qi]
        o_ref[blk(qi), :] = (acc[:dv] / acc[dv:dv + 1]).T.astype(o_ref.dtype)
        return carry

    lax.fori_loop(0, nq, finish, 0)


def _attention(q_src, k_src, v_src, q_col, k_col, v_col, n_heads, dv, fcum=None):
    s = q_src.shape[0]
    t = min(ATT_TQ, s)
    nq = s // t
    pairs = [(qi, kb) for qi in range(nq) for kb in range(qi)] or [(0, 0)]
    qi_tab = jnp.array([a for a, _ in pairs], I32)
    kb_tab = jnp.array([b for _, b in pairs], I32)
    first_tab = jnp.array([int(b == 0) for _, b in pairs], I32)
    has_bias = fcum is not None
    dk = 2 * HEAD_DIM if has_bias else HEAD_DIM
    unroll = FOX_UNROLL if has_bias else DIFF_UNROLL
    in_specs = [pl.BlockSpec((s, HEAD_DIM), lambda h, *_: (0, q_col + h)),
                pl.BlockSpec((s, HEAD_DIM), lambda h, *_: (0, k_col + h)),
                pl.BlockSpec((s, dv), lambda h, *_: (0, v_col(h)))]
    args = [q_src, k_src, v_src]
    scratch = []
    if has_bias:
        in_specs.append(pl.BlockSpec((s, LANES), lambda h, *_: (0, 0)))
        args.append(fcum)
        scratch.append(pltpu.VMEM((s, dk), BF16))
    scratch += [pltpu.VMEM((dk, s), BF16), pltpu.VMEM((dv + ONES_ROWS, s), BF16),
                pltpu.VMEM((t, t), F32), pltpu.VMEM((t, t), F32), *[pltpu.VMEM((t, t), BF16)] * unroll,
                pltpu.VMEM((nq, dv + ONES_ROWS, t), F32), pltpu.VMEM((nq, SUBLANES, t), F32)]
    return pl.pallas_call(
        functools.partial(_attn_kernel, t=t, has_bias=has_bias, unroll=unroll),
        out_shape=jax.ShapeDtypeStruct((s, n_heads * dv), BF16),
        grid_spec=pltpu.PrefetchScalarGridSpec(
            num_scalar_prefetch=3, grid=(n_heads,), in_specs=in_specs,
            out_specs=pl.BlockSpec((s, dv), lambda h, *_: (0, h)), scratch_shapes=scratch),
        compiler_params=_params(("arbitrary",)),
        name="fox_attention" if has_bias else "diff_attention",
    )(qi_tab, kb_tab, first_tab, *args)


def _merge_kernel(om_ref, of_ref, gd_ref, gf_ref, lam_ref, g_ref, wd_ref, wf_ref, o_ref, *, lam_init):
    lam_v = lam_ref[...]
    lam = (jnp.exp(jnp.sum(lam_v[0:1] * lam_v[1:2], axis=1, keepdims=True))
           - jnp.exp(jnp.sum(lam_v[2:3] * lam_v[3:4], axis=1, keepdims=True)) + lam_init)
    heads = []
    for h in range(g_ref.shape[0]):
        o1 = om_ref[:, (2 * h) * DIFF_V_DIM:(2 * h + 1) * DIFF_V_DIM].astype(F32)
        o2 = om_ref[:, (2 * h + 1) * DIFF_V_DIM:(2 * h + 2) * DIFF_V_DIM].astype(F32)
        o = o1 - lam * o2
        o = o * lax.rsqrt(jnp.mean(o * o, axis=1, keepdims=True) + LN_EPS) * g_ref[h:h + 1, :] * (1.0 - lam_init)
        heads.append(o.astype(BF16))
    ud = jnp.dot(jnp.concatenate(heads, axis=1), wd_ref[...], preferred_element_type=F32)
    uf = jnp.dot(of_ref[...], wf_ref[...], preferred_element_type=F32)
    o_ref[...] = (gd_ref[...].astype(F32) * ud + gf_ref[...].astype(F32) * uf).astype(o_ref.dtype)


def _merge(o_maps, o_fox, gates, lam_rows, norm_g, wd, wf, lam_init):
    s, d = o_maps.shape[0], wd.shape[1]
    tm = min(MERGE_TM, s)
    row = lambda c: pl.BlockSpec((tm, c), lambda i: (i, 0))
    full = lambda a: pl.BlockSpec(a.shape, lambda i: (0, 0))
    return pl.pallas_call(
        functools.partial(_merge_kernel, lam_init=lam_init),
        out_shape=jax.ShapeDtypeStruct((s, d), BF16),
        grid=(s // tm,),
        in_specs=[row(o_maps.shape[1]), row(o_fox.shape[1]),
                  pl.BlockSpec((tm, d), lambda i: (i, 0)), pl.BlockSpec((tm, d), lambda i: (i, 1)),
                  full(lam_rows), full(norm_g), full(wd), full(wf)],
        out_specs=row(d),
        compiler_params=_params(("parallel",)),
        name="merge_branches",
    )(o_maps, o_fox, gates, gates, lam_rows, norm_g, wd, wf)


def _layer_norm(y, g, b):
    mu = jnp.mean(y, axis=1, keepdims=True)
    yc = y - mu
    var = jnp.mean(yc * yc, axis=1, keepdims=True)
    return yc * lax.rsqrt(var + LN_EPS) * g + b


def _route(lg):
    lane = lax.broadcasted_iota(I32, lg.shape, 1)
    far = jnp.int32(4 * LANES)
    is_g = lane < N_GROUPS
    gl = jnp.where(is_g, lg, NEG)
    gmax = jnp.max(gl, axis=1, keepdims=True)
    gidx = jnp.min(jnp.where(gl == gmax, lane, far), axis=1, keepdims=True)
    top_gp = 1.0 / jnp.sum(jnp.where(is_g, jnp.exp(gl - gmax), 0.0), axis=1, keepdims=True)
    lo = N_GROUPS + gidx * EXPERTS_PER_GROUP
    el = jnp.where((lane >= lo) & (lane < lo + EXPERTS_PER_GROUP), lg, NEG)
    m1 = jnp.max(el, axis=1, keepdims=True)
    i1 = jnp.min(jnp.where(el == m1, lane, far), axis=1, keepdims=True)
    el2 = jnp.where(lane == i1, 2.0 * NEG, el)
    m2 = jnp.max(el2, axis=1, keepdims=True)
    i2 = jnp.min(jnp.where(el2 == m2, lane, far), axis=1, keepdims=True)
    d = jnp.exp(m2 - m1)
    w1 = top_gp / (1.0 + d)
    w2 = w1 * d
    ids = jnp.where(lane == 0, i1 - N_GROUPS, jnp.where(lane == 1, i2 - N_GROUPS, 0))
    wts = jnp.where(lane == 0, w1, jnp.where(lane == 1, w2, 0.0))
    return ids, wts


def _outln_kernel(mg_ref, x_ref, wo_ref, g_ref, b_ref, wr_ref, br_ref, x1_ref, ids_ref, wts_ref, *, alpha):
    mix = jnp.dot(mg_ref[...], wo_ref[...], preferred_element_type=F32)
    x1 = _layer_norm(alpha * x_ref[...] + mix, g_ref[...], b_ref[...])
    x1_ref[...] = x1
    xh = x1.astype(BF16)
    xl = (x1 - xh.astype(F32)).astype(BF16)
    wr = wr_ref[...]
    a = jnp.dot(xh, wr, preferred_element_type=F32)
    b = jnp.dot(xl, wr[:, :LANES], preferred_element_type=F32)
    ids, wts = _route(a[:, :LANES] + a[:, LANES:] + b + br_ref[...])
    ids_ref[...] = ids
    wts_ref[...] = wts


def _out_ln_route(merged, x, w_out, g, b, wr2, br, alpha):
    s, d = x.shape
    tm = min(OUTLN_TM, s)
    row = lambda c: pl.BlockSpec((tm, c), lambda i: (i, 0))
    full = lambda a: pl.BlockSpec(a.shape, lambda i: (0, 0))
    return pl.pallas_call(
        functools.partial(_outln_kernel, alpha=alpha),
        out_shape=(jax.ShapeDtypeStruct((s, d), F32), jax.ShapeDtypeStruct((s, LANES), I32),
                   jax.ShapeDtypeStruct((s, LANES), F32)),
        grid=(s // tm,),
        in_specs=[row(d), row(d), full(w_out), full(g), full(b), full(wr2), full(br)],
        out_specs=(row(d), row(LANES), row(LANES)),
        compiler_params=_params(("parallel",)),
        name="out_proj_ln_route",
    )(merged, x, w_out, g, b, wr2, br)


def _rank_kernel(ids_ref, dest_ref, bexp_ref, meta_ref, cnt_ref, carry_ref, start_ref, *, block, n_blocks):
    ph, i = pl.program_id(0), pl.program_id(1)
    ids = ids_ref[...]
    tb = ids.shape[0]
    lane = lax.broadcasted_iota(I32, (tb, LANES), 1)
    oh1 = (lane == ids[:, 0:1]).astype(F32)
    oh2 = (lane == ids[:, 1:2]).astype(F32)
    oh = oh1 + oh2
    colsum = jnp.sum(oh, axis=0, keepdims=True)

    @pl.when((ph == 0) & (i == 0))
    def _():
        cnt_ref[...] = jnp.zeros_like(cnt_ref)

    @pl.when(ph == 0)
    def _():
        cnt_ref[...] += jnp.broadcast_to(colsum, cnt_ref.shape)

    @pl.when((ph == 1) & (i == 0))
    def _():
        cnt = cnt_ref[0:1, :]
        nblk = jnp.floor((cnt + (block - 1)) * (1.0 / block))
        r = lax.broadcasted_iota(I32, (LANES, LANES), 0)
        c = lax.broadcasted_iota(I32, (LANES, LANES), 1)
        upper = (r < c).astype(BF16)
        nb8 = jnp.broadcast_to(nblk, (8, LANES)).astype(BF16)
        bstart = jnp.dot(nb8, upper, preferred_element_type=F32)
        start_ref[...] = bstart * block
        carry_ref[...] = jnp.zeros_like(carry_ref)
        lane1 = lax.broadcasted_iota(I32, (1, LANES), 1)
        bend = jnp.where(lane1 < N_EXPERTS, bstart[0:1] + nblk, 4.0 * n_blocks)
        bidx = lax.broadcasted_iota(I32, (n_blocks, LANES), 0).astype(F32)
        be = jnp.sum((jnp.broadcast_to(bend, (n_blocks, LANES)) <= bidx).astype(F32), axis=1, keepdims=True)
        be = jnp.minimum(be, N_EXPERTS - 1.0)
        lane_b = lax.broadcasted_iota(I32, (n_blocks, LANES), 1).astype(F32)
        owns = jnp.broadcast_to((nblk > 0.0) & (lane1 < N_EXPERTS), (n_blocks, LANES))
        nxt = jnp.min(jnp.where(owns & (lane_b > be), lane_b, 1.0 * LANES), axis=1, keepdims=True)
        nxt = jnp.where(nxt >= 1.0 * LANES, -1.0, nxt)
        before = jnp.sum((owns & (lane_b < be)).astype(F32), axis=1, keepdims=True)
        parity = before - 2.0 * jnp.floor(before * 0.5)
        bexp_ref[...] = jnp.where(lane_b == 0.0, be, jnp.where(lane_b == 1.0, nxt, parity)).astype(I32)
        total = jnp.sum(jnp.where(lane1 < N_EXPERTS, nblk, 0.0), axis=1, keepdims=True)
        row = lax.broadcasted_iota(I32, (8, LANES), 0)
        first_pad = start_ref[0:1, :] + cnt
        meta = jnp.where(row == 0, jnp.broadcast_to(total, (8, LANES)),
                         jnp.where(row == 1, jnp.broadcast_to(first_pad, (8, LANES)),
                                   jnp.broadcast_to(bstart[0:1] * block + nblk * block, (8, LANES))))
        meta_ref[...] = meta.astype(I32)

    @pl.when(ph == 1)
    def _():
        rr = lax.broadcasted_iota(I32, (tb, tb), 0)
        cc = lax.broadcasted_iota(I32, (tb, tb), 1)
        lower = (cc < rr).astype(BF16)
        prefix = jnp.dot(lower, oh.astype(BF16), preferred_element_type=F32)
        pos = prefix + carry_ref[0:1, :] + start_ref[0:1, :]
        d1 = jnp.sum(pos * oh1, axis=1, keepdims=True)
        d2 = jnp.sum(pos * oh2, axis=1, keepdims=True)
        dest_ref[...] = jnp.where(lane == 0, d1, jnp.where(lane == 1, d2, 0.0)).astype(I32)
        carry_ref[...] += jnp.broadcast_to(colsum, carry_ref.shape)


def _rank(ids, block, n_blocks):
    t = ids.shape[0]
    tb = min(RANK_TB, t)
    return pl.pallas_call(
        functools.partial(_rank_kernel, block=block, n_blocks=n_blocks),
        out_shape=(jax.ShapeDtypeStruct((t, LANES), I32), jax.ShapeDtypeStruct((n_blocks, LANES), I32),
                   jax.ShapeDtypeStruct((8, LANES), I32)),
        grid=(2, t // tb),
        in_specs=[pl.BlockSpec((tb, LANES), lambda p, i: (i, 0))],
        out_specs=(pl.BlockSpec((tb, LANES), lambda p, i: (i * p, 0)),
                   pl.BlockSpec((n_blocks, LANES), lambda p, i: (0, 0)),
                   pl.BlockSpec((8, LANES), lambda p, i: (0, 0))),
        scratch_shapes=[pltpu.VMEM((8, LANES), F32)] * 3,
        compiler_params=_params(("arbitrary", "arbitrary")),
        name="moe_rank",
    )(ids)


def _invert_kernel(dest_ref, zeros_hbm, inv_hbm, inv_ref, sem):
    i = pl.program_id(0)
    n = dest_ref.shape[0]

    @pl.when(i == 0)
    def _():
        fill = pltpu.make_async_copy(zeros_hbm, inv_ref, sem)
        fill.start()
        fill.wait()

    def body(j, c):
        inv_ref[dest_ref[j]] = i * n + j
        return c

    lax.fori_loop(0, n, body, 0, unroll=8)

    @pl.when(i == pl.num_programs(0) - 1)
    def _():
        flush = pltpu.make_async_copy(inv_ref, inv_hbm, sem)
        flush.start()
        flush.wait()


def _invert(dest_flat, rows):
    a = dest_flat.shape[0]
    chunk = min(INVERT_CHUNK, a)
    return pl.pallas_call(
        _invert_kernel,
        out_shape=jax.ShapeDtypeStruct((rows,), I32),
        grid=(a // chunk,),
        in_specs=[pl.BlockSpec((chunk,), lambda i: (i,), memory_space=pltpu.SMEM),
                  pl.BlockSpec(memory_space=pl.ANY)],
        out_specs=pl.BlockSpec(memory_space=pl.ANY),
        scratch_shapes=[pltpu.SMEM((rows,), I32), pltpu.SemaphoreType.DMA],
        compiler_params=_params(("arbitrary",)),
        name="moe_invert",
    )(dest_flat, jnp.zeros((rows,), I32))


def _expert_kernel(tab_ref, nact_ref, inv0_ref, inv1_ref, inv2_ref, x1_ref, wg_hbm, wu_hbm, wd_hbm, ys_ref,
                   xbuf, wg_f, wu_f, wd_f, wg_b, wu_b, wd_b, gsem, wsem):
    b = pl.program_id(0)
    n_active = nact_ref[0]
    active = b < n_active
    block = xbuf.shape[1]
    expert, next_expert, wslot = tab_ref[0, b], tab_ref[1, b], tab_ref[2, b]
    first_of_expert = (b == 0) | (tab_ref[0, jnp.maximum(b - 1, 0)] != expert)

    def row_copy(token, slot, r):
        return pltpu.make_async_copy(x1_ref.at[pl.ds(token, 1)], xbuf.at[slot, pl.ds(r, 1)], gsem.at[slot])

    def gather(inv_ref, blk):
        for slot in range(GATHER_DEPTH):
            @pl.when(lax.rem(blk, GATHER_DEPTH) == slot)
            def _():
                for r0 in range(0, block, DMA_GROUP):
                    tokens = [lax.shift_right_logical(inv_ref[0, 0, r], 1) for r in range(r0, r0 + DMA_GROUP)]
                    for j, token in enumerate(tokens):
                        row_copy(token, slot, r0 + j).start()

    def weight_copies(e, s):
        return (pltpu.make_async_copy(wg_hbm.at[e], wg_f.at[s], wsem.at[s, 0]),
                pltpu.make_async_copy(wu_hbm.at[e], wu_f.at[s], wsem.at[s, 1]),
                pltpu.make_async_copy(wd_hbm.at[e], wd_f.at[s], wsem.at[s, 2]))

    @pl.when(b == 0)
    def _():
        for c in weight_copies(expert, wslot):
            c.start(priority=1)
        gather(inv0_ref, b)

    @pl.when((b == 0) & (n_active > 1))
    def _():
        gather(inv1_ref, b + 1)

    @pl.when(b + 2 < n_active)
    def _():
        gather(inv2_ref, b + 2)

    @pl.when(active & first_of_expert)
    def _():
        for c in weight_copies(expert, wslot):
            c.wait()

        @pl.when(next_expert >= 0)
        def _():
            for c in weight_copies(next_expert, 1 - wslot):
                c.start(priority=1)

        wg_b[...] = wg_f[wslot].astype(BF16)
        wu_b[...] = wu_f[wslot].astype(BF16)
        wd_b[...] = wd_f[wslot].astype(BF16)

    @pl.when(active)
    def _():
        for r in range(block):
            row_copy(0, lax.rem(b, GATHER_DEPTH), r).wait()
        x = xbuf[lax.rem(b, GATHER_DEPTH)].astype(BF16)
        g = jnp.dot(x, wg_b[...], preferred_element_type=F32)
        u = jnp.dot(x, wu_b[...], preferred_element_type=F32)
        h = (g / (1.0 + jnp.exp(-g)) * u).astype(BF16)
        ys_ref[...] = jnp.dot(h, wd_b[...], preferred_element_type=F32)

    @pl.when(jnp.logical_not(active))
    def _():
        ys_ref[...] = jnp.zeros_like(ys_ref)


def _experts(x1, inv, tab, nact, w_gate, w_up, w_down, block):
    n_blocks = tab.shape[1]
    d, ff = w_gate.shape[1], w_gate.shape[2]
    inv3 = inv.reshape(n_blocks, 1, block)
    ahead = lambda k: pl.BlockSpec((1, 1, block), lambda b, tb, na: (jnp.minimum(b + k, n_blocks - 1), 0, 0),
                                   memory_space=pltpu.SMEM)
    hbm = pl.BlockSpec(memory_space=pl.ANY)
    return pl.pallas_call(
        _expert_kernel,
        out_shape=jax.ShapeDtypeStruct((n_blocks * block, d), F32),
        grid_spec=pltpu.PrefetchScalarGridSpec(
            num_scalar_prefetch=2, grid=(n_blocks,),
            in_specs=[ahead(0), ahead(1), ahead(2), hbm, hbm, hbm, hbm],
            out_specs=pl.BlockSpec((block, d), lambda b, tb, na: (b, 0)),
            scratch_shapes=[pltpu.VMEM((GATHER_DEPTH, block, d), F32),
                            pltpu.VMEM((2, d, ff), F32), pltpu.VMEM((2, d, ff), F32), pltpu.VMEM((2, ff, d), F32),
                            pltpu.VMEM((d, ff), BF16), pltpu.VMEM((d, ff), BF16), pltpu.VMEM((ff, d), BF16),
                            pltpu.SemaphoreType.DMA((GATHER_DEPTH,)), pltpu.SemaphoreType.DMA((2, 3))]),
        compiler_params=_params(("arbitrary",)),
        name="moe_experts",
    )(tab, nact, inv3, inv3, inv3, x1, w_gate, w_up, w_down)


def _combine_kernel(dest_ref, next_ref, x1_ref, wts_ref, g_ref, b_ref, ys_ref, o_ref, buf, sem, *, alpha):
    i = pl.program_id(0)
    tm = x1_ref.shape[0]
    slot = lax.rem(i, 2)

    def row_copy(row, s, r, k):
        return pltpu.make_async_copy(ys_ref.at[pl.ds(row, 1)], buf.at[s, k, pl.ds(r, 1)], sem.at[s])

    def gather(idx_ref, dyn_slot):
        for s in range(2):
            @pl.when(dyn_slot == s)
            def _():
                for a0 in range(0, TOP_K * tm, DMA_GROUP):
                    rows = [idx_ref[0, 0, a] for a in range(a0, a0 + DMA_GROUP)]
                    for j, row in enumerate(rows):
                        r, k = divmod(a0 + j, TOP_K)
                        row_copy(row, s, r, k).start(priority=k % 2)

    @pl.when(i == 0)
    def _():
        gather(dest_ref, slot)

    @pl.when(i + 1 < pl.num_programs(0))
    def _():
        gather(next_ref, 1 - slot)

    for r in range(tm):
        for k in range(TOP_K):
            row_copy(0, slot, r, k).wait()
    w = wts_ref[...]
    ffn = w[:, 0:1] * buf[slot, 0] + w[:, 1:2] * buf[slot, 1]
    o_ref[...] = _layer_norm(alpha * x1_ref[...] + ffn, g_ref[...], b_ref[...])


def _combine(ys, dest3, x1, wts, g, b, alpha):
    t, d = x1.shape
    tm = dest3.shape[2] // TOP_K
    row = lambda c: pl.BlockSpec((tm, c), lambda i: (i, 0))
    full = lambda a: pl.BlockSpec(a.shape, lambda i: (0, 0))
    return pl.pallas_call(
        functools.partial(_combine_kernel, alpha=alpha),
        out_shape=jax.ShapeDtypeStruct((t, d), F32),
        grid=(t // tm,),
        in_specs=[pl.BlockSpec((1, 1, TOP_K * tm), lambda i: (i, 0, 0), memory_space=pltpu.SMEM),
                  pl.BlockSpec((1, 1, TOP_K * tm), lambda i: (jnp.minimum(i + 1, t // tm - 1), 0, 0),
                               memory_space=pltpu.SMEM),
                  row(d), row(LANES), full(g), full(b), pl.BlockSpec(memory_space=pl.ANY)],
        out_specs=row(d),
        scratch_shapes=[pltpu.VMEM((2, TOP_K, tm, d), F32), pltpu.SemaphoreType.DMA((2,))],
        compiler_params=_params(("arbitrary",)),
        name="moe_combine_ln",
    )(dest3, dest3, x1, wts, g, b, ys)


def _rope_tables(s):
    inv_freq = ROPE_THETA ** (-jnp.arange(0, HEAD_DIM, 2, dtype=F32) / HEAD_DIM)
    ang = jnp.arange(s, dtype=F32)[:, None] * inv_freq[None, :]
    cos, sin = jnp.cos(ang), jnp.sin(ang)
    return jnp.concatenate([cos, cos], axis=1), jnp.concatenate([-sin, sin], axis=1)


def _split_bf16_pair(w, width):
    hi = w.astype(BF16)
    lo = (w - hi.astype(F32)).astype(BF16)
    pad = lambda a: jnp.pad(a, ((0, 0), (0, width - a.shape[1])))
    return jnp.concatenate([pad(hi), pad(lo)], axis=1)


def _layer(x2, layer, depth, w_in, b_forget, lam_q1, lam_k1, lam_q2, lam_k2, diff_norm_g, w_proj_diff,
           w_proj_fox, w_out, ln1_g, ln1_b, w_rg, b_rg, w_re, b_re, w_gate, w_up, w_down, ln2_g, ln2_b):
    s, d = x2.shape
    alpha = (2 * depth) ** 0.25
    lam_init = 0.8 - 0.6 * math.exp(-0.3 * layer)
    n_diff, n_fox = diff_norm_g.shape[0], b_forget.shape[0]
    qk_cols = n_diff * 2 * HEAD_DIM
    lin_cols = n_diff * DIFF_V_DIM + 3 * n_fox * HEAD_DIM
    q_scale = HEAD_DIM ** -0.5 * LOG2E

    f0 = 2 * qk_cols + lin_cols
    w_t = w_in.T
    fcum, x_bf = _forget_cumsum(x2, w_t[f0:f0 + n_fox].T, b_forget)

    cos_t, sin_t = _rope_tables(s)
    tm = min(PROJ_TM, s)
    tab = pl.BlockSpec((tm, HEAD_DIM), lambda j, i: (i, 0))
    assert qk_cols == PROJ_TN, "column block 0 of the rotary call must be exactly the queries"
    qk = _proj(x_bf, w_t, 0, 2 * qk_cols, "rope", (cos_t, sin_t), (tab, tab), q_scale)
    col_scale = jnp.ones((1, lin_cols), F32).at[:, n_diff * DIFF_V_DIM:n_diff * DIFF_V_DIM + n_fox * HEAD_DIM].set(q_scale)
    lin = _proj(x_bf, w_t, 2 * qk_cols, lin_cols, "scale", (col_scale,),
                (pl.BlockSpec((1, min(PROJ_TN, lin_cols)), lambda j, i: (0, j)),))
    gates = _proj(x_bf, w_t, f0 + n_fox, w_in.shape[1] - f0 - n_fox, "sigmoid")

    lam_rows = jnp.stack([lam_q1, lam_k1, lam_q2, lam_k2]).astype(F32)
    n_maps = 2 * n_diff
    o_maps = _attention(qk, qk, lin, 0, n_maps, lambda h: h // 2, n_maps, DIFF_V_DIM)
    fq_col = n_diff * DIFF_V_DIM // HEAD_DIM
    o_fox = _attention(lin, lin, lin, fq_col, fq_col + n_fox, lambda h: fq_col + 2 * n_fox + h, n_fox, HEAD_DIM, fcum)

    merged = _merge(o_maps, o_fox, gates, lam_rows, diff_norm_g.astype(F32), w_proj_diff.astype(BF16),
                    w_proj_fox.astype(BF16), lam_init)
    w_router = jnp.concatenate([w_rg, jnp.moveaxis(w_re, 0, 1).reshape(d, N_EXPERTS)], axis=1)
    b_router = jnp.pad(jnp.concatenate([b_rg, b_re.reshape(N_EXPERTS)]).astype(F32),
                       (0, LANES - N_GROUPS - N_EXPERTS)).reshape(1, LANES)
    x1, ids, wts = _out_ln_route(merged, x2, w_out.astype(BF16), ln1_g.reshape(1, d), ln1_b.reshape(1, d),
                                 _split_bf16_pair(w_router, LANES), b_router, alpha)

    block = MOE_BLOCK
    n_blocks = -(-(s * TOP_K) // block) + N_EXPERTS
    dest, bexp, meta = _rank(ids, block, n_blocks)
    tmr = min(ROW_TM, s)
    dest2 = dest[:, :TOP_K]
    inv = _invert(dest2.reshape(s * TOP_K), n_blocks * block)
    ys = _experts(x1, inv, bexp[:, :3].T, meta[0, :1], w_gate, w_up, w_down, block)
    dest3 = dest2.reshape(s // tmr, 1, TOP_K * tmr)
    return _combine(ys, dest3, x1, wts, ln2_g.reshape(1, d), ln2_b.reshape(1, d), alpha)


def kernel(x, w_in, b_forget, lam_q1, lam_k1, lam_q2, lam_k2, diff_norm_g, w_proj_diff, w_proj_fox, w_out,
           ln1_g, ln1_b, w_router_group, b_router_group, w_router_expert, b_router_expert, w_gate, w_up,
           w_down, ln2_g, ln2_b):
    batch, s, d = x.shape
    depth = w_in.shape[0]
    params = (w_in, b_forget, lam_q1, lam_k1, lam_q2, lam_k2, diff_norm_g, w_proj_diff, w_proj_fox, w_out,
              ln1_g, ln1_b, w_router_group, b_router_group, w_router_expert, b_router_expert, w_gate, w_up,
              w_down, ln2_g, ln2_b)
    outs = []
    for bi in range(batch):
        h = x[bi]
        for layer in range(depth):
            h = _layer(h, layer, depth, *(p[layer] for p in params))
        outs.append(h)
    return jnp.stack(outs)
```

```python
import functools
import math

import jax
import jax.numpy as jnp
from jax import lax
from jax.experimental import pallas as pl
from jax.experimental.pallas import tpu as pltpu

F32, BF16, I32 = jnp.float32, jnp.bfloat16, jnp.int32

HEAD_DIM = 128
DIFF_V_DIM = 2 * HEAD_DIM
ROPE_THETA = 10000.0
N_GROUPS = 4
EXPERTS_PER_GROUP = 8
N_EXPERTS = N_GROUPS * EXPERTS_PER_GROUP
TOP_K = 2
LN_EPS = 1e-5
LOG2E = 1.4426950408889634
NEG = -1e30
LANES = 128
SUBLANES = 8
VMEM_LIMIT = 56 * 1024 * 1024

MOE_BLOCK = 128
PROJ_TM, PROJ_TN = 1024, 1024
PROJ_CAST_ROWS = 256
ATT_TQ = 512
ONES_ROWS = 16
FOX_UNROLL, DIFF_UNROLL = 16, 8
FORGET_TS = 512
MERGE_TM = 512
OUTLN_TM = 512
RANK_TB = 1024
ROW_TM = 256
INVERT_CHUNK = 1024
GATHER_DEPTH = 3
DMA_GROUP = 16


def _params(sem, vmem=VMEM_LIMIT):
    return pltpu.CompilerParams(dimension_semantics=sem, vmem_limit_bytes=vmem)


def _proj_kernel(x_ref, wt_hbm, *rest, epilogue, row0, q_scale):
    *extra, o_ref, w_f32, w_bf, sem = rest
    j, i = pl.program_id(0), pl.program_id(1)
    tn = w_bf.shape[0]
    slot = lax.rem(j, 2)

    def fetch(jj, s):
        rows = pl.ds(pl.multiple_of(row0 + jj * tn, SUBLANES), tn)
        return pltpu.make_async_copy(wt_hbm.at[rows, :], w_f32.at[s], sem.at[s])

    @pl.when(i == 0)
    def _():
        @pl.when(j == 0)
        def _():
            fetch(j, slot).start()

        fetch(j, slot).wait()

        @pl.when(j + 1 < pl.num_programs(0))
        def _():
            fetch(j + 1, 1 - slot).start()

        rows = min(PROJ_CAST_ROWS, tn)

        def chunk(c, carry):
            r = pl.ds(pl.multiple_of(c * rows, rows), rows)
            w_bf[r, :] = w_f32[slot, r, :].astype(BF16)
            return carry

        lax.fori_loop(0, tn // rows, chunk, 0)

    acc = lax.dot_general(x_ref[...], w_bf[...], (((1,), (1,)), ((), ())), preferred_element_type=F32)
    if epilogue == "rope":
        cos_ref, sin_ref = extra
        scale = jnp.where(j == 0, q_scale, 1.0)
        cosf, sinf = cos_ref[...] * scale, sin_ref[...] * scale
        for h in range(tn // HEAD_DIM):
            t = acc[:, h * HEAD_DIM:(h + 1) * HEAD_DIM]
            o_ref[:, h * HEAD_DIM:(h + 1) * HEAD_DIM] = (
                t * cosf + pltpu.roll(t, HEAD_DIM // 2, 1) * sinf).astype(o_ref.dtype)
    elif epilogue == "scale":
        o_ref[...] = (acc * extra[0][...]).astype(o_ref.dtype)
    else:
        o_ref[...] = (1.0 / (1.0 + jnp.exp(-acc))).astype(o_ref.dtype)


def _proj(x_bf, w_t, col0, n, epilogue, extra=(), extra_specs=(), q_scale=1.0):
    m, k = x_bf.shape
    tm, tn = min(PROJ_TM, m), min(PROJ_TN, n)
    assert col0 % SUBLANES == 0 and n % tn == 0
    return pl.pallas_call(
        functools.partial(_proj_kernel, epilogue=epilogue, row0=col0, q_scale=q_scale),
        out_shape=jax.ShapeDtypeStruct((m, n), BF16),
        grid=(n // tn, m // tm),
        in_specs=[pl.BlockSpec((tm, k), lambda j, i: (i, 0)), pl.BlockSpec(memory_space=pl.ANY), *extra_specs],
        out_specs=pl.BlockSpec((tm, tn), lambda j, i: (i, j)),
        scratch_shapes=[pltpu.VMEM((2, tn, k), F32), pltpu.VMEM((tn, k), BF16), pltpu.SemaphoreType.DMA((2,))],
        compiler_params=_params(("arbitrary", "arbitrary")),
        name=f"in_proj_{epilogue}",
    )(x_bf, w_t, *extra)


def _split3(v):
    h = v.astype(BF16)
    r = v - h.astype(F32)
    m = r.astype(BF16)
    return h, m, (r - m.astype(F32)).astype(BF16)


def _forget_kernel(x_ref, w_ref, b_ref, o_ref, xbf_ref, carry_ref):
    i = pl.program_id(0)

    @pl.when(i == 0)
    def _():
        carry_ref[...] = jnp.zeros_like(carry_ref)

    x = x_ref[...]
    xh = x.astype(BF16)
    xbf_ref[...] = xh
    xl = (x - xh.astype(F32)).astype(BF16)
    w = w_ref[...]
    a = jnp.dot(xh, w, preferred_element_type=F32)
    b = jnp.dot(xl, w[:, :LANES], preferred_element_type=F32)
    z = a[:, :LANES] + a[:, LANES:] + b + b_ref[...]
    logf = jnp.minimum(z, 0.0) - jnp.log1p(jnp.exp(-jnp.abs(z)))
    ts = logf.shape[0]
    tri = (lax.broadcasted_iota(I32, (ts, ts), 1) <= lax.broadcasted_iota(I32, (ts, ts), 0)).astype(BF16)
    c = carry_ref[0:1, :]
    for piece in _split3(logf):
        c = c + jnp.dot(tri, piece, preferred_element_type=F32)
    o_ref[...] = c * LOG2E
    carry_ref[...] = jnp.broadcast_to(c[ts - 1:ts, :], carry_ref.shape)


def _forget_cumsum(x, w_f, b_f):
    s, d = x.shape
    nh = w_f.shape[1]
    wh = w_f.astype(BF16)
    wl = (w_f - wh.astype(F32)).astype(BF16)
    pad = lambda a: jnp.pad(a, ((0, 0), (0, LANES - nh)))
    w2 = jnp.concatenate([pad(wh), pad(wl)], axis=1)
    b2 = jnp.pad(b_f.astype(F32), (0, LANES - nh)).reshape(1, LANES)
    ts = min(FORGET_TS, s)
    return pl.pallas_call(
        _forget_kernel,
        out_shape=(jax.ShapeDtypeStruct((s, LANES), F32), jax.ShapeDtypeStruct((s, d), BF16)),
        grid=(s // ts,),
        in_specs=[pl.BlockSpec((ts, d), lambda i: (i, 0)),
                  pl.BlockSpec((d, 2 * LANES), lambda i: (0, 0)),
                  pl.BlockSpec((1, LANES), lambda i: (0, 0))],
        out_specs=(pl.BlockSpec((ts, LANES), lambda i: (i, 0)), pl.BlockSpec((ts, d), lambda i: (i, 0))),
        scratch_shapes=[pltpu.VMEM((8, LANES), F32)],
        compiler_params=_params(("arbitrary",)),
        name="forget_cumsum",
    )(x, w2, b2)


def _pipeline3(n, unroll, stage_a, stage_b, stage_c, carry):
    if n < 3:
        for i in range(n):
            carry, post = stage_b(i, i % 2, i % unroll, stage_a(i, i % 2), carry)
            stage_c(i, i % unroll, post)
        return carry

    def step(i, m, state, do_a=True):
        aux, post_old, post_new, carry = state
        stage_c(i - 2, (m - 2) % unroll, post_old)
        aux_next = stage_a(i + 1, (m + 1) % 2) if do_a else aux
        carry, post = stage_b(i, m % 2, m, aux, carry)
        return aux_next, post_new, post, carry

    aux0 = stage_a(0, 0)
    aux1 = stage_a(1, 1)
    carry, post0 = stage_b(0, 0, 0, aux0, carry)
    aux2 = stage_a(2, 0)
    carry, post1 = stage_b(1, 1, 1, aux1, carry)
    state = (aux2, post0, post1, carry)
    mid = n - 3

    def unrolled(h, s):
        for u in range(unroll):
            s = step(unroll * h + 2 + u, (2 + u) % unroll, s)
        return s

    state = lax.fori_loop(0, mid // unroll, unrolled, state)
    for i in range(mid - mid % unroll + 2, n - 1):
        state = step(i, i % unroll, state)
    _, post_old, post_new, carry = step(n - 1, (n - 1) % unroll, state, do_a=False)
    stage_c(n - 2, (n - 2) % unroll, post_old)
    stage_c(n - 1, (n - 1) % unroll, post_new)
    return carry


def _attn_kernel(qi_tab, kb_tab, first_tab, q_ref, k_ref, v_ref, *rest, t, has_bias, unroll):
    if has_bias:
        c_ref, o_ref, kaug, qt, vt, st0, st1, *p, acc_all, m_all = rest
    else:
        o_ref, qt, vt, st0, st1, *p, acc_all, m_all = rest
        kaug = k_ref
    st = (st0, st1)
    seq, dv = v_ref.shape
    nq = seq // t
    dq = q_ref.shape[1]
    blk = lambda b: pl.ds(pl.multiple_of(b * t, t), t)
    head = pl.program_id(0)

    def prep(c, carry):
        rows = blk(c)
        lane = lax.broadcasted_iota(I32, (t, LANES), 1)
        if has_bias:
            neg_c = -jnp.sum(jnp.where(lane == head, c_ref[rows, :], 0.0), axis=1, keepdims=True)
            hi, mid, lo = (x.astype(F32) for x in _split3(neg_c))
            aug = jnp.where(lane == 0, hi, jnp.where(lane == 1, mid, jnp.where(lane == 2, lo, 0.0)))
            kaug[rows, :dq] = k_ref[rows, :]
            kaug[rows, dq:] = aug.astype(BF16)
            ones = (lax.broadcasted_iota(I32, (LANES, t), 0) < 3).astype(F32)
            qt[:, rows] = jnp.concatenate([q_ref[rows, :].T, ones.astype(BF16)], axis=0)
        else:
            qt[:, rows] = q_ref[rows, :].T
        one_row = (lax.broadcasted_iota(I32, (ONES_ROWS, t), 0) == 0).astype(F32)
        vt[:, rows] = jnp.concatenate([v_ref[rows, :].T, one_row.astype(BF16)], axis=0)
        return carry

    lax.fori_loop(0, nq, prep, 0)

    def scores(qi, kb):
        return jnp.dot(kaug[blk(kb), :], qt[:, blk(qi)], preferred_element_type=F32)

    def values(kb, slot):
        return jnp.dot(vt[:, blk(kb)], p[slot][...], preferred_element_type=F32)

    def save_max(qi, m):
        m_all[qi] = jnp.broadcast_to(m, m_all.shape[1:])

    def diag_scores(i, slot):
        mask = lax.broadcasted_iota(I32, (t, t), 1) >= lax.broadcasted_iota(I32, (t, t), 0)
        s = jnp.where(mask, scores(i, i), NEG)
        st[slot][...] = s
        return jnp.max(s, axis=0, keepdims=True)

    def diag_softmax(i, slot, pslot, cmax, carry):
        p[pslot][...] = jnp.exp2(st[slot][...] - cmax).astype(BF16)
        save_max(i, cmax)
        return carry, None

    def diag_values(i, slot, _):
        acc_all[i] = values(i, slot)

    _pipeline3(nq, unroll, diag_scores, diag_softmax, diag_values, 0)

    n_low = nq * (nq - 1) // 2
    if n_low:
        last = n_low - 1
        tab = lambda ref, f: ref[jnp.minimum(f, last)]

        def low_scores(f, slot):
            s = scores(tab(qi_tab, f), tab(kb_tab, f))
            st[slot][...] = s
            return jnp.max(s, axis=0, keepdims=True)

        def low_softmax(f, slot, pslot, cmax, m):
            qi = tab(qi_tab, f)
            m_prev = jnp.where(tab(first_tab, f) == 1, m_all[qi][0:1], m)
            m_new = jnp.maximum(m_prev, cmax)
            p[pslot][...] = jnp.exp2(st[slot][...] - m_new).astype(BF16)
            return m_new, jnp.exp2(m_prev - m_new)

        def low_values(f, slot, a):
            qi = tab(qi_tab, f)
            acc_all[qi] = a * acc_all[qi] + values(tab(kb_tab, f), slot)

        _pipeline3(n_low, unroll, low_scores, low_softmax, low_values, jnp.full((1, t), NEG, F32))

    def finish(qi, carry):
        acc = acc_all[qi]
        o_ref[blk(qi), :] = (acc[:dv] / acc[dv:dv + 1]).T.astype(o_ref.dtype)
        return carry

    lax.fori_loop(0, nq, finish, 0)


def _attention(q_src, k_src, v_src, q_col, k_col, v_col, n_heads, dv, fcum=None):
    s = q_src.shape[0]
    t = min(ATT_TQ, s)
    nq = s // t
    pairs = [(qi, kb) for qi in range(nq) for kb in range(qi)] or [(0, 0)]
    qi_tab = jnp.array([a for a, _ in pairs], I32)
    kb_tab = jnp.array([b for _, b in pairs], I32)
    first_tab = jnp.array([int(b == 0) for _, b in pairs], I32)
    has_bias = fcum is not None
    dk = 2 * HEAD_DIM if has_bias else HEAD_DIM
    unroll = FOX_UNROLL if has_bias else DIFF_UNROLL
    in_specs = [pl.BlockSpec((s, HEAD_DIM), lambda h, *_: (0, q_col + h)),
                pl.BlockSpec((s, HEAD_DIM), lambda h, *_: (0, k_col + h)),
                pl.BlockSpec((s, dv), lambda h, *_: (0, v_col(h)))]
    args = [q_src, k_src, v_src]
    scratch = []
    if has_bias:
        in_specs.append(pl.BlockSpec((s, LANES), lambda h, *_: (0, 0)))
        args.append(fcum)
        scratch.append(pltpu.VMEM((s, dk), BF16))
    scratch += [pltpu.VMEM((dk, s), BF16), pltpu.VMEM((dv + ONES_ROWS, s), BF16),
                pltpu.VMEM((t, t), F32), pltpu.VMEM((t, t), F32), *[pltpu.VMEM((t, t), BF16)] * unroll,
                pltpu.VMEM((nq, dv + ONES_ROWS, t), F32), pltpu.VMEM((nq, SUBLANES, t), F32)]
    return pl.pallas_call(
        functools.partial(_attn_kernel, t=t, has_bias=has_bias, unroll=unroll),
        out_shape=jax.ShapeDtypeStruct((s, n_heads * dv), BF16),
        grid_spec=pltpu.PrefetchScalarGridSpec(
            num_scalar_prefetch=3, grid=(n_heads,), in_specs=in_specs,
            out_specs=pl.BlockSpec((s, dv), lambda h, *_: (0, h)), scratch_shapes=scratch),
        compiler_params=_params(("arbitrary",)),
        name="fox_attention" if has_bias else "diff_attention",
    )(qi_tab, kb_tab, first_tab, *args)


def _merge_kernel(om_ref, of_ref, gd_ref, gf_ref, lam_ref, g_ref, wd_ref, wf_ref, o_ref, *, lam_init):
    lam_v = lam_ref[...]
    lam = (jnp.exp(jnp.sum(lam_v[0:1] * lam_v[1:2], axis=1, keepdims=True))
           - jnp.exp(jnp.sum(lam_v[2:3] * lam_v[3:4], axis=1, keepdims=True)) + lam_init)
    heads = []
    for h in range(g_ref.shape[0]):
        o1 = om_ref[:, (2 * h) * DIFF_V_DIM:(2 * h + 1) * DIFF_V_DIM].astype(F32)
        o2 = om_ref[:, (2 * h + 1) * DIFF_V_DIM:(2 * h + 2) * DIFF_V_DIM].astype(F32)
        o = o1 - lam * o2
        o = o * lax.rsqrt(jnp.mean(o * o, axis=1, keepdims=True) + LN_EPS) * g_ref[h:h + 1, :] * (1.0 - lam_init)
        heads.append(o.astype(BF16))
    ud = jnp.dot(jnp.concatenate(heads, axis=1), wd_ref[...], preferred_element_type=F32)
    uf = jnp.dot(of_ref[...], wf_ref[...], preferred_element_type=F32)
    o_ref[...] = (gd_ref[...].astype(F32) * ud + gf_ref[...].astype(F32) * uf).astype(o_ref.dtype)


def _merge(o_maps, o_fox, gates, lam_rows, norm_g, wd, wf, lam_init):
    s, d = o_maps.shape[0], wd.shape[1]
    tm = min(MERGE_TM, s)
    row = lambda c: pl.BlockSpec((tm, c), lambda i: (i, 0))
    full = lambda a: pl.BlockSpec(a.shape, lambda i: (0, 0))
    return pl.pallas_call(
        functools.partial(_merge_kernel, lam_init=lam_init),
        out_shape=jax.ShapeDtypeStruct((s, d), BF16),
        grid=(s // tm,),
        in_specs=[row(o_maps.shape[1]), row(o_fox.shape[1]),
                  pl.BlockSpec((tm, d), lambda i: (i, 0)), pl.BlockSpec((tm, d), lambda i: (i, 1)),
                  full(lam_rows), full(norm_g), full(wd), full(wf)],
        out_specs=row(d),
        compiler_params=_params(("parallel",)),
        name="merge_branches",
    )(o_maps, o_fox, gates, gates, lam_rows, norm_g, wd, wf)


def _layer_norm(y, g, b):
    mu = jnp.mean(y, axis=1, keepdims=True)
    yc = y - mu
    var = jnp.mean(yc * yc, axis=1, keepdims=True)
    return yc * lax.rsqrt(var + LN_EPS) * g + b


def _route(lg):
    lane = lax.broadcasted_iota(I32, lg.shape, 1)
    far = jnp.int32(4 * LANES)
    is_g = lane < N_GROUPS
    gl = jnp.where(is_g, lg, NEG)
    gmax = jnp.max(gl, axis=1, keepdims=True)
    gidx = jnp.min(jnp.where(gl == gmax, lane, far), axis=1, keepdims=True)
    top_gp = 1.0 / jnp.sum(jnp.where(is_g, jnp.exp(gl - gmax), 0.0), axis=1, keepdims=True)
    lo = N_GROUPS + gidx * EXPERTS_PER_GROUP
    el = jnp.where((lane >= lo) & (lane < lo + EXPERTS_PER_GROUP), lg, NEG)
    m1 = jnp.max(el, axis=1, keepdims=True)
    i1 = jnp.min(jnp.where(el == m1, lane, far), axis=1, keepdims=True)
    el2 = jnp.where(lane == i1, 2.0 * NEG, el)
    m2 = jnp.max(el2, axis=1, keepdims=True)
    i2 = jnp.min(jnp.where(el2 == m2, lane, far), axis=1, keepdims=True)
    d = jnp.exp(m2 - m1)
    w1 = top_gp / (1.0 + d)
    w2 = w1 * d
    ids = jnp.where(lane == 0, i1 - N_GROUPS, jnp.where(lane == 1, i2 - N_GROUPS, 0))
    wts = jnp.where(lane == 0, w1, jnp.where(lane == 1, w2, 0.0))
    return ids, wts


def _outln_kernel(mg_ref, x_ref, wo_ref, g_ref, b_ref, wr_ref, br_ref, x1_ref, ids_ref, wts_ref, *, alpha):
    mix = jnp.dot(mg_ref[...], wo_ref[...], preferred_element_type=F32)
    x1 = _layer_norm(alpha * x_ref[...] + mix, g_ref[...], b_ref[...])
    x1_ref[...] = x1
    xh = x1.astype(BF16)
    xl = (x1 - xh.astype(F32)).astype(BF16)
    wr = wr_ref[...]
    a = jnp.dot(xh, wr, preferred_element_type=F32)
    b = jnp.dot(xl, wr[:, :LANES], preferred_element_type=F32)
    ids, wts = _route(a[:, :LANES] + a[:, LANES:] + b + br_ref[...])
    ids_ref[...] = ids
    wts_ref[...] = wts


def _out_ln_route(merged, x, w_out, g, b, wr2, br, alpha):
    s, d = x.shape
    tm = min(OUTLN_TM, s)
    row = lambda c: pl.BlockSpec((tm, c), lambda i: (i, 0))
    full = lambda a: pl.BlockSpec(a.shape, lambda i: (0, 0))
    return pl.pallas_call(
        functools.partial(_outln_kernel, alpha=alpha),
        out_shape=(jax.ShapeDtypeStruct((s, d), F32), jax.ShapeDtypeStruct((s, LANES), I32),
                   jax.ShapeDtypeStruct((s, LANES), F32)),
        grid=(s // tm,),
        in_specs=[row(d), row(d), full(w_out), full(g), full(b), full(wr2), full(br)],
        out_specs=(row(d), row(LANES), row(LANES)),
        compiler_params=_params(("parallel",)),
        name="out_proj_ln_route",
    )(merged, x, w_out, g, b, wr2, br)


def _rank_kernel(ids_ref, dest_ref, bexp_ref, meta_ref, cnt_ref, carry_ref, start_ref, *, block, n_blocks):
    ph, i = pl.program_id(0), pl.program_id(1)
    ids = ids_ref[...]
    tb = ids.shape[0]
    lane = lax.broadcasted_iota(I32, (tb, LANES), 1)
    oh1 = (lane == ids[:, 0:1]).astype(F32)
    oh2 = (lane == ids[:, 1:2]).astype(F32)
    oh = oh1 + oh2
    colsum = jnp.sum(oh, axis=0, keepdims=True)

    @pl.when((ph == 0) & (i == 0))
    def _():
        cnt_ref[...] = jnp.zeros_like(cnt_ref)

    @pl.when(ph == 0)
    def _():
        cnt_ref[...] += jnp.broadcast_to(colsum, cnt_ref.shape)

    @pl.when((ph == 1) & (i == 0))
    def _():
        cnt = cnt_ref[0:1, :]
        nblk = jnp.floor((cnt + (block - 1)) * (1.0 / block))
        r = lax.broadcasted_iota(I32, (LANES, LANES), 0)
        c = lax.broadcasted_iota(I32, (LANES, LANES), 1)
        upper = (r < c).astype(BF16)
        nb8 = jnp.broadcast_to(nblk, (8, LANES)).astype(BF16)
        bstart = jnp.dot(nb8, upper, preferred_element_type=F32)
        start_ref[...] = bstart * block
        carry_ref[...] = jnp.zeros_like(carry_ref)
        lane1 = lax.broadcasted_iota(I32, (1, LANES), 1)
        bend = jnp.where(lane1 < N_EXPERTS, bstart[0:1] + nblk, 4.0 * n_blocks)
        bidx = lax.broadcasted_iota(I32, (n_blocks, LANES), 0).astype(F32)
        be = jnp.sum((jnp.broadcast_to(bend, (n_blocks, LANES)) <= bidx).astype(F32), axis=1, keepdims=True)
        be = jnp.minimum(be, N_EXPERTS - 1.0)
        lane_b = lax.broadcasted_iota(I32, (n_blocks, LANES), 1).astype(F32)
        owns = jnp.broadcast_to((nblk > 0.0) & (lane1 < N_EXPERTS), (n_blocks, LANES))
        nxt = jnp.min(jnp.where(owns & (lane_b > be), lane_b, 1.0 * LANES), axis=1, keepdims=True)
        nxt = jnp.where(nxt >= 1.0 * LANES, -1.0, nxt)
        before = jnp.sum((owns & (lane_b < be)).astype(F32), axis=1, keepdims=True)
        parity = before - 2.0 * jnp.floor(before * 0.5)
        bexp_ref[...] = jnp.where(lane_b == 0.0, be, jnp.where(lane_b == 1.0, nxt, parity)).astype(I32)
        total = jnp.sum(jnp.where(lane1 < N_EXPERTS, nblk, 0.0), axis=1, keepdims=True)
        row = lax.broadcasted_iota(I32, (8, LANES), 0)
        first_pad = start_ref[0:1, :] + cnt
        meta = jnp.where(row == 0, jnp.broadcast_to(total, (8, LANES)),
                         jnp.where(row == 1, jnp.broadcast_to(first_pad, (8, LANES)),
                                   jnp.broadcast_to(bstart[0:1] * block + nblk * block, (8, LANES))))
        meta_ref[...] = meta.astype(I32)

    @pl.when(ph == 1)
    def _():
        rr = lax.broadcasted_iota(I32, (tb, tb), 0)
        cc = lax.broadcasted_iota(I32, (tb, tb), 1)
        lower = (cc < rr).astype(BF16)
        prefix = jnp.dot(lower, oh.astype(BF16), preferred_element_type=F32)
        pos = prefix + carry_ref[0:1, :] + start_ref[0:1, :]
        d1 = jnp.sum(pos * oh1, axis=1, keepdims=True)
        d2 = jnp.sum(pos * oh2, axis=1, keepdims=True)
        dest_ref[...] = jnp.where(lane == 0, d1, jnp.where(lane == 1, d2, 0.0)).astype(I32)
        carry_ref[...] += jnp.broadcast_to(colsum, carry_ref.shape)


def _rank(ids, block, n_blocks):
    t = ids.shape[0]
    tb = min(RANK_TB, t)
    return pl.pallas_call(
        functools.partial(_rank_kernel, block=block, n_blocks=n_blocks),
        out_shape=(jax.ShapeDtypeStruct((t, LANES), I32), jax.ShapeDtypeStruct((n_blocks, LANES), I32),
                   jax.ShapeDtypeStruct((8, LANES), I32)),
        grid=(2, t // tb),
        in_specs=[pl.BlockSpec((tb, LANES), lambda p, i: (i, 0))],
        out_specs=(pl.BlockSpec((tb, LANES), lambda p, i: (i * p, 0)),
                   pl.BlockSpec((n_blocks, LANES), lambda p, i: (0, 0)),
                   pl.BlockSpec((8, LANES), lambda p, i: (0, 0))),
        scratch_shapes=[pltpu.VMEM((8, LANES), F32)] * 3,
        compiler_params=_params(("arbitrary", "arbitrary")),
        name="moe_rank",
    )(ids)


def _invert_kernel(dest_ref, zeros_hbm, inv_hbm, inv_ref, sem):
    i = pl.program_id(0)
    n = dest_ref.shape[0]

    @pl.when(i == 0)
    def _():
        fill = pltpu.make_async_copy(zeros_hbm, inv_ref, sem)
        fill.start()
        fill.wait()

    def body(j, c):
        inv_ref[dest_ref[j]] = i * n + j
        return c

    lax.fori_loop(0, n, body, 0, unroll=8)

    @pl.when(i == pl.num_programs(0) - 1)
    def _():
        flush = pltpu.make_async_copy(inv_ref, inv_hbm, sem)
        flush.start()
        flush.wait()


def _invert(dest_flat, rows):
    a = dest_flat.shape[0]
    chunk = min(INVERT_CHUNK, a)
    return pl.pallas_call(
        _invert_kernel,
        out_shape=jax.ShapeDtypeStruct((rows,), I32),
        grid=(a // chunk,),
        in_specs=[pl.BlockSpec((chunk,), lambda i: (i,), memory_space=pltpu.SMEM),
                  pl.BlockSpec(memory_space=pl.ANY)],
        out_specs=pl.BlockSpec(memory_space=pl.ANY),
        scratch_shapes=[pltpu.SMEM((rows,), I32), pltpu.SemaphoreType.DMA],
        compiler_params=_params(("arbitrary",)),
        name="moe_invert",
    )(dest_flat, jnp.zeros((rows,), I32))


def _expert_kernel(tab_ref, nact_ref, inv0_ref, inv1_ref, inv2_ref, x1_ref, wg_hbm, wu_hbm, wd_hbm, ys_ref,
                   xbuf, wg_f, wu_f, wd_f, wg_b, wu_b, wd_b, gsem, wsem):
    b = pl.program_id(0)
    n_active = nact_ref[0]
    active = b < n_active
    block = xbuf.shape[1]
    expert, next_expert, wslot = tab_ref[0, b], tab_ref[1, b], tab_ref[2, b]
    first_of_expert = (b == 0) | (tab_ref[0, jnp.maximum(b - 1, 0)] != expert)

    def row_copy(token, slot, r):
        return pltpu.make_async_copy(x1_ref.at[pl.ds(token, 1)], xbuf.at[slot, pl.ds(r, 1)], gsem.at[slot])

    def gather(inv_ref, blk):
        for slot in range(GATHER_DEPTH):
            @pl.when(lax.rem(blk, GATHER_DEPTH) == slot)
            def _():
                for r0 in range(0, block, DMA_GROUP):
                    tokens = [lax.shift_right_logical(inv_ref[0, 0, r], 1) for r in range(r0, r0 + DMA_GROUP)]
                    for j, token in enumerate(tokens):
                        row_copy(token, slot, r0 + j).start()

    def weight_copies(e, s):
        return (pltpu.make_async_copy(wg_hbm.at[e], wg_f.at[s], wsem.at[s, 0]),
                pltpu.make_async_copy(wu_hbm.at[e], wu_f.at[s], wsem.at[s, 1]),
                pltpu.make_async_copy(wd_hbm.at[e], wd_f.at[s], wsem.at[s, 2]))

    @pl.when(b == 0)
    def _():
        for c in weight_copies(expert, wslot):
            c.start(priority=1)
        gather(inv0_ref, b)

    @pl.when((b == 0) & (n_active > 1))
    def _():
        gather(inv1_ref, b + 1)

    @pl.when(b + 2 < n_active)
    def _():
        gather(inv2_ref, b + 2)

    @pl.when(active & first_of_expert)
    def _():
        for c in weight_copies(expert, wslot):
            c.wait()

        @pl.when(next_expert >= 0)
        def _():
            for c in weight_copies(next_expert, 1 - wslot):
                c.start(priority=1)

        wg_b[...] = wg_f[wslot].astype(BF16)
        wu_b[...] = wu_f[wslot].astype(BF16)
        wd_b[...] = wd_f[wslot].astype(BF16)

    @pl.when(active)
    def _():
        for r in range(block):
            row_copy(0, lax.rem(b, GATHER_DEPTH), r).wait()
        x = xbuf[lax.rem(b, GATHER_DEPTH)].astype(BF16)
        g = jnp.dot(x, wg_b[...], preferred_element_type=F32)
        u = jnp.dot(x, wu_b[...], preferred_element_type=F32)
        h = (g / (1.0 + jnp.exp(-g)) * u).astype(BF16)
        ys_ref[...] = jnp.dot(h, wd_b[...], preferred_element_type=F32)

    @pl.when(jnp.logical_not(active))
    def _():
        ys_ref[...] = jnp.zeros_like(ys_ref)


def _experts(x1, inv, tab, nact, w_gate, w_up, w_down, block):
    n_blocks = tab.shape[1]
    d, ff = w_gate.shape[1], w_gate.shape[2]
    inv3 = inv.reshape(n_blocks, 1, block)
    ahead = lambda k: pl.BlockSpec((1, 1, block), lambda b, tb, na: (jnp.minimum(b + k, n_blocks - 1), 0, 0),
                                   memory_space=pltpu.SMEM)
    hbm = pl.BlockSpec(memory_space=pl.ANY)
    return pl.pallas_call(
        _expert_kernel,
        out_shape=jax.ShapeDtypeStruct((n_blocks * block, d), F32),
        grid_spec=pltpu.PrefetchScalarGridSpec(
            num_scalar_prefetch=2, grid=(n_blocks,),
            in_specs=[ahead(0), ahead(1), ahead(2), hbm, hbm, hbm, hbm],
            out_specs=pl.BlockSpec((block, d), lambda b, tb, na: (b, 0)),
            scratch_shapes=[pltpu.VMEM((GATHER_DEPTH, block, d), F32),
                            pltpu.VMEM((2, d, ff), F32), pltpu.VMEM((2, d, ff), F32), pltpu.VMEM((2, ff, d), F32),
                            pltpu.VMEM((d, ff), BF16), pltpu.VMEM((d, ff), BF16), pltpu.VMEM((ff, d), BF16),
                            pltpu.SemaphoreType.DMA((GATHER_DEPTH,)), pltpu.SemaphoreType.DMA((2, 3))]),
        compiler_params=_params(("arbitrary",)),
        name="moe_experts",
    )(tab, nact, inv3, inv3, inv3, x1, w_gate, w_up, w_down)


def _combine_kernel(dest_ref, next_ref, x1_ref, wts_ref, g_ref, b_ref, ys_ref, o_ref, buf, sem, *, alpha):
    i = pl.program_id(0)
    tm = x1_ref.shape[0]
    slot = lax.rem(i, 2)

    def row_copy(row, s, r, k):
        return pltpu.make_async_copy(ys_ref.at[pl.ds(row, 1)], buf.at[s, k, pl.ds(r, 1)], sem.at[s])

    def gather(idx_ref, dyn_slot):
        for s in range(2):
            @pl.when(dyn_slot == s)
            def _():
                for a0 in range(0, TOP_K * tm, DMA_GROUP):
                    rows = [idx_ref[0, 0, a] for a in range(a0, a0 + DMA_GROUP)]
                    for j, row in enumerate(rows):
                        r, k = divmod(a0 + j, TOP_K)
                        row_copy(row, s, r, k).start(priority=k % 2)

    @pl.when(i == 0)
    def _():
        gather(dest_ref, slot)

    @pl.when(i + 1 < pl.num_programs(0))
    def _():
        gather(next_ref, 1 - slot)

    for r in range(tm):
        for k in range(TOP_K):
            row_copy(0, slot, r, k).wait()
    w = wts_ref[...]
    ffn = w[:, 0:1] * buf[slot, 0] + w[:, 1:2] * buf[slot, 1]
    o_ref[...] = _layer_norm(alpha * x1_ref[...] + ffn, g_ref[...], b_ref[...])


def _combine(ys, dest3, x1, wts, g, b, alpha):
    t, d = x1.shape
    tm = dest3.shape[2] // TOP_K
    row = lambda c: pl.BlockSpec((tm, c), lambda i: (i, 0))
    full = lambda a: pl.BlockSpec(a.shape, lambda i: (0, 0))
    return pl.pallas_call(
        functools.partial(_combine_kernel, alpha=alpha),
        out_shape=jax.ShapeDtypeStruct((t, d), F32),
        grid=(t // tm,),
        in_specs=[pl.BlockSpec((1, 1, TOP_K * tm), lambda i: (i, 0, 0), memory_space=pltpu.SMEM),
                  pl.BlockSpec((1, 1, TOP_K * tm), lambda i: (jnp.minimum(i + 1, t // tm - 1), 0, 0),
                               memory_space=pltpu.SMEM),
                  row(d), row(LANES), full(g), full(b), pl.BlockSpec(memory_space=pl.ANY)],
        out_specs=row(d),
        scratch_shapes=[pltpu.VMEM((2, TOP_K, tm, d), F32), pltpu.SemaphoreType.DMA((2,))],
        compiler_params=_params(("arbitrary",)),
        name="moe_combine_ln",
    )(dest3, dest3, x1, wts, g, b, ys)


def _rope_tables(s):
    inv_freq = ROPE_THETA ** (-jnp.arange(0, HEAD_DIM, 2, dtype=F32) / HEAD_DIM)
    ang = jnp.arange(s, dtype=F32)[:, None] * inv_freq[None, :]
    cos, sin = jnp.cos(ang), jnp.sin(ang)
    return jnp.concatenate([cos, cos], axis=1), jnp.concatenate([-sin, sin], axis=1)


def _split_bf16_pair(w, width):
    hi = w.astype(BF16)
    lo = (w - hi.astype(F32)).astype(BF16)
    pad = lambda a: jnp.pad(a, ((0, 0), (0, width - a.shape[1])))
    return jnp.concatenate([pad(hi), pad(lo)], axis=1)


def _layer(x2, layer, depth, w_in, b_forget, lam_q1, lam_k1, lam_q2, lam_k2, diff_norm_g, w_proj_diff,
           w_proj_fox, w_out, ln1_g, ln1_b, w_rg, b_rg, w_re, b_re, w_gate, w_up, w_down, ln2_g, ln2_b):
    s, d = x2.shape
    alpha = (2 * depth) ** 0.25
    lam_init = 0.8 - 0.6 * math.exp(-0.3 * layer)
    n_diff, n_fox = diff_norm_g.shape[0], b_forget.shape[0]
    qk_cols = n_diff * 2 * HEAD_DIM
    lin_cols = n_diff * DIFF_V_DIM + 3 * n_fox * HEAD_DIM
    q_scale = HEAD_DIM ** -0.5 * LOG2E

    f0 = 2 * qk_cols + lin_cols
    w_t = w_in.T
    fcum, x_bf = _forget_cumsum(x2, w_t[f0:f0 + n_fox].T, b_forget)

    cos_t, sin_t = _rope_tables(s)
    tm = min(PROJ_TM, s)
    tab = pl.BlockSpec((tm, HEAD_DIM), lambda j, i: (i, 0))
    assert qk_cols == PROJ_TN, "column block 0 of the rotary call must be exactly the queries"
    qk = _proj(x_bf, w_t, 0, 2 * qk_cols, "rope", (cos_t, sin_t), (tab, tab), q_scale)
    col_scale = jnp.ones((1, lin_cols), F32).at[:, n_diff * DIFF_V_DIM:n_diff * DIFF_V_DIM + n_fox * HEAD_DIM].set(q_scale)
    lin = _proj(x_bf, w_t, 2 * qk_cols, lin_cols, "scale", (col_scale,),
                (pl.BlockSpec((1, min(PROJ_TN, lin_cols)), lambda j, i: (0, j)),))
    gates = _proj(x_bf, w_t, f0 + n_fox, w_in.shape[1] - f0 - n_fox, "sigmoid")

    lam_rows = jnp.stack([lam_q1, lam_k1, lam_q2, lam_k2]).astype(F32)
    n_maps = 2 * n_diff
    o_maps = _attention(qk, qk, lin, 0, n_maps, lambda h: h // 2, n_maps, DIFF_V_DIM)
    fq_col = n_diff * DIFF_V_DIM // HEAD_DIM
    o_fox = _attention(lin, lin, lin, fq_col, fq_col + n_fox, lambda h: fq_col + 2 * n_fox + h, n_fox, HEAD_DIM, fcum)

    merged = _merge(o_maps, o_fox, gates, lam_rows, diff_norm_g.astype(F32), w_proj_diff.astype(BF16),
                    w_proj_fox.astype(BF16), lam_init)
    w_router = jnp.concatenate([w_rg, jnp.moveaxis(w_re, 0, 1).reshape(d, N_EXPERTS)], axis=1)
    b_router = jnp.pad(jnp.concatenate([b_rg, b_re.reshape(N_EXPERTS)]).astype(F32),
                       (0, LANES - N_GROUPS - N_EXPERTS)).reshape(1, LANES)
    x1, ids, wts = _out_ln_route(merged, x2, w_out.astype(BF16), ln1_g.reshape(1, d), ln1_b.reshape(1, d),
                                 _split_bf16_pair(w_router, LANES), b_router, alpha)

    block = MOE_BLOCK
    n_blocks = -(-(s * TOP_K) // block) + N_EXPERTS
    dest, bexp, meta = _rank(ids, block, n_blocks)
    tmr = min(ROW_TM, s)
    dest2 = dest[:, :TOP_K]
    inv = _invert(dest2.reshape(s * TOP_K), n_blocks * block)
    ys = _experts(x1, inv, bexp[:, :3].T, meta[0, :1], w_gate, w_up, w_down, block)
    dest3 = dest2.reshape(s // tmr, 1, TOP_K * tmr)
    return _combine(ys, dest3, x1, wts, ln2_g.reshape(1, d), ln2_b.reshape(1, d), alpha)


def kernel(x, w_in, b_forget, lam_q1, lam_k1, lam_q2, lam_k2, diff_norm_g, w_proj_diff, w_proj_fox, w_out,
           ln1_g, ln1_b, w_router_group, b_router_group, w_router_expert, b_router_expert, w_gate, w_up,
           w_down, ln2_g, ln2_b):
    batch, s, d = x.shape
    depth = w_in.shape[0]
    params = (w_in, b_forget, lam_q1, lam_k1, lam_q2, lam_k2, diff_norm_g, w_proj_diff, w_proj_fox, w_out,
              ln1_g, ln1_b, w_router_group, b_router_group, w_router_expert, b_router_expert, w_gate, w_up,
              w_down, ln2_g, ln2_b)
    outs = []
    for bi in range(batch):
        h = x[bi]
        for layer in range(depth):
            h = _layer(h, layer, depth, *(p[layer] for p in params))
        outs.append(h)
    return jnp.stack(outs)
```

```python
import functools
import math

import jax
import jax.numpy as jnp
from jax import lax
from jax.experimental import pallas as pl
from jax.experimental.pallas import tpu as pltpu

F32, BF16, I32 = jnp.float32, jnp.bfloat16, jnp.int32

HEAD_DIM = 128
DIFF_V_DIM = 2 * HEAD_DIM
ROPE_THETA = 10000.0
N_GROUPS = 4
EXPERTS_PER_GROUP = 8
N_EXPERTS = N_GROUPS * EXPERTS_PER_GROUP
TOP_K = 2
LN_EPS = 1e-5
LOG2E = 1.4426950408889634
NEG = -1e30
LANES = 128
SUBLANES = 8
VMEM_LIMIT = 56 * 1024 * 1024

MOE_BLOCK = 128
PROJ_TM, PROJ_TN = 1024, 1024
PROJ_CAST_ROWS = 256
ATT_TQ = 512
ONES_ROWS = 16
FOX_UNROLL, DIFF_UNROLL = 16, 8
FORGET_TS = 512
MERGE_TM = 512
OUTLN_TM = 512
RANK_TB = 1024
ROW_TM = 256
INVERT_CHUNK = 1024
GATHER_DEPTH = 3
DMA_GROUP = 16


def _params(sem, vmem=VMEM_LIMIT):
    return pltpu.CompilerParams(dimension_semantics=sem, vmem_limit_bytes=vmem)


def _proj_kernel(x_ref, wt_hbm, *rest, epilogue, row0, q_scale):
    *extra, o_ref, w_f32, w_bf, sem = rest
    j, i = pl.program_id(0), pl.program_id(1)
    tn = w_bf.shape[0]
    slot = lax.rem(j, 2)

    def fetch(jj, s):
        rows = pl.ds(pl.multiple_of(row0 + jj * tn, SUBLANES), tn)
        return pltpu.make_async_copy(wt_hbm.at[rows, :], w_f32.at[s], sem.at[s])

    @pl.when(i == 0)
    def _():
        @pl.when(j == 0)
        def _():
            fetch(j, slot).start()

        fetch(j, slot).wait()

        @pl.when(j + 1 < pl.num_programs(0))
        def _():
            fetch(j + 1, 1 - slot).start()

        rows = min(PROJ_CAST_ROWS, tn)

        def chunk(c, carry):
            r = pl.ds(pl.multiple_of(c * rows, rows), rows)
            w_bf[r, :] = w_f32[slot, r, :].astype(BF16)
            return carry

        lax.fori_loop(0, tn // rows, chunk, 0)

    acc = lax.dot_general(x_ref[...], w_bf[...], (((1,), (1,)), ((), ())), preferred_element_type=F32)
    if epilogue == "rope":
        cos_ref, sin_ref = extra
        scale = jnp.where(j == 0, q_scale, 1.0)
        cosf, sinf = cos_ref[...] * scale, sin_ref[...] * scale
        for h in range(tn // HEAD_DIM):
            t = acc[:, h * HEAD_DIM:(h + 1) * HEAD_DIM]
            o_ref[:, h * HEAD_DIM:(h + 1) * HEAD_DIM] = (
                t * cosf + pltpu.roll(t, HEAD_DIM // 2, 1) * sinf).astype(o_ref.dtype)
    elif epilogue == "scale":
        o_ref[...] = (acc * extra[0][...]).astype(o_ref.dtype)
    else:
        o_ref[...] = (1.0 / (1.0 + jnp.exp(-acc))).astype(o_ref.dtype)


def _proj(x_bf, w_t, col0, n, epilogue, extra=(), extra_specs=(), q_scale=1.0):
    m, k = x_bf.shape
    tm, tn = min(PROJ_TM, m), min(PROJ_TN, n)
    assert col0 % SUBLANES == 0 and n % tn == 0
    return pl.pallas_call(
        functools.partial(_proj_kernel, epilogue=epilogue, row0=col0, q_scale=q_scale),
        out_shape=jax.ShapeDtypeStruct((m, n), BF16),
        grid=(n // tn, m // tm),
        in_specs=[pl.BlockSpec((tm, k), lambda j, i: (i, 0)), pl.BlockSpec(memory_space=pl.ANY), *extra_specs],
        out_specs=pl.BlockSpec((tm, tn), lambda j, i: (i, j)),
        scratch_shapes=[pltpu.VMEM((2, tn, k), F32), pltpu.VMEM((tn, k), BF16), pltpu.SemaphoreType.DMA((2,))],
        compiler_params=_params(("arbitrary", "arbitrary")),
        name=f"in_proj_{epilogue}",
    )(x_bf, w_t, *extra)


def _split3(v):
    h = v.astype(BF16)
    r = v - h.astype(F32)
    m = r.astype(BF16)
    return h, m, (r - m.astype(F32)).astype(BF16)


def _forget_kernel(x_ref, w_ref, b_ref, o_ref, xbf_ref, carry_ref):
    i = pl.program_id(0)

    @pl.when(i == 0)
    def _():
        carry_ref[...] = jnp.zeros_like(carry_ref)

    x = x_ref[...]
    xh = x.astype(BF16)
    xbf_ref[...] = xh
    xl = (x - xh.astype(F32)).astype(BF16)
    w = w_ref[...]
    a = jnp.dot(xh, w, preferred_element_type=F32)
    b = jnp.dot(xl, w[:, :LANES], preferred_element_type=F32)
    z = a[:, :LANES] + a[:, LANES:] + b + b_ref[...]
    logf = jnp.minimum(z, 0.0) - jnp.log1p(jnp.exp(-jnp.abs(z)))
    ts = logf.shape[0]
    tri = (lax.broadcasted_iota(I32, (ts, ts), 1) <= lax.broadcasted_iota(I32, (ts, ts), 0)).astype(BF16)
    c = carry_ref[0:1, :]
    for piece in _split3(logf):
        c = c + jnp.dot(tri, piece, preferred_element_type=F32)
    o_ref[...] = c * LOG2E
    carry_ref[...] = jnp.broadcast_to(c[ts - 1:ts, :], carry_ref.shape)


def _forget_cumsum(x, w_f, b_f):
    s, d = x.shape
    nh = w_f.shape[1]
    wh = w_f.astype(BF16)
    wl = (w_f - wh.astype(F32)).astype(BF16)
    pad = lambda a: jnp.pad(a, ((0, 0), (0, LANES - nh)))
    w2 = jnp.concatenate([pad(wh), pad(wl)], axis=1)
    b2 = jnp.pad(b_f.astype(F32), (0, LANES - nh)).reshape(1, LANES)
    ts = min(FORGET_TS, s)
    return pl.pallas_call(
        _forget_kernel,
        out_shape=(jax.ShapeDtypeStruct((s, LANES), F32), jax.ShapeDtypeStruct((s, d), BF16)),
        grid=(s // ts,),
        in_specs=[pl.BlockSpec((ts, d), lambda i: (i, 0)),
                  pl.BlockSpec((d, 2 * LANES), lambda i: (0, 0)),
                  pl.BlockSpec((1, LANES), lambda i: (0, 0))],
        out_specs=(pl.BlockSpec((ts, LANES), lambda i: (i, 0)), pl.BlockSpec((ts, d), lambda i: (i, 0))),
        scratch_shapes=[pltpu.VMEM((8, LANES), F32)],
        compiler_params=_params(("arbitrary",)),
        name="forget_cumsum",
    )(x, w2, b2)


def _pipeline3(n, unroll, stage_a, stage_b, stage_c, carry):
    if n < 3:
        for i in range(n):
            carry, post = stage_b(i, i % 2, i % unroll, stage_a(i, i % 2), carry)
            stage_c(i, i % unroll, post)
        return carry

    def step(i, m, state, do_a=True):
        aux, post_old, post_new, carry = state
        stage_c(i - 2, (m - 2) % unroll, post_old)
        aux_next = stage_a(i + 1, (m + 1) % 2) if do_a else aux
        carry, post = stage_b(i, m % 2, m, aux, carry)
        return aux_next, post_new, post, carry

    aux0 = stage_a(0, 0)
    aux1 = stage_a(1, 1)
    carry, post0 = stage_b(0, 0, 0, aux0, carry)
    aux2 = stage_a(2, 0)
    carry, post1 = stage_b(1, 1, 1, aux1, carry)
    state = (aux2, post0, post1, carry)
    mid = n - 3

    def unrolled(h, s):
        for u in range(unroll):
            s = step(unroll * h + 2 + u, (2 + u) % unroll, s)
        return s

    state = lax.fori_loop(0, mid // unroll, unrolled, state)
    for i in range(mid - mid % unroll + 2, n - 1):
        state = step(i, i % unroll, state)
    _, post_old, post_new, carry = step(n - 1, (n - 1) % unroll, state, do_a=False)
    stage_c(n - 2, (n - 2) % unroll, post_old)
    stage_c(n - 1, (n - 1) % unroll, post_new)
    return carry


def _attn_kernel(qi_tab, kb_tab, q_ref, k_ref, v_ref, *rest, t, has_bias, unroll):
    if has_bias:
        c_ref, o_ref, kaug, qt, vt, st0, st1, *p, acc_all, m_all = rest
    else:
        o_ref, qt, vt, st0, st1, *p, acc_all, m_all = rest
        kaug = k_ref
    st = (st0, st1)
    seq, dv = v_ref.shape
    nq = seq // t
    dq = q_ref.shape[1]
    blk = lambda b: pl.ds(pl.multiple_of(b * t, t), t)
    head = pl.program_id(0)

    def prep(c, carry):
        rows = blk(c)
        lane = lax.broadcasted_iota(I32, (t, LANES), 1)
        if has_bias:
            neg_c = -jnp.sum(jnp.where(lane == head, c_ref[rows, :], 0.0), axis=1, keepdims=True)
            hi, mid, lo = (x.astype(F32) for x in _split3(neg_c))
            aug = jnp.where(lane == 0, hi, jnp.where(lane == 1, mid, jnp.where(lane == 2, lo, 0.0)))
            kaug[rows, :dq] = k_ref[rows, :]
            kaug[rows, dq:] = aug.astype(BF16)
            ones = (lax.broadcasted_iota(I32, (LANES, t), 0) < 3).astype(F32)
            qt[:, rows] = jnp.concatenate([q_ref[rows, :].T, ones.astype(BF16)], axis=0)
        else:
            qt[:, rows] = q_ref[rows, :].T
        one_row = (lax.broadcasted_iota(I32, (ONES_ROWS, t), 0) == 0).astype(F32)
        vt[:, rows] = jnp.concatenate([v_ref[rows, :].T, one_row.astype(BF16)], axis=0)
        return carry

    lax.fori_loop(0, nq, prep, 0)

    def scores(qi, kb):
        return jnp.dot(kaug[blk(kb), :], qt[:, blk(qi)], preferred_element_type=F32)

    def values(kb, slot):
        return jnp.dot(vt[:, blk(kb)], p[slot][...], preferred_element_type=F32)

    def save_max(qi, m):
        m_all[qi] = jnp.broadcast_to(m, m_all.shape[1:])

    def diag_scores(i, slot):
        mask = lax.broadcasted_iota(I32, (t, t), 1) >= lax.broadcasted_iota(I32, (t, t), 0)
        s = jnp.where(mask, scores(i, i), NEG)
        st[slot][...] = s
        return jnp.max(s, axis=0, keepdims=True)

    def diag_softmax(i, slot, pslot, cmax, carry):
        p[pslot][...] = jnp.exp2(st[slot][...] - cmax).astype(BF16)
        save_max(i, cmax)
        return carry, None

    def diag_values(i, slot, _):
        acc_all[i] = values(i, slot)

    _pipeline3(nq, unroll, diag_scores, diag_softmax, diag_values, 0)

    n_low = nq * (nq - 1) // 2
    if n_low:
        last = n_low - 1
        tab = lambda ref, f: ref[jnp.minimum(f, last)]

        def low_scores(f, slot):
            s = scores(tab(qi_tab, f), tab(kb_tab, f))
            st[slot][...] = s
            return jnp.max(s, axis=0, keepdims=True)

        def low_softmax(f, slot, pslot, cmax, carry):
            qi = tab(qi_tab, f)
            m_prev = m_all[qi][0:1]
            m_new = jnp.maximum(m_prev, cmax)
            save_max(qi, m_new)
            p[pslot][...] = jnp.exp2(st[slot][...] - m_new).astype(BF16)
            return carry, jnp.exp2(m_prev - m_new)

        def low_values(f, slot, a):
            qi = tab(qi_tab, f)
            acc_all[qi] = a * acc_all[qi] + values(tab(kb_tab, f), slot)

        _pipeline3(n_low, unroll, low_scores, low_softmax, low_values, 0)

    def finish(qi, carry):
        acc = acc_all[qi]
        o_ref[blk(qi), :] = (acc[:dv] / acc[dv:dv + 1]).T.astype(o_ref.dtype)
        return carry

    lax.fori_loop(0, nq, finish, 0)


def _attention(q_src, k_src, v_src, q_col, k_col, v_col, n_heads, dv, fcum=None):
    s = q_src.shape[0]
    t = min(ATT_TQ, s)
    nq = s // t
    chains = ([], [])
    for qi in sorted(range(1, nq), reverse=True):
        min(chains, key=len).extend((qi, kb) for kb in range(qi))
    pairs = [pr for both in zip(*chains) for pr in both]
    pairs += max(chains, key=len)[min(map(len, chains)):] or ([] if pairs else [(0, 0)])
    qi_tab = jnp.array([a for a, _ in pairs], I32)
    kb_tab = jnp.array([b for _, b in pairs], I32)
    has_bias = fcum is not None
    dk = 2 * HEAD_DIM if has_bias else HEAD_DIM
    unroll = FOX_UNROLL if has_bias else DIFF_UNROLL
    in_specs = [pl.BlockSpec((s, HEAD_DIM), lambda h, *_: (0, q_col + h)),
                pl.BlockSpec((s, HEAD_DIM), lambda h, *_: (0, k_col + h)),
                pl.BlockSpec((s, dv), lambda h, *_: (0, v_col(h)))]
    args = [q_src, k_src, v_src]
    scratch = []
    if has_bias:
        in_specs.append(pl.BlockSpec((s, LANES), lambda h, *_: (0, 0)))
        args.append(fcum)
        scratch.append(pltpu.VMEM((s, dk), BF16))
    scratch += [pltpu.VMEM((dk, s), BF16), pltpu.VMEM((dv + ONES_ROWS, s), BF16),
                pltpu.VMEM((t, t), F32), pltpu.VMEM((t, t), F32), *[pltpu.VMEM((t, t), BF16)] * unroll,
                pltpu.VMEM((nq, dv + ONES_ROWS, t), F32), pltpu.VMEM((nq, SUBLANES, t), F32)]
    return pl.pallas_call(
        functools.partial(_attn_kernel, t=t, has_bias=has_bias, unroll=unroll),
        out_shape=jax.ShapeDtypeStruct((s, n_heads * dv), BF16),
        grid_spec=pltpu.PrefetchScalarGridSpec(
            num_scalar_prefetch=2, grid=(n_heads,), in_specs=in_specs,
            out_specs=pl.BlockSpec((s, dv), lambda h, *_: (0, h)), scratch_shapes=scratch),
        compiler_params=_params(("arbitrary",)),
        name="fox_attention" if has_bias else "diff_attention",
    )(qi_tab, kb_tab, *args)


def _merge_kernel(om_ref, of_ref, gd_ref, gf_ref, lam_ref, g_ref, wd_ref, wf_ref, o_ref, *, lam_init):
    lam_v = lam_ref[...]
    lam = (jnp.exp(jnp.sum(lam_v[0:1] * lam_v[1:2], axis=1, keepdims=True))
           - jnp.exp(jnp.sum(lam_v[2:3] * lam_v[3:4], axis=1, keepdims=True)) + lam_init)
    heads = []
    for h in range(g_ref.shape[0]):
        o1 = om_ref[:, (2 * h) * DIFF_V_DIM:(2 * h + 1) * DIFF_V_DIM].astype(F32)
        o2 = om_ref[:, (2 * h + 1) * DIFF_V_DIM:(2 * h + 2) * DIFF_V_DIM].astype(F32)
        o = o1 - lam * o2
        o = o * lax.rsqrt(jnp.mean(o * o, axis=1, keepdims=True) + LN_EPS) * g_ref[h:h + 1, :] * (1.0 - lam_init)
        heads.append(o.astype(BF16))
    ud = jnp.dot(jnp.concatenate(heads, axis=1), wd_ref[...], preferred_element_type=F32)
    uf = jnp.dot(of_ref[...], wf_ref[...], preferred_element_type=F32)
    o_ref[...] = (gd_ref[...].astype(F32) * ud + gf_ref[...].astype(F32) * uf).astype(o_ref.dtype)


def _merge(o_maps, o_fox, gates, lam_rows, norm_g, wd, wf, lam_init):
    s, d = o_maps.shape[0], wd.shape[1]
    tm = min(MERGE_TM, s)
    row = lambda c: pl.BlockSpec((tm, c), lambda i: (i, 0))
    full = lambda a: pl.BlockSpec(a.shape, lambda i: (0, 0))
    return pl.pallas_call(
        functools.partial(_merge_kernel, lam_init=lam_init),
        out_shape=jax.ShapeDtypeStruct((s, d), BF16),
        grid=(s // tm,),
        in_specs=[row(o_maps.shape[1]), row(o_fox.shape[1]),
                  pl.BlockSpec((tm, d), lambda i: (i, 0)), pl.BlockSpec((tm, d), lambda i: (i, 1)),
                  full(lam_rows), full(norm_g), full(wd), full(wf)],
        out_specs=row(d),
        compiler_params=_params(("parallel",)),
        name="merge_branches",
    )(o_maps, o_fox, gates, gates, lam_rows, norm_g, wd, wf)


def _layer_norm(y, g, b):
    mu = jnp.mean(y, axis=1, keepdims=True)
    yc = y - mu
    var = jnp.mean(yc * yc, axis=1, keepdims=True)
    return yc * lax.rsqrt(var + LN_EPS) * g + b


def _route(lg):
    lane = lax.broadcasted_iota(I32, lg.shape, 1)
    far = jnp.int32(4 * LANES)
    is_g = lane < N_GROUPS
    gl = jnp.where(is_g, lg, NEG)
    gmax = jnp.max(gl, axis=1, keepdims=True)
    gidx = jnp.min(jnp.where(gl == gmax, lane, far), axis=1, keepdims=True)
    top_gp = 1.0 / jnp.sum(jnp.where(is_g, jnp.exp(gl - gmax), 0.0), axis=1, keepdims=True)
    lo = N_GROUPS + gidx * EXPERTS_PER_GROUP
    el = jnp.where((lane >= lo) & (lane < lo + EXPERTS_PER_GROUP), lg, NEG)
    m1 = jnp.max(el, axis=1, keepdims=True)
    i1 = jnp.min(jnp.where(el == m1, lane, far), axis=1, keepdims=True)
    el2 = jnp.where(lane == i1, 2.0 * NEG, el)
    m2 = jnp.max(el2, axis=1, keepdims=True)
    i2 = jnp.min(jnp.where(el2 == m2, lane, far), axis=1, keepdims=True)
    d = jnp.exp(m2 - m1)
    w1 = top_gp / (1.0 + d)
    w2 = w1 * d
    ids = jnp.where(lane == 0, i1 - N_GROUPS, jnp.where(lane == 1, i2 - N_GROUPS, 0))
    wts = jnp.where(lane == 0, w1, jnp.where(lane == 1, w2, 0.0))
    return ids, wts


def _outln_kernel(mg_ref, x_ref, wo_ref, g_ref, b_ref, wr_ref, br_ref, x1_ref, ids_ref, wts_ref, *, alpha):
    mix = jnp.dot(mg_ref[...], wo_ref[...], preferred_element_type=F32)
    x1 = _layer_norm(alpha * x_ref[...] + mix, g_ref[...], b_ref[...])
    x1_ref[...] = x1
    xh = x1.astype(BF16)
    xl = (x1 - xh.astype(F32)).astype(BF16)
    wr = wr_ref[...]
    a = jnp.dot(xh, wr, preferred_element_type=F32)
    b = jnp.dot(xl, wr[:, :LANES], preferred_element_type=F32)
    ids, wts = _route(a[:, :LANES] + a[:, LANES:] + b + br_ref[...])
    ids_ref[...] = ids
    wts_ref[...] = wts


def _out_ln_route(merged, x, w_out, g, b, wr2, br, alpha):
    s, d = x.shape
    tm = min(OUTLN_TM, s)
    row = lambda c: pl.BlockSpec((tm, c), lambda i: (i, 0))
    full = lambda a: pl.BlockSpec(a.shape, lambda i: (0, 0))
    return pl.pallas_call(
        functools.partial(_outln_kernel, alpha=alpha),
        out_shape=(jax.ShapeDtypeStruct((s, d), F32), jax.ShapeDtypeStruct((s, LANES), I32),
                   jax.ShapeDtypeStruct((s, LANES), F32)),
        grid=(s // tm,),
        in_specs=[row(d), row(d), full(w_out), full(g), full(b), full(wr2), full(br)],
        out_specs=(row(d), row(LANES), row(LANES)),
        compiler_params=_params(("parallel",)),
        name="out_proj_ln_route",
    )(merged, x, w_out, g, b, wr2, br)


def _rank_kernel(ids_ref, dest_ref, bexp_ref, meta_ref, cnt_ref, carry_ref, start_ref, *, block, n_blocks):
    ph, i = pl.program_id(0), pl.program_id(1)
    ids = ids_ref[...]
    tb = ids.shape[0]
    lane = lax.broadcasted_iota(I32, (tb, LANES), 1)
    oh1 = (lane == ids[:, 0:1]).astype(F32)
    oh2 = (lane == ids[:, 1:2]).astype(F32)
    oh = oh1 + oh2
    colsum = jnp.sum(oh, axis=0, keepdims=True)

    @pl.when((ph == 0) & (i == 0))
    def _():
        cnt_ref[...] = jnp.zeros_like(cnt_ref)

    @pl.when(ph == 0)
    def _():
        cnt_ref[...] += jnp.broadcast_to(colsum, cnt_ref.shape)

    @pl.when((ph == 1) & (i == 0))
    def _():
        cnt = cnt_ref[0:1, :]
        nblk = jnp.floor((cnt + (block - 1)) * (1.0 / block))
        r = lax.broadcasted_iota(I32, (LANES, LANES), 0)
        c = lax.broadcasted_iota(I32, (LANES, LANES), 1)
        upper = (r < c).astype(BF16)
        nb8 = jnp.broadcast_to(nblk, (8, LANES)).astype(BF16)
        bstart = jnp.dot(nb8, upper, preferred_element_type=F32)
        start_ref[...] = bstart * block
        carry_ref[...] = jnp.zeros_like(carry_ref)
        lane1 = lax.broadcasted_iota(I32, (1, LANES), 1)
        bend = jnp.where(lane1 < N_EXPERTS, bstart[0:1] + nblk, 4.0 * n_blocks)
        bidx = lax.broadcasted_iota(I32, (n_blocks, LANES), 0).astype(F32)
        be = jnp.sum((jnp.broadcast_to(bend, (n_blocks, LANES)) <= bidx).astype(F32), axis=1, keepdims=True)
        be = jnp.minimum(be, N_EXPERTS - 1.0)
        lane_b = lax.broadcasted_iota(I32, (n_blocks, LANES), 1).astype(F32)
        owns = jnp.broadcast_to((nblk > 0.0) & (lane1 < N_EXPERTS), (n_blocks, LANES))
        nxt = jnp.min(jnp.where(owns & (lane_b > be), lane_b, 1.0 * LANES), axis=1, keepdims=True)
        nxt = jnp.where(nxt >= 1.0 * LANES, -1.0, nxt)
        before = jnp.sum((owns & (lane_b < be)).astype(F32), axis=1, keepdims=True)
        parity = before - 2.0 * jnp.floor(before * 0.5)
        bexp_ref[...] = jnp.where(lane_b == 0.0, be, jnp.where(lane_b == 1.0, nxt, parity)).astype(I32)
        total = jnp.sum(jnp.where(lane1 < N_EXPERTS, nblk, 0.0), axis=1, keepdims=True)
        row = lax.broadcasted_iota(I32, (8, LANES), 0)
        first_pad = start_ref[0:1, :] + cnt
        meta = jnp.where(row == 0, jnp.broadcast_to(total, (8, LANES)),
                         jnp.where(row == 1, jnp.broadcast_to(first_pad, (8, LANES)),
                                   jnp.broadcast_to(bstart[0:1] * block + nblk * block, (8, LANES))))
        meta_ref[...] = meta.astype(I32)

    @pl.when(ph == 1)
    def _():
        rr = lax.broadcasted_iota(I32, (tb, tb), 0)
        cc = lax.broadcasted_iota(I32, (tb, tb), 1)
        lower = (cc < rr).astype(BF16)
        prefix = jnp.dot(lower, oh.astype(BF16), preferred_element_type=F32)
        pos = prefix + carry_ref[0:1, :] + start_ref[0:1, :]
        d1 = jnp.sum(pos * oh1, axis=1, keepdims=True)
        d2 = jnp.sum(pos * oh2, axis=1, keepdims=True)
        dest_ref[...] = jnp.where(lane == 0, d1, jnp.where(lane == 1, d2, 0.0)).astype(I32)
        carry_ref[...] += jnp.broadcast_to(colsum, carry_ref.shape)


def _rank(ids, block, n_blocks):
    t = ids.shape[0]
    tb = min(RANK_TB, t)
    return pl.pallas_call(
        functools.partial(_rank_kernel, block=block, n_blocks=n_blocks),
        out_shape=(jax.ShapeDtypeStruct((t, LANES), I32), jax.ShapeDtypeStruct((n_blocks, LANES), I32),
                   jax.ShapeDtypeStruct((8, LANES), I32)),
        grid=(2, t // tb),
        in_specs=[pl.BlockSpec((tb, LANES), lambda p, i: (i, 0))],
        out_specs=(pl.BlockSpec((tb, LANES), lambda p, i: (i * p, 0)),
                   pl.BlockSpec((n_blocks, LANES), lambda p, i: (0, 0)),
                   pl.BlockSpec((8, LANES), lambda p, i: (0, 0))),
        scratch_shapes=[pltpu.VMEM((8, LANES), F32)] * 3,
        compiler_params=_params(("arbitrary", "arbitrary")),
        name="moe_rank",
    )(ids)


def _invert_kernel(dest_ref, zeros_hbm, inv_hbm, inv_ref, sem):
    i = pl.program_id(0)
    n = dest_ref.shape[0]

    @pl.when(i == 0)
    def _():
        fill = pltpu.make_async_copy(zeros_hbm, inv_ref, sem)
        fill.start()
        fill.wait()

    def body(j, c):
        inv_ref[dest_ref[j]] = i * n + j
        return c

    lax.fori_loop(0, n, body, 0, unroll=8)

    @pl.when(i == pl.num_programs(0) - 1)
    def _():
        flush = pltpu.make_async_copy(inv_ref, inv_hbm, sem)
        flush.start()
        flush.wait()


def _invert(dest_flat, rows):
    a = dest_flat.shape[0]
    chunk = min(INVERT_CHUNK, a)
    return pl.pallas_call(
        _invert_kernel,
        out_shape=jax.ShapeDtypeStruct((rows,), I32),
        grid=(a // chunk,),
        in_specs=[pl.BlockSpec((chunk,), lambda i: (i,), memory_space=pltpu.SMEM),
                  pl.BlockSpec(memory_space=pl.ANY)],
        out_specs=pl.BlockSpec(memory_space=pl.ANY),
        scratch_shapes=[pltpu.SMEM((rows,), I32), pltpu.SemaphoreType.DMA],
        compiler_params=_params(("arbitrary",)),
        name="moe_invert",
    )(dest_flat, jnp.zeros((rows,), I32))


def _expert_kernel(tab_ref, nact_ref, inv0_ref, inv1_ref, inv2_ref, x1_ref, wg_hbm, wu_hbm, wd_hbm, ys_ref,
                   xbuf, wg_f, wu_f, wd_f, wg_b, wu_b, wd_b, gsem, wsem):
    b = pl.program_id(0)
    n_active = nact_ref[0]
    active = b < n_active
    block = xbuf.shape[1]
    expert, next_expert, wslot = tab_ref[0, b], tab_ref[1, b], tab_ref[2, b]
    first_of_expert = (b == 0) | (tab_ref[0, jnp.maximum(b - 1, 0)] != expert)

    def row_copy(token, slot, r):
        return pltpu.make_async_copy(x1_ref.at[pl.ds(token, 1)], xbuf.at[slot, pl.ds(r, 1)], gsem.at[slot])

    def gather(inv_ref, blk):
        for slot in range(GATHER_DEPTH):
            @pl.when(lax.rem(blk, GATHER_DEPTH) == slot)
            def _():
                for r0 in range(0, block, DMA_GROUP):
                    tokens = [lax.shift_right_logical(inv_ref[0, 0, r], 1) for r in range(r0, r0 + DMA_GROUP)]
                    for j, token in enumerate(tokens):
                        row_copy(token, slot, r0 + j).start()

    def weight_copies(e, s):
        return (pltpu.make_async_copy(wg_hbm.at[e], wg_f.at[s], wsem.at[s, 0]),
                pltpu.make_async_copy(wu_hbm.at[e], wu_f.at[s], wsem.at[s, 1]),
                pltpu.make_async_copy(wd_hbm.at[e], wd_f.at[s], wsem.at[s, 2]))

    @pl.when(b == 0)
    def _():
        for c in weight_copies(expert, wslot):
            c.start(priority=1)
        gather(inv0_ref, b)

    @pl.when((b == 0) & (n_active > 1))
    def _():
        gather(inv1_ref, b + 1)

    @pl.when(b + 2 < n_active)
    def _():
        gather(inv2_ref, b + 2)

    @pl.when(active & first_of_expert)
    def _():
        for c in weight_copies(expert, wslot):
            c.wait()

        @pl.when(next_expert >= 0)
        def _():
            for c in weight_copies(next_expert, 1 - wslot):
                c.start(priority=1)

        wg_b[...] = wg_f[wslot].astype(BF16)
        wu_b[...] = wu_f[wslot].astype(BF16)
        wd_b[...] = wd_f[wslot].astype(BF16)

    @pl.when(active)
    def _():
        for r in range(block):
            row_copy(0, lax.rem(b, GATHER_DEPTH), r).wait()
        x = xbuf[lax.rem(b, GATHER_DEPTH)].astype(BF16)
        g = jnp.dot(x, wg_b[...], preferred_element_type=F32)
        u = jnp.dot(x, wu_b[...], preferred_element_type=F32)
        h = (g / (1.0 + jnp.exp(-g)) * u).astype(BF16)
        ys_ref[...] = jnp.dot(h, wd_b[...], preferred_element_type=F32)

    @pl.when(jnp.logical_not(active))
    def _():
        ys_ref[...] = jnp.zeros_like(ys_ref)


def _experts(x1, inv, tab, nact, w_gate, w_up, w_down, block):
    n_blocks = tab.shape[1]
    d, ff = w_gate.shape[1], w_gate.shape[2]
    inv3 = inv.reshape(n_blocks, 1, block)
    ahead = lambda k: pl.BlockSpec((1, 1, block), lambda b, tb, na: (jnp.minimum(b + k, n_blocks - 1), 0, 0),
                                   memory_space=pltpu.SMEM)
    hbm = pl.BlockSpec(memory_space=pl.ANY)
    return pl.pallas_call(
        _expert_kernel,
        out_shape=jax.ShapeDtypeStruct((n_blocks * block, d), F32),
        grid_spec=pltpu.PrefetchScalarGridSpec(
            num_scalar_prefetch=2, grid=(n_blocks,),
            in_specs=[ahead(0), ahead(1), ahead(2), hbm, hbm, hbm, hbm],
            out_specs=pl.BlockSpec((block, d), lambda b, tb, na: (b, 0)),
            scratch_shapes=[pltpu.VMEM((GATHER_DEPTH, block, d), F32),
                            pltpu.VMEM((2, d, ff), F32), pltpu.VMEM((2, d, ff), F32), pltpu.VMEM((2, ff, d), F32),
                            pltpu.VMEM((d, ff), BF16), pltpu.VMEM((d, ff), BF16), pltpu.VMEM((ff, d), BF16),
                            pltpu.SemaphoreType.DMA((GATHER_DEPTH,)), pltpu.SemaphoreType.DMA((2, 3))]),
        compiler_params=_params(("arbitrary",)),
        name="moe_experts",
    )(tab, nact, inv3, inv3, inv3, x1, w_gate, w_up, w_down)


def _combine_kernel(dest_ref, next_ref, x1_ref, wts_ref, g_ref, b_ref, ys_ref, o_ref, buf, sem, *, alpha):
    i = pl.program_id(0)
    tm = x1_ref.shape[0]
    slot = lax.rem(i, 2)

    def row_copy(row, s, r, k):
        return pltpu.make_async_copy(ys_ref.at[pl.ds(row, 1)], buf.at[s, k, pl.ds(r, 1)], sem.at[s])

    def gather(idx_ref, dyn_slot):
        for s in range(2):
            @pl.when(dyn_slot == s)
            def _():
                for a0 in range(0, TOP_K * tm, DMA_GROUP):
                    rows = [idx_ref[0, 0, a] for a in range(a0, a0 + DMA_GROUP)]
                    for j, row in enumerate(rows):
                        r, k = divmod(a0 + j, TOP_K)
                        row_copy(row, s, r, k).start(priority=k % 2)

    @pl.when(i == 0)
    def _():
        gather(dest_ref, slot)

    @pl.when(i + 1 < pl.num_programs(0))
    def _():
        gather(next_ref, 1 - slot)

    for r in range(tm):
        for k in range(TOP_K):
            row_copy(0, slot, r, k).wait()
    w = wts_ref[...]
    ffn = w[:, 0:1] * buf[slot, 0] + w[:, 1:2] * buf[slot, 1]
    o_ref[...] = _layer_norm(alpha * x1_ref[...] + ffn, g_ref[...], b_ref[...])


def _combine(ys, dest3, x1, wts, g, b, alpha):
    t, d = x1.shape
    tm = dest3.shape[2] // TOP_K
    row = lambda c: pl.BlockSpec((tm, c), lambda i: (i, 0))
    full = lambda a: pl.BlockSpec(a.shape, lambda i: (0, 0))
    return pl.pallas_call(
        functools.partial(_combine_kernel, alpha=alpha),
        out_shape=jax.ShapeDtypeStruct((t, d), F32),
        grid=(t // tm,),
        in_specs=[pl.BlockSpec((1, 1, TOP_K * tm), lambda i: (i, 0, 0), memory_space=pltpu.SMEM),
                  pl.BlockSpec((1, 1, TOP_K * tm), lambda i: (jnp.minimum(i + 1, t // tm - 1), 0, 0),
                               memory_space=pltpu.SMEM),
                  row(d), row(LANES), full(g), full(b), pl.BlockSpec(memory_space=pl.ANY)],
        out_specs=row(d),
        scratch_shapes=[pltpu.VMEM((2, TOP_K, tm, d), F32), pltpu.SemaphoreType.DMA((2,))],
        compiler_params=_params(("arbitrary",)),
        name="moe_combine_ln",
    )(dest3, dest3, x1, wts, g, b, ys)


def _rope_tables(s):
    inv_freq = ROPE_THETA ** (-jnp.arange(0, HEAD_DIM, 2, dtype=F32) / HEAD_DIM)
    ang = jnp.arange(s, dtype=F32)[:, None] * inv_freq[None, :]
    cos, sin = jnp.cos(ang), jnp.sin(ang)
    return jnp.concatenate([cos, cos], axis=1), jnp.concatenate([-sin, sin], axis=1)


def _split_bf16_pair(w, width):
    hi = w.astype(BF16)
    lo = (w - hi.astype(F32)).astype(BF16)
    pad = lambda a: jnp.pad(a, ((0, 0), (0, width - a.shape[1])))
    return jnp.concatenate([pad(hi), pad(lo)], axis=1)


def _layer(x2, layer, depth, w_in, b_forget, lam_q1, lam_k1, lam_q2, lam_k2, diff_norm_g, w_proj_diff,
           w_proj_fox, w_out, ln1_g, ln1_b, w_rg, b_rg, w_re, b_re, w_gate, w_up, w_down, ln2_g, ln2_b):
    s, d = x2.shape
    alpha = (2 * depth) ** 0.25
    lam_init = 0.8 - 0.6 * math.exp(-0.3 * layer)
    n_diff, n_fox = diff_norm_g.shape[0], b_forget.shape[0]
    qk_cols = n_diff * 2 * HEAD_DIM
    lin_cols = n_diff * DIFF_V_DIM + 3 * n_fox * HEAD_DIM
    q_scale = HEAD_DIM ** -0.5 * LOG2E

    f0 = 2 * qk_cols + lin_cols
    w_t = w_in.T
    fcum, x_bf = _forget_cumsum(x2, w_t[f0:f0 + n_fox].T, b_forget)

    cos_t, sin_t = _rope_tables(s)
    tm = min(PROJ_TM, s)
    tab = pl.BlockSpec((tm, HEAD_DIM), lambda j, i: (i, 0))
    assert qk_cols == PROJ_TN, "column block 0 of the rotary call must be exactly the queries"
    qk = _proj(x_bf, w_t, 0, 2 * qk_cols, "rope", (cos_t, sin_t), (tab, tab), q_scale)
    col_scale = jnp.ones((1, lin_cols), F32).at[:, n_diff * DIFF_V_DIM:n_diff * DIFF_V_DIM + n_fox * HEAD_DIM].set(q_scale)
    lin = _proj(x_bf, w_t, 2 * qk_cols, lin_cols, "scale", (col_scale,),
                (pl.BlockSpec((1, min(PROJ_TN, lin_cols)), lambda j, i: (0, j)),))
    gates = _proj(x_bf, w_t, f0 + n_fox, w_in.shape[1] - f0 - n_fox, "sigmoid")

    lam_rows = jnp.stack([lam_q1, lam_k1, lam_q2, lam_k2]).astype(F32)
    n_maps = 2 * n_diff
    o_maps = _attention(qk, qk, lin, 0, n_maps, lambda h: h // 2, n_maps, DIFF_V_DIM)
    fq_col = n_diff * DIFF_V_DIM // HEAD_DIM
    o_fox = _attention(lin, lin, lin, fq_col, fq_col + n_fox, lambda h: fq_col + 2 * n_fox + h, n_fox, HEAD_DIM, fcum)

    merged = _merge(o_maps, o_fox, gates, lam_rows, diff_norm_g.astype(F32), w_proj_diff.astype(BF16),
                    w_proj_fox.astype(BF16), lam_init)
    w_router = jnp.concatenate([w_rg, jnp.moveaxis(w_re, 0, 1).reshape(d, N_EXPERTS)], axis=1)
    b_router = jnp.pad(jnp.concatenate([b_rg, b_re.reshape(N_EXPERTS)]).astype(F32),
                       (0, LANES - N_GROUPS - N_EXPERTS)).reshape(1, LANES)
    x1, ids, wts = _out_ln_route(merged, x2, w_out.astype(BF16), ln1_g.reshape(1, d), ln1_b.reshape(1, d),
                                 _split_bf16_pair(w_router, LANES), b_router, alpha)

    block = MOE_BLOCK
    n_blocks = -(-(s * TOP_K) // block) + N_EXPERTS
    dest, bexp, meta = _rank(ids, block, n_blocks)
    tmr = min(ROW_TM, s)
    dest2 = dest[:, :TOP_K]
    inv = _invert(dest2.reshape(s * TOP_K), n_blocks * block)
    ys = _experts(x1, inv, bexp[:, :3].T, meta[0, :1], w_gate, w_up, w_down, block)
    dest3 = dest2.reshape(s // tmr, 1, TOP_K * tmr)
    return _combine(ys, dest3, x1, wts, ln2_g.reshape(1, d), ln2_b.reshape(1, d), alpha)


def kernel(x, w_in, b_forget, lam_q1, lam_k1, lam_q2, lam_k2, diff_norm_g, w_proj_diff, w_proj_fox, w_out,
           ln1_g, ln1_b, w_router_group, b_router_group, w_router_expert, b_router_expert, w_gate, w_up,
           w_down, ln2_g, ln2_b):
    batch, s, d = x.shape
    depth = w_in.shape[0]
    params = (w_in, b_forget, lam_q1, lam_k1, lam_q2, lam_k2, diff_norm_g, w_proj_diff, w_proj_fox, w_out,
              ln1_g, ln1_b, w_router_group, b_router_group, w_router_expert, b_router_expert, w_gate, w_up,
              w_down, ln2_g, ln2_b)
    outs = []
    for bi in range(batch):
        h = x[bi]
        for layer in range(depth):
            h = _layer(h, layer, depth, *(p[layer] for p in params))
        outs.append(h)
    return jnp.stack(outs)
```

```python
import functools
import math

import jax
import jax.numpy as jnp
from jax import lax
from jax.experimental import pallas as pl
from jax.experimental.pallas import tpu as pltpu

F32, BF16, I32 = jnp.float32, jnp.bfloat16, jnp.int32

HEAD_DIM = 128
DIFF_V_DIM = 2 * HEAD_DIM
ROPE_THETA = 10000.0
N_GROUPS = 4
EXPERTS_PER_GROUP = 8
N_EXPERTS = N_GROUPS * EXPERTS_PER_GROUP
TOP_K = 2
LN_EPS = 1e-5
LOG2E = 1.4426950408889634
NEG = -1e30
LANES = 128
SUBLANES = 8
VMEM_LIMIT = 56 * 1024 * 1024

MOE_BLOCK = 128
PROJ_TM, PROJ_TN = 1024, 1024
PROJ_CAST_ROWS = 256
ATT_TQ = 512
ONES_ROWS = 16
FOX_UNROLL, DIFF_UNROLL = 16, 8
FORGET_TS = 512
MERGE_TM = 512
OUTLN_TM = 512
RANK_TB = 1024
ROW_TM = 256
INVERT_CHUNK = 1024
GATHER_DEPTH = 3
DMA_GROUP = 16


def _params(sem, vmem=VMEM_LIMIT):
    return pltpu.CompilerParams(dimension_semantics=sem, vmem_limit_bytes=vmem)


def _proj_kernel(x_ref, wt_hbm, *rest, epilogue, row0, q_scale):
    *extra, o_ref, w_f32, w_bf, sem = rest
    j, i = pl.program_id(0), pl.program_id(1)
    tn = w_bf.shape[0]
    slot = lax.rem(j, 2)

    def fetch(jj, s):
        rows = pl.ds(pl.multiple_of(row0 + jj * tn, SUBLANES), tn)
        return pltpu.make_async_copy(wt_hbm.at[rows, :], w_f32.at[s], sem.at[s])

    @pl.when(i == 0)
    def _():
        @pl.when(j == 0)
        def _():
            fetch(j, slot).start()

        fetch(j, slot).wait()

        @pl.when(j + 1 < pl.num_programs(0))
        def _():
            fetch(j + 1, 1 - slot).start()

        rows = min(PROJ_CAST_ROWS, tn)

        def chunk(c, carry):
            r = pl.ds(pl.multiple_of(c * rows, rows), rows)
            w_bf[r, :] = w_f32[slot, r, :].astype(BF16)
            return carry

        lax.fori_loop(0, tn // rows, chunk, 0)

    acc = lax.dot_general(x_ref[...], w_bf[...], (((1,), (1,)), ((), ())), preferred_element_type=F32)
    if epilogue == "rope":
        cos_ref, sin_ref = extra
        scale = jnp.where(j == 0, q_scale, 1.0)
        cosf, sinf = cos_ref[...] * scale, sin_ref[...] * scale
        for h in range(tn // HEAD_DIM):
            t = acc[:, h * HEAD_DIM:(h + 1) * HEAD_DIM]
            o_ref[:, h * HEAD_DIM:(h + 1) * HEAD_DIM] = (
                t * cosf + pltpu.roll(t, HEAD_DIM // 2, 1) * sinf).astype(o_ref.dtype)
    elif epilogue == "scale":
        o_ref[...] = (acc * extra[0][...]).astype(o_ref.dtype)
    else:
        o_ref[...] = (1.0 / (1.0 + jnp.exp(-acc))).astype(o_ref.dtype)


def _proj(x_bf, w_t, col0, n, epilogue, extra=(), extra_specs=(), q_scale=1.0):
    m, k = x_bf.shape
    tm, tn = min(PROJ_TM, m), min(PROJ_TN, n)
    assert col0 % SUBLANES == 0 and n % tn == 0
    return pl.pallas_call(
        functools.partial(_proj_kernel, epilogue=epilogue, row0=col0, q_scale=q_scale),
        out_shape=jax.ShapeDtypeStruct((m, n), BF16),
        grid=(n // tn, m // tm),
        in_specs=[pl.BlockSpec((tm, k), lambda j, i: (i, 0)), pl.BlockSpec(memory_space=pl.ANY), *extra_specs],
        out_specs=pl.BlockSpec((tm, tn), lambda j, i: (i, j)),
        scratch_shapes=[pltpu.VMEM((2, tn, k), F32), pltpu.VMEM((tn, k), BF16), pltpu.SemaphoreType.DMA((2,))],
        compiler_params=_params(("arbitrary", "arbitrary")),
        name=f"in_proj_{epilogue}",
    )(x_bf, w_t, *extra)


def _split3(v):
    h = v.astype(BF16)
    r = v - h.astype(F32)
    m = r.astype(BF16)
    return h, m, (r - m.astype(F32)).astype(BF16)


def _forget_kernel(x_ref, w_ref, b_ref, o_ref, xbf_ref, carry_ref):
    i = pl.program_id(0)

    @pl.when(i == 0)
    def _():
        carry_ref[...] = jnp.zeros_like(carry_ref)

    x = x_ref[...]
    xh = x.astype(BF16)
    xbf_ref[...] = xh
    xl = (x - xh.astype(F32)).astype(BF16)
    w = w_ref[...]
    a = jnp.dot(xh, w, preferred_element_type=F32)
    b = jnp.dot(xl, w[:, :LANES], preferred_element_type=F32)
    z = a[:, :LANES] + a[:, LANES:] + b + b_ref[...]
    logf = jnp.minimum(z, 0.0) - jnp.log1p(jnp.exp(-jnp.abs(z)))
    ts = logf.shape[0]
    tri = (lax.broadcasted_iota(I32, (ts, ts), 1) <= lax.broadcasted_iota(I32, (ts, ts), 0)).astype(BF16)
    c = carry_ref[0:1, :]
    for piece in _split3(logf):
        c = c + jnp.dot(tri, piece, preferred_element_type=F32)
    o_ref[...] = c * LOG2E
    carry_ref[...] = jnp.broadcast_to(c[ts - 1:ts, :], carry_ref.shape)


def _forget_cumsum(x, w_f, b_f):
    s, d = x.shape
    nh = w_f.shape[1]
    wh = w_f.astype(BF16)
    wl = (w_f - wh.astype(F32)).astype(BF16)
    pad = lambda a: jnp.pad(a, ((0, 0), (0, LANES - nh)))
    w2 = jnp.concatenate([pad(wh), pad(wl)], axis=1)
    b2 = jnp.pad(b_f.astype(F32), (0, LANES - nh)).reshape(1, LANES)
    ts = min(FORGET_TS, s)
    return pl.pallas_call(
        _forget_kernel,
        out_shape=(jax.ShapeDtypeStruct((s, LANES), F32), jax.ShapeDtypeStruct((s, d), BF16)),
        grid=(s // ts,),
        in_specs=[pl.BlockSpec((ts, d), lambda i: (i, 0)),
                  pl.BlockSpec((d, 2 * LANES), lambda i: (0, 0)),
                  pl.BlockSpec((1, LANES), lambda i: (0, 0))],
        out_specs=(pl.BlockSpec((ts, LANES), lambda i: (i, 0)), pl.BlockSpec((ts, d), lambda i: (i, 0))),
        scratch_shapes=[pltpu.VMEM((8, LANES), F32)],
        compiler_params=_params(("arbitrary",)),
        name="forget_cumsum",
    )(x, w2, b2)


def _pipeline3(n, unroll, stage_a, stage_b, stage_c, carry):
    if n < 3:
        for i in range(n):
            carry, post = stage_b(i, i % 2, i % unroll, stage_a(i, i % 2), carry)
            stage_c(i, i % unroll, post)
        return carry

    def step(i, m, state, do_a=True):
        aux, post_old, post_new, carry = state
        stage_c(i - 2, (m - 2) % unroll, post_old)
        aux_next = stage_a(i + 1, (m + 1) % 2) if do_a else aux
        carry, post = stage_b(i, m % 2, m, aux, carry)
        return aux_next, post_new, post, carry

    aux0 = stage_a(0, 0)
    aux1 = stage_a(1, 1)
    carry, post0 = stage_b(0, 0, 0, aux0, carry)
    aux2 = stage_a(2, 0)
    carry, post1 = stage_b(1, 1, 1, aux1, carry)
    state = (aux2, post0, post1, carry)
    mid = n - 3

    def unrolled(h, s):
        for u in range(unroll):
            s = step(unroll * h + 2 + u, (2 + u) % unroll, s)
        return s

    state = lax.fori_loop(0, mid // unroll, unrolled, state)
    for i in range(mid - mid % unroll + 2, n - 1):
        state = step(i, i % unroll, state)
    _, post_old, post_new, carry = step(n - 1, (n - 1) % unroll, state, do_a=False)
    stage_c(n - 2, (n - 2) % unroll, post_old)
    stage_c(n - 1, (n - 1) % unroll, post_new)
    return carry


def _attn_kernel(qi_tab, kb_tab, first_tab, q_ref, k_ref, v_ref, *rest, t, has_bias, unroll, v_share):
    if has_bias:
        c_ref, o_ref, kaug, qt, vt, st0, st1, *p, acc_all, m_all = rest
    else:
        o_ref, qt, vt, st0, st1, *p, acc_all, m_all = rest
        kaug = k_ref
    st = (st0, st1)
    seq, dv = v_ref.shape
    nq = seq // t
    dq = q_ref.shape[1]
    blk = lambda b: pl.ds(pl.multiple_of(b * t, t), t)
    head = pl.program_id(0)

    def prep(c, carry):
        rows = blk(c)
        lane = lax.broadcasted_iota(I32, (t, LANES), 1)
        if has_bias:
            neg_c = -jnp.sum(jnp.where(lane == head, c_ref[rows, :], 0.0), axis=1, keepdims=True)
            hi, mid, lo = (x.astype(F32) for x in _split3(neg_c))
            aug = jnp.where(lane == 0, hi, jnp.where(lane == 1, mid, jnp.where(lane == 2, lo, 0.0)))
            kaug[rows, :dq] = k_ref[rows, :]
            kaug[rows, dq:] = aug.astype(BF16)
            ones = (lax.broadcasted_iota(I32, (LANES, t), 0) < 3).astype(F32)
            qt[:, rows] = jnp.concatenate([q_ref[rows, :].T, ones.astype(BF16)], axis=0)
        else:
            qt[:, rows] = q_ref[rows, :].T
        return carry

    def prep_values(c, carry):
        one_row = (lax.broadcasted_iota(I32, (ONES_ROWS, t), 0) == 0).astype(BF16)
        vt[:, blk(c)] = jnp.concatenate([v_ref[blk(c), :].T, one_row], axis=0)
        return carry

    lax.fori_loop(0, nq, prep, 0)

    @pl.when(head % v_share == 0)
    def _():
        lax.fori_loop(0, nq, prep_values, 0)

    def scores(qi, kb):
        return jnp.dot(kaug[blk(kb), :], qt[:, blk(qi)], preferred_element_type=F32)

    def values(kb, slot):
        return jnp.dot(vt[:, blk(kb)], p[slot][...], preferred_element_type=F32)

    def save_max(qi, m):
        m_all[qi] = jnp.broadcast_to(m, m_all.shape[1:])

    def diag_scores(i, slot):
        mask = lax.broadcasted_iota(I32, (t, t), 1) >= lax.broadcasted_iota(I32, (t, t), 0)
        s = jnp.where(mask, scores(i, i), NEG)
        st[slot][...] = s
        return jnp.max(s, axis=0, keepdims=True)

    def diag_softmax(i, slot, pslot, cmax, carry):
        p[pslot][...] = jnp.exp2(st[slot][...] - cmax).astype(BF16)
        save_max(i, cmax)
        return carry, None

    def diag_values(i, slot, _):
        acc_all[i] = values(i, slot)

    _pipeline3(nq, unroll, diag_scores, diag_softmax, diag_values, 0)

    n_low = nq * (nq - 1) // 2
    if n_low:
        last = n_low - 1
        tab = lambda ref, f: ref[jnp.minimum(f, last)]

        def low_scores(f, slot):
            s = scores(tab(qi_tab, f), tab(kb_tab, f))
            st[slot][...] = s
            return jnp.max(s, axis=0, keepdims=True)

        def low_softmax(f, slot, pslot, cmax, m):
            qi = tab(qi_tab, f)
            m_prev = jnp.where(tab(first_tab, f) == 1, m_all[qi][0:1], m)
            m_new = jnp.maximum(m_prev, cmax)
            p[pslot][...] = jnp.exp2(st[slot][...] - m_new).astype(BF16)
            return m_new, jnp.exp2(m_prev - m_new)

        def low_values(f, slot, a):
            qi = tab(qi_tab, f)
            acc_all[qi] = a * acc_all[qi] + values(tab(kb_tab, f), slot)

        _pipeline3(n_low, unroll, low_scores, low_softmax, low_values, jnp.full((1, t), NEG, F32))

    def finish(qi, carry):
        acc = acc_all[qi]
        o_ref[blk(qi), :] = (acc[:dv] / acc[dv:dv + 1]).T.astype(o_ref.dtype)
        return carry

    lax.fori_loop(0, nq, finish, 0)


def _attention(q_src, k_src, v_src, q_col, k_col, v_col, n_heads, dv, fcum=None, v_share=1):
    s = q_src.shape[0]
    t = min(ATT_TQ, s)
    nq = s // t
    pairs = [(qi, kb) for qi in range(nq) for kb in range(qi)] or [(0, 0)]
    qi_tab = jnp.array([a for a, _ in pairs], I32)
    kb_tab = jnp.array([b for _, b in pairs], I32)
    first_tab = jnp.array([int(b == 0) for _, b in pairs], I32)
    has_bias = fcum is not None
    dk = 2 * HEAD_DIM if has_bias else HEAD_DIM
    unroll = FOX_UNROLL if has_bias else DIFF_UNROLL
    in_specs = [pl.BlockSpec((s, HEAD_DIM), lambda h, *_: (0, q_col + h)),
                pl.BlockSpec((s, HEAD_DIM), lambda h, *_: (0, k_col + h)),
                pl.BlockSpec((s, dv), lambda h, *_: (0, v_col(h)))]
    args = [q_src, k_src, v_src]
    scratch = []
    if has_bias:
        in_specs.append(pl.BlockSpec((s, LANES), lambda h, *_: (0, 0)))
        args.append(fcum)
        scratch.append(pltpu.VMEM((s, dk), BF16))
    scratch += [pltpu.VMEM((dk, s), BF16), pltpu.VMEM((dv + ONES_ROWS, s), BF16),
                pltpu.VMEM((t, t), F32), pltpu.VMEM((t, t), F32), *[pltpu.VMEM((t, t), BF16)] * unroll,
                pltpu.VMEM((nq, dv + ONES_ROWS, t), F32), pltpu.VMEM((nq, SUBLANES, t), F32)]
    return pl.pallas_call(
        functools.partial(_attn_kernel, t=t, has_bias=has_bias, unroll=unroll, v_share=v_share),
        out_shape=jax.ShapeDtypeStruct((s, n_heads * dv), BF16),
        grid_spec=pltpu.PrefetchScalarGridSpec(
            num_scalar_prefetch=3, grid=(n_heads,), in_specs=in_specs,
            out_specs=pl.BlockSpec((s, dv), lambda h, *_: (0, h)), scratch_shapes=scratch),
        compiler_params=_params(("arbitrary",)),
        name="fox_attention" if has_bias else "diff_attention",
    )(qi_tab, kb_tab, first_tab, *args)


def _merge_kernel(om_ref, of_ref, gd_ref, gf_ref, lam_ref, g_ref, wd_ref, wf_ref, o_ref, *, lam_init):
    lam_v = lam_ref[...]
    lam = (jnp.exp(jnp.sum(lam_v[0:1] * lam_v[1:2], axis=1, keepdims=True))
           - jnp.exp(jnp.sum(lam_v[2:3] * lam_v[3:4], axis=1, keepdims=True)) + lam_init)
    heads = []
    for h in range(g_ref.shape[0]):
        o1 = om_ref[:, (2 * h) * DIFF_V_DIM:(2 * h + 1) * DIFF_V_DIM].astype(F32)
        o2 = om_ref[:, (2 * h + 1) * DIFF_V_DIM:(2 * h + 2) * DIFF_V_DIM].astype(F32)
        o = o1 - lam * o2
        o = o * lax.rsqrt(jnp.mean(o * o, axis=1, keepdims=True) + LN_EPS) * g_ref[h:h + 1, :] * (1.0 - lam_init)
        heads.append(o.astype(BF16))
    ud = jnp.dot(jnp.concatenate(heads, axis=1), wd_ref[...], preferred_element_type=F32)
    uf = jnp.dot(of_ref[...], wf_ref[...], preferred_element_type=F32)
    o_ref[...] = (gd_ref[...].astype(F32) * ud + gf_ref[...].astype(F32) * uf).astype(o_ref.dtype)


def _merge(o_maps, o_fox, gates, lam_rows, norm_g, wd, wf, lam_init):
    s, d = o_maps.shape[0], wd.shape[1]
    tm = min(MERGE_TM, s)
    row = lambda c: pl.BlockSpec((tm, c), lambda i: (i, 0))
    full = lambda a: pl.BlockSpec(a.shape, lambda i: (0, 0))
    return pl.pallas_call(
        functools.partial(_merge_kernel, lam_init=lam_init),
        out_shape=jax.ShapeDtypeStruct((s, d), BF16),
        grid=(s // tm,),
        in_specs=[row(o_maps.shape[1]), row(o_fox.shape[1]),
                  pl.BlockSpec((tm, d), lambda i: (i, 0)), pl.BlockSpec((tm, d), lambda i: (i, 1)),
                  full(lam_rows), full(norm_g), full(wd), full(wf)],
        out_specs=row(d),
        compiler_params=_params(("parallel",)),
        name="merge_branches",
    )(o_maps, o_fox, gates, gates, lam_rows, norm_g, wd, wf)


def _layer_norm(y, g, b):
    mu = jnp.mean(y, axis=1, keepdims=True)
    yc = y - mu
    var = jnp.mean(yc * yc, axis=1, keepdims=True)
    return yc * lax.rsqrt(var + LN_EPS) * g + b


def _route(lg):
    lane = lax.broadcasted_iota(I32, lg.shape, 1)
    far = jnp.int32(4 * LANES)
    is_g = lane < N_GROUPS
    gl = jnp.where(is_g, lg, NEG)
    gmax = jnp.max(gl, axis=1, keepdims=True)
    gidx = jnp.min(jnp.where(gl == gmax, lane, far), axis=1, keepdims=True)
    top_gp = 1.0 / jnp.sum(jnp.where(is_g, jnp.exp(gl - gmax), 0.0), axis=1, keepdims=True)
    lo = N_GROUPS + gidx * EXPERTS_PER_GROUP
    el = jnp.where((lane >= lo) & (lane < lo + EXPERTS_PER_GROUP), lg, NEG)
    m1 = jnp.max(el, axis=1, keepdims=True)
    i1 = jnp.min(jnp.where(el == m1, lane, far), axis=1, keepdims=True)
    el2 = jnp.where(lane == i1, 2.0 * NEG, el)
    m2 = jnp.max(el2, axis=1, keepdims=True)
    i2 = jnp.min(jnp.where(el2 == m2, lane, far), axis=1, keepdims=True)
    d = jnp.exp(m2 - m1)
    w1 = top_gp / (1.0 + d)
    w2 = w1 * d
    ids = jnp.where(lane == 0, i1 - N_GROUPS, jnp.where(lane == 1, i2 - N_GROUPS, 0))
    wts = jnp.where(lane == 0, w1, jnp.where(lane == 1, w2, 0.0))
    return ids, wts


def _outln_kernel(mg_ref, x_ref, wo_ref, g_ref, b_ref, wr_ref, br_ref, x1_ref, ids_ref, wts_ref, *, alpha):
    mix = jnp.dot(mg_ref[...], wo_ref[...], preferred_element_type=F32)
    x1 = _layer_norm(alpha * x_ref[...] + mix, g_ref[...], b_ref[...])
    x1_ref[...] = x1
    xh = x1.astype(BF16)
    xl = (x1 - xh.astype(F32)).astype(BF16)
    wr = wr_ref[...]
    a = jnp.dot(xh, wr, preferred_element_type=F32)
    b = jnp.dot(xl, wr[:, :LANES], preferred_element_type=F32)
    ids, wts = _route(a[:, :LANES] + a[:, LANES:] + b + br_ref[...])
    ids_ref[...] = ids
    wts_ref[...] = wts


def _out_ln_route(merged, x, w_out, g, b, wr2, br, alpha):
    s, d = x.shape
    tm = min(OUTLN_TM, s)
    row = lambda c: pl.BlockSpec((tm, c), lambda i: (i, 0))
    full = lambda a: pl.BlockSpec(a.shape, lambda i: (0, 0))
    return pl.pallas_call(
        functools.partial(_outln_kernel, alpha=alpha),
        out_shape=(jax.ShapeDtypeStruct((s, d), F32), jax.ShapeDtypeStruct((s, LANES), I32),
                   jax.ShapeDtypeStruct((s, LANES), F32)),
        grid=(s // tm,),
        in_specs=[row(d), row(d), full(w_out), full(g), full(b), full(wr2), full(br)],
        out_specs=(row(d), row(LANES), row(LANES)),
        compiler_params=_params(("parallel",)),
        name="out_proj_ln_route",
    )(merged, x, w_out, g, b, wr2, br)


def _rank_kernel(ids_ref, dest_ref, bexp_ref, meta_ref, cnt_ref, carry_ref, start_ref, *, block, n_blocks):
    ph, i = pl.program_id(0), pl.program_id(1)
    ids = ids_ref[...]
    tb = ids.shape[0]
    lane = lax.broadcasted_iota(I32, (tb, LANES), 1)
    oh1 = (lane == ids[:, 0:1]).astype(F32)
    oh2 = (lane == ids[:, 1:2]).astype(F32)
    oh = oh1 + oh2
    colsum = jnp.sum(oh, axis=0, keepdims=True)

    @pl.when((ph == 0) & (i == 0))
    def _():
        cnt_ref[...] = jnp.zeros_like(cnt_ref)

    @pl.when(ph == 0)
    def _():
        cnt_ref[...] += jnp.broadcast_to(colsum, cnt_ref.shape)

    @pl.when((ph == 1) & (i == 0))
    def _():
        cnt = cnt_ref[0:1, :]
        nblk = jnp.floor((cnt + (block - 1)) * (1.0 / block))
        r = lax.broadcasted_iota(I32, (LANES, LANES), 0)
        c = lax.broadcasted_iota(I32, (LANES, LANES), 1)
        upper = (r < c).astype(BF16)
        nb8 = jnp.broadcast_to(nblk, (8, LANES)).astype(BF16)
        bstart = jnp.dot(nb8, upper, preferred_element_type=F32)
        start_ref[...] = bstart * block
        carry_ref[...] = jnp.zeros_like(carry_ref)
        lane1 = lax.broadcasted_iota(I32, (1, LANES), 1)
        bend = jnp.where(lane1 < N_EXPERTS, bstart[0:1] + nblk, 4.0 * n_blocks)
        bidx = lax.broadcasted_iota(I32, (n_blocks, LANES), 0).astype(F32)
        be = jnp.sum((jnp.broadcast_to(bend, (n_blocks, LANES)) <= bidx).astype(F32), axis=1, keepdims=True)
        be = jnp.minimum(be, N_EXPERTS - 1.0)
        lane_b = lax.broadcasted_iota(I32, (n_blocks, LANES), 1).astype(F32)
        owns = jnp.broadcast_to((nblk > 0.0) & (lane1 < N_EXPERTS), (n_blocks, LANES))
        nxt = jnp.min(jnp.where(owns & (lane_b > be), lane_b, 1.0 * LANES), axis=1, keepdims=True)
        nxt = jnp.where(nxt >= 1.0 * LANES, -1.0, nxt)
        before = jnp.sum((owns & (lane_b < be)).astype(F32), axis=1, keepdims=True)
        parity = before - 2.0 * jnp.floor(before * 0.5)
        bexp_ref[...] = jnp.where(lane_b == 0.0, be, jnp.where(lane_b == 1.0, nxt, parity)).astype(I32)
        total = jnp.sum(jnp.where(lane1 < N_EXPERTS, nblk, 0.0), axis=1, keepdims=True)
        row = lax.broadcasted_iota(I32, (8, LANES), 0)
        first_pad = start_ref[0:1, :] + cnt
        meta = jnp.where(row == 0, jnp.broadcast_to(total, (8, LANES)),
                         jnp.where(row == 1, jnp.broadcast_to(first_pad, (8, LANES)),
                                   jnp.broadcast_to(bstart[0:1] * block + nblk * block, (8, LANES))))
        meta_ref[...] = meta.astype(I32)

    @pl.when(ph == 1)
    def _():
        rr = lax.broadcasted_iota(I32, (tb, tb), 0)
        cc = lax.broadcasted_iota(I32, (tb, tb), 1)
        lower = (cc < rr).astype(BF16)
        prefix = jnp.dot(lower, oh.astype(BF16), preferred_element_type=F32)
        pos = prefix + carry_ref[0:1, :] + start_ref[0:1, :]
        d1 = jnp.sum(pos * oh1, axis=1, keepdims=True)
        d2 = jnp.sum(pos * oh2, axis=1, keepdims=True)
        dest_ref[...] = jnp.where(lane == 0, d1, jnp.where(lane == 1, d2, 0.0)).astype(I32)
        carry_ref[...] += jnp.broadcast_to(colsum, carry_ref.shape)


def _rank(ids, block, n_blocks):
    t = ids.shape[0]
    tb = min(RANK_TB, t)
    return pl.pallas_call(
        functools.partial(_rank_kernel, block=block, n_blocks=n_blocks),
        out_shape=(jax.ShapeDtypeStruct((t, LANES), I32), jax.ShapeDtypeStruct((n_blocks, LANES), I32),
                   jax.ShapeDtypeStruct((8, LANES), I32)),
        grid=(2, t // tb),
        in_specs=[pl.BlockSpec((tb, LANES), lambda p, i: (i, 0))],
        out_specs=(pl.BlockSpec((tb, LANES), lambda p, i: (i * p, 0)),
                   pl.BlockSpec((n_blocks, LANES), lambda p, i: (0, 0)),
                   pl.BlockSpec((8, LANES), lambda p, i: (0, 0))),
        scratch_shapes=[pltpu.VMEM((8, LANES), F32)] * 3,
        compiler_params=_params(("arbitrary", "arbitrary")),
        name="moe_rank",
    )(ids)


def _invert_kernel(dest_ref, zeros_hbm, inv_hbm, inv_ref, sem):
    i = pl.program_id(0)
    n = dest_ref.shape[0]

    @pl.when(i == 0)
    def _():
        fill = pltpu.make_async_copy(zeros_hbm, inv_ref, sem)
        fill.start()
        fill.wait()

    def body(j, c):
        inv_ref[dest_ref[j]] = i * n + j
        return c

    lax.fori_loop(0, n, body, 0, unroll=8)

    @pl.when(i == pl.num_programs(0) - 1)
    def _():
        flush = pltpu.make_async_copy(inv_ref, inv_hbm, sem)
        flush.start()
        flush.wait()


def _invert(dest_flat, rows):
    a = dest_flat.shape[0]
    chunk = min(INVERT_CHUNK, a)
    return pl.pallas_call(
        _invert_kernel,
        out_shape=jax.ShapeDtypeStruct((rows,), I32),
        grid=(a // chunk,),
        in_specs=[pl.BlockSpec((chunk,), lambda i: (i,), memory_space=pltpu.SMEM),
                  pl.BlockSpec(memory_space=pl.ANY)],
        out_specs=pl.BlockSpec(memory_space=pl.ANY),
        scratch_shapes=[pltpu.SMEM((rows,), I32), pltpu.SemaphoreType.DMA],
        compiler_params=_params(("arbitrary",)),
        name="moe_invert",
    )(dest_flat, jnp.zeros((rows,), I32))


def _expert_kernel(tab_ref, nact_ref, inv0_ref, inv1_ref, inv2_ref, x1_ref, wg_hbm, wu_hbm, wd_hbm, ys_ref,
                   xbuf, wg_f, wu_f, wd_f, wg_b, wu_b, wd_b, gsem, wsem):
    b = pl.program_id(0)
    n_active = nact_ref[0]
    active = b < n_active
    block = xbuf.shape[1]
    expert, next_expert, wslot = tab_ref[0, b], tab_ref[1, b], tab_ref[2, b]
    first_of_expert = (b == 0) | (tab_ref[0, jnp.maximum(b - 1, 0)] != expert)

    def row_copy(token, slot, r):
        return pltpu.make_async_copy(x1_ref.at[pl.ds(token, 1)], xbuf.at[slot, pl.ds(r, 1)], gsem.at[slot])

    def gather(inv_ref, blk):
        for slot in range(GATHER_DEPTH):
            @pl.when(lax.rem(blk, GATHER_DEPTH) == slot)
            def _():
                for r0 in range(0, block, DMA_GROUP):
                    tokens = [lax.shift_right_logical(inv_ref[0, 0, r], 1) for r in range(r0, r0 + DMA_GROUP)]
                    for j, token in enumerate(tokens):
                        row_copy(token, slot, r0 + j).start()

    def weight_copies(e, s):
        return (pltpu.make_async_copy(wg_hbm.at[e], wg_f.at[s], wsem.at[s, 0]),
                pltpu.make_async_copy(wu_hbm.at[e], wu_f.at[s], wsem.at[s, 1]),
                pltpu.make_async_copy(wd_hbm.at[e], wd_f.at[s], wsem.at[s, 2]))

    @pl.when(b == 0)
    def _():
        for c in weight_copies(expert, wslot):
            c.start(priority=1)
        gather(inv0_ref, b)

    @pl.when((b == 0) & (n_active > 1))
    def _():
        gather(inv1_ref, b + 1)

    @pl.when(b + 2 < n_active)
    def _():
        gather(inv2_ref, b + 2)

    @pl.when(active & first_of_expert)
    def _():
        for c in weight_copies(expert, wslot):
            c.wait()

        @pl.when(next_expert >= 0)
        def _():
            for c in weight_copies(next_expert, 1 - wslot):
                c.start(priority=1)

        wg_b[...] = wg_f[wslot].astype(BF16)
        wu_b[...] = wu_f[wslot].astype(BF16)
        wd_b[...] = wd_f[wslot].astype(BF16)

    @pl.when(active)
    def _():
        for r in range(block):
            row_copy(0, lax.rem(b, GATHER_DEPTH), r).wait()
        x = xbuf[lax.rem(b, GATHER_DEPTH)].astype(BF16)
        g = jnp.dot(x, wg_b[...], preferred_element_type=F32)
        u = jnp.dot(x, wu_b[...], preferred_element_type=F32)
        h = (g / (1.0 + jnp.exp(-g)) * u).astype(BF16)
        ys_ref[...] = jnp.dot(h, wd_b[...], preferred_element_type=F32)

    @pl.when(jnp.logical_not(active))
    def _():
        ys_ref[...] = jnp.zeros_like(ys_ref)


def _experts(x1, inv, tab, nact, w_gate, w_up, w_down, block):
    n_blocks = tab.shape[1]
    d, ff = w_gate.shape[1], w_gate.shape[2]
    inv3 = inv.reshape(n_blocks, 1, block)
    ahead = lambda k: pl.BlockSpec((1, 1, block), lambda b, tb, na: (jnp.minimum(b + k, n_blocks - 1), 0, 0),
                                   memory_space=pltpu.SMEM)
    hbm = pl.BlockSpec(memory_space=pl.ANY)
    return pl.pallas_call(
        _expert_kernel,
        out_shape=jax.ShapeDtypeStruct((n_blocks * block, d), F32),
        grid_spec=pltpu.PrefetchScalarGridSpec(
            num_scalar_prefetch=2, grid=(n_blocks,),
            in_specs=[ahead(0), ahead(1), ahead(2), hbm, hbm, hbm, hbm],
            out_specs=pl.BlockSpec((block, d), lambda b, tb, na: (b, 0)),
            scratch_shapes=[pltpu.VMEM((GATHER_DEPTH, block, d), F32),
                            pltpu.VMEM((2, d, ff), F32), pltpu.VMEM((2, d, ff), F32), pltpu.VMEM((2, ff, d), F32),
                            pltpu.VMEM((d, ff), BF16), pltpu.VMEM((d, ff), BF16), pltpu.VMEM((ff, d), BF16),
                            pltpu.SemaphoreType.DMA((GATHER_DEPTH,)), pltpu.SemaphoreType.DMA((2, 3))]),
        compiler_params=_params(("arbitrary",)),
        name="moe_experts",
    )(tab, nact, inv3, inv3, inv3, x1, w_gate, w_up, w_down)


def _combine_kernel(dest_ref, next_ref, x1_ref, wts_ref, g_ref, b_ref, ys_ref, o_ref, buf, sem, *, alpha):
    i = pl.program_id(0)
    tm = x1_ref.shape[0]
    slot = lax.rem(i, 2)

    def row_copy(row, s, r, k):
        return pltpu.make_async_copy(ys_ref.at[pl.ds(row, 1)], buf.at[s, k, pl.ds(r, 1)], sem.at[s])

    def gather(idx_ref, dyn_slot):
        for s in range(2):
            @pl.when(dyn_slot == s)
            def _():
                for a0 in range(0, TOP_K * tm, DMA_GROUP):
                    rows = [idx_ref[0, 0, a] for a in range(a0, a0 + DMA_GROUP)]
                    for j, row in enumerate(rows):
                        r, k = divmod(a0 + j, TOP_K)
                        row_copy(row, s, r, k).start(priority=k % 2)

    @pl.when(i == 0)
    def _():
        gather(dest_ref, slot)

    @pl.when(i + 1 < pl.num_programs(0))
    def _():
        gather(next_ref, 1 - slot)

    for r in range(tm):
        for k in range(TOP_K):
            row_copy(0, slot, r, k).wait()
    w = wts_ref[...]
    ffn = w[:, 0:1] * buf[slot, 0] + w[:, 1:2] * buf[slot, 1]
    o_ref[...] = _layer_norm(alpha * x1_ref[...] + ffn, g_ref[...], b_ref[...])


def _combine(ys, dest3, x1, wts, g, b, alpha):
    t, d = x1.shape
    tm = dest3.shape[2] // TOP_K
    row = lambda c: pl.BlockSpec((tm, c), lambda i: (i, 0))
    full = lambda a: pl.BlockSpec(a.shape, lambda i: (0, 0))
    return pl.pallas_call(
        functools.partial(_combine_kernel, alpha=alpha),
        out_shape=jax.ShapeDtypeStruct((t, d), F32),
        grid=(t // tm,),
        in_specs=[pl.BlockSpec((1, 1, TOP_K * tm), lambda i: (i, 0, 0), memory_space=pltpu.SMEM),
                  pl.BlockSpec((1, 1, TOP_K * tm), lambda i: (jnp.minimum(i + 1, t // tm - 1), 0, 0),
                               memory_space=pltpu.SMEM),
                  row(d), row(LANES), full(g), full(b), pl.BlockSpec(memory_space=pl.ANY)],
        out_specs=row(d),
        scratch_shapes=[pltpu.VMEM((2, TOP_K, tm, d), F32), pltpu.SemaphoreType.DMA((2,))],
        compiler_params=_params(("arbitrary",)),
        name="moe_combine_ln",
    )(dest3, dest3, x1, wts, g, b, ys)


def _rope_tables(s):
    inv_freq = ROPE_THETA ** (-jnp.arange(0, HEAD_DIM, 2, dtype=F32) / HEAD_DIM)
    ang = jnp.arange(s, dtype=F32)[:, None] * inv_freq[None, :]
    cos, sin = jnp.cos(ang), jnp.sin(ang)
    return jnp.concatenate([cos, cos], axis=1), jnp.concatenate([-sin, sin], axis=1)


def _split_bf16_pair(w, width):
    hi = w.astype(BF16)
    lo = (w - hi.astype(F32)).astype(BF16)
    pad = lambda a: jnp.pad(a, ((0, 0), (0, width - a.shape[1])))
    return jnp.concatenate([pad(hi), pad(lo)], axis=1)


def _layer(x2, layer, depth, w_in, b_forget, lam_q1, lam_k1, lam_q2, lam_k2, diff_norm_g, w_proj_diff,
           w_proj_fox, w_out, ln1_g, ln1_b, w_rg, b_rg, w_re, b_re, w_gate, w_up, w_down, ln2_g, ln2_b):
    s, d = x2.shape
    alpha = (2 * depth) ** 0.25
    lam_init = 0.8 - 0.6 * math.exp(-0.3 * layer)
    n_diff, n_fox = diff_norm_g.shape[0], b_forget.shape[0]
    qk_cols = n_diff * 2 * HEAD_DIM
    lin_cols = n_diff * DIFF_V_DIM + 3 * n_fox * HEAD_DIM
    q_scale = HEAD_DIM ** -0.5 * LOG2E

    f0 = 2 * qk_cols + lin_cols
    w_t = w_in.T
    fcum, x_bf = _forget_cumsum(x2, w_t[f0:f0 + n_fox].T, b_forget)

    cos_t, sin_t = _rope_tables(s)
    tm = min(PROJ_TM, s)
    tab = pl.BlockSpec((tm, HEAD_DIM), lambda j, i: (i, 0))
    assert qk_cols == PROJ_TN, "column block 0 of the rotary call must be exactly the queries"
    qk = _proj(x_bf, w_t, 0, 2 * qk_cols, "rope", (cos_t, sin_t), (tab, tab), q_scale)
    col_scale = jnp.ones((1, lin_cols), F32).at[:, n_diff * DIFF_V_DIM:n_diff * DIFF_V_DIM + n_fox * HEAD_DIM].set(q_scale)
    lin = _proj(x_bf, w_t, 2 * qk_cols, lin_cols, "scale", (col_scale,),
                (pl.BlockSpec((1, min(PROJ_TN, lin_cols)), lambda j, i: (0, j)),))
    gates = _proj(x_bf, w_t, f0 + n_fox, w_in.shape[1] - f0 - n_fox, "sigmoid")

    lam_rows = jnp.stack([lam_q1, lam_k1, lam_q2, lam_k2]).astype(F32)
    n_maps = 2 * n_diff
    o_maps = _attention(qk, qk, lin, 0, n_maps, lambda h: h // 2, n_maps, DIFF_V_DIM, v_share=2)
    fq_col = n_diff * DIFF_V_DIM // HEAD_DIM
    o_fox = _attention(lin, lin, lin, fq_col, fq_col + n_fox, lambda h: fq_col + 2 * n_fox + h, n_fox, HEAD_DIM, fcum)

    merged = _merge(o_maps, o_fox, gates, lam_rows, diff_norm_g.astype(F32), w_proj_diff.astype(BF16),
                    w_proj_fox.astype(BF16), lam_init)
    w_router = jnp.concatenate([w_rg, jnp.moveaxis(w_re, 0, 1).reshape(d, N_EXPERTS)], axis=1)
    b_router = jnp.pad(jnp.concatenate([b_rg, b_re.reshape(N_EXPERTS)]).astype(F32),
                       (0, LANES - N_GROUPS - N_EXPERTS)).reshape(1, LANES)
    x1, ids, wts = _out_ln_route(merged, x2, w_out.astype(BF16), ln1_g.reshape(1, d), ln1_b.reshape(1, d),
                                 _split_bf16_pair(w_router, LANES), b_router, alpha)

    block = MOE_BLOCK
    n_blocks = -(-(s * TOP_K) // block) + N_EXPERTS
    dest, bexp, meta = _rank(ids, block, n_blocks)
    tmr = min(ROW_TM, s)
    dest2 = dest[:, :TOP_K]
    inv = _invert(dest2.reshape(s * TOP_K), n_blocks * block)
    ys = _experts(x1, inv, bexp[:, :3].T, meta[0, :1], w_gate, w_up, w_down, block)
    dest3 = dest2.reshape(s // tmr, 1, TOP_K * tmr)
    return _combine(ys, dest3, x1, wts, ln2_g.reshape(1, d), ln2_b.reshape(1, d), alpha)


def kernel(x, w_in, b_forget, lam_q1, lam_k1, lam_q2, lam_k2, diff_norm_g, w_proj_diff, w_proj_fox, w_out,
           ln1_g, ln1_b, w_router_group, b_router_group, w_router_expert, b_router_expert, w_gate, w_up,
           w_down, ln2_g, ln2_b):
    batch, s, d = x.shape
    depth = w_in.shape[0]
    params = (w_in, b_forget, lam_q1, lam_k1, lam_q2, lam_k2, diff_norm_g, w_proj_diff, w_proj_fox, w_out,
              ln1_g, ln1_b, w_router_group, b_router_group, w_router_expert, b_router_expert, w_gate, w_up,
              w_down, ln2_g, ln2_b)
    outs = []
    for bi in range(batch):
        h = x[bi]
        for layer in range(depth):
            h = _layer(h, layer, depth, *(p[layer] for p in params))
        outs.append(h)
    return jnp.stack(outs)
```
